```python
import jax, jax.numpy as jnp
from jax import lax
import numpy as np

D_MODEL = 2048
BATCH = 4
SEQ = 2048
DEPTH = 4
DEC_BATCH = 128
DEC_SEQ = 1
PAST_LEN = 8192
PAGE_SIZE = 128

N_A_LAYERS = DEPTH // 2
N_B_LAYERS = DEPTH - N_A_LAYERS
CONV_CH = D_MODEL
CONV_W = 31
N_MEM = 256
MEM_HEADS = 4
MEM_HEAD_DIM = 256
MEM_W = MEM_HEADS * MEM_HEAD_DIM
MLA_HEADS = 16
Q_LORA = 512
KV_LORA = 512
NOPE_DIM = 128
ROPE_DIM = 64
QK_DIM = NOPE_DIM + ROPE_DIM
V_DIM = 128
ROPE_THETA = 10000.0
Q_BLOCK = 128
N_GROUPS = 8
EXPERTS_PER_GROUP = 8
N_EXPERTS = N_GROUPS * EXPERTS_PER_GROUP
TOP_K = 2
D_EXPERT = 512
MOE_BLOCK = 128
EPS = 1e-6

kernel_name = 'yoco_conformer_mla_hmoe_step'


def rmsnorm(x, g):
    xf = x.astype(jnp.float32)
    y = xf * lax.rsqrt(jnp.mean(xf * xf, axis=-1, keepdims=True) + EPS)
    return (y * g.astype(jnp.float32)).astype(x.dtype)


def layernorm(x, g, b):
    xf = x.astype(jnp.float32)
    xc = xf - jnp.mean(xf, axis=-1, keepdims=True)
    y = xc * lax.rsqrt(jnp.mean(xc * xc, axis=-1, keepdims=True) + EPS)
    return (y * g.astype(jnp.float32) + b.astype(jnp.float32)).astype(x.dtype)


def rope(x, pos):
    half = ROPE_DIM // 2
    inv = ROPE_THETA ** (-jnp.arange(half, dtype=jnp.float32) / half)
    ang = pos.astype(jnp.float32)[:, None] * inv[None, :]
    shape = (ang.shape[0],) + (1,) * (x.ndim - 3) + (half,)
    cos = jnp.cos(ang).reshape(shape)
    sin = jnp.sin(ang).reshape(shape)
    x1 = x[..., :half].astype(jnp.float32)
    x2 = x[..., half:].astype(jnp.float32)
    return jnp.concatenate([x1 * cos - x2 * sin, x1 * sin + x2 * cos], axis=-1).astype(x.dtype)


def memory_kv(mem, g_mem_l, w_mem_kv_l, g_mk_l):
    b, n, _ = mem.shape
    kv = jnp.einsum('bnd,de->bne', rmsnorm(mem, g_mem_l), w_mem_kv_l)
    kv = kv.reshape(b, n, 2, MEM_HEADS, MEM_HEAD_DIM)
    return rmsnorm(kv[:, :, 0], g_mk_l), kv[:, :, 1]


def memory_attend(q_raw, mem_k, mem_v, g_mq_l):
    b, s, _ = q_raw.shape
    q = rmsnorm(q_raw.reshape(b, s, MEM_HEADS, MEM_HEAD_DIM), g_mq_l)
    sc = jnp.einsum('bshd,bnhd->bhsn', q, mem_k).astype(jnp.float32) * (MEM_HEAD_DIM ** -0.5)
    p = jax.nn.softmax(sc, axis=-1).astype(mem_v.dtype)
    return jnp.einsum('bhsn,bnhd->bshd', p, mem_v).reshape(b, s, MEM_W)


def conformer_conv(u, state, conv_w_l, conv_b_l, ln_g_l, ln_b_l):
    a = u[..., :CONV_CH] * jax.nn.sigmoid(u[..., CONV_CH:])
    buf = jnp.concatenate([state.astype(a.dtype), a], axis=1)
    y = lax.conv_general_dilated(buf, conv_w_l[:, None, :].astype(a.dtype), window_strides=(1,),
                                 padding='VALID', dimension_numbers=('NWC', 'WIO', 'NWC'),
                                 feature_group_count=CONV_CH)
    y = jax.nn.silu(layernorm(y + conv_b_l, ln_g_l, ln_b_l))
    return y, buf[:, -(CONV_W - 1):]


def mixer_a(h, conv_state, mem_k, mem_v, w_in, cw, cb, lg, lb, w_out, g_mq_l):
    u = jnp.einsum('bsd,de->bse', h, w_in)
    y_conv, new_state = conformer_conv(u[..., :2 * CONV_CH], conv_state, cw, cb, lg, lb)
    y_mem = memory_attend(u[..., 2 * CONV_CH:], mem_k, mem_v, g_mq_l)
    y = jnp.einsum('bse,ed->bsd', jnp.concatenate([y_conv, y_mem], axis=-1), w_out)
    return y, new_state


def shared_latent(x, pos, g_kv_in, w_kv_down, g_kv_lat, w_uk, g_k):
    ck = jnp.einsum('bsd,de->bse', rmsnorm(x, g_kv_in), w_kv_down)
    c = rmsnorm(ck[..., :KV_LORA], g_kv_lat)
    kr = ck[..., KV_LORA:]
    k_nope = jnp.einsum('bsc,chn->bshn', c, w_uk)
    ssq = (jnp.sum(jnp.square(k_nope.astype(jnp.float32)), axis=-1)
           + jnp.sum(jnp.square(kr.astype(jnp.float32)), axis=-1)[..., None])
    kscale = lax.rsqrt(ssq / QK_DIM + EPS).astype(x.dtype)
    krope = rope(kr * g_k[NOPE_DIM:], pos)
    return c, krope, kscale, k_nope


def mla_queries(u, pos, g_qlat_l, w_q_up_l, g_q_l):
    cq = rmsnorm(u[..., :Q_LORA], g_qlat_l)
    q = rmsnorm(jnp.einsum('bsc,che->bshe', cq, w_q_up_l), g_q_l)
    return q[..., :NOPE_DIM], rope(q[..., NOPE_DIM:], pos)


def mla_attend_prompt(q, k, v):
    b, s, h, _ = q.shape
    nb = s // Q_BLOCK
    qb = q.reshape(b, nb, Q_BLOCK, h, QK_DIM).transpose(1, 0, 2, 3, 4)
    kpos = jnp.arange(s)

    def block(args):
        qi, i = args
        sc = jnp.einsum('bqhd,bkhd->bhqk', qi, k).astype(jnp.float32) * (QK_DIM ** -0.5)
        qpos = i * Q_BLOCK + jnp.arange(Q_BLOCK)
        sc = jnp.where(kpos[None, :] <= qpos[:, None], sc, -jnp.inf)
        p = jax.nn.softmax(sc, axis=-1).astype(v.dtype)
        return jnp.einsum('bhqk,bkhv->bqhv', p, v)

    o = lax.map(block, (qb, jnp.arange(nb)))
    return o.transpose(1, 0, 2, 3, 4).reshape(b, s, h * V_DIM)


def mla_attend_sample(q_nope, q_rope, c_new, kr_new, ks_new, cache_latent, cache_krope, cache_kscale,
                      page_table, w_uk, w_uv, g_k):
    db, t = q_nope.shape[:2]
    q_lat = jnp.einsum('bthn,chn->bthc', q_nope * g_k[:NOPE_DIM], w_uk)
    causal_new = jnp.arange(t)[None, :] <= jnp.arange(t)[:, None]

    def per_seq(args):
        ql, qr, cn, krn, ksn, pages = args
        c = jnp.concatenate([cache_latent[pages].reshape(-1, KV_LORA), cn], axis=0)
        kr = jnp.concatenate([cache_krope[pages].reshape(-1, ROPE_DIM), krn], axis=0)
        ks = jnp.concatenate([cache_kscale[pages].reshape(-1, MLA_HEADS), ksn], axis=0)
        sc = (jnp.einsum('thc,kc->thk', ql, c) + jnp.einsum('thr,kr->thk', qr, kr)).astype(jnp.float32)
        sc = sc * ks.T[None].astype(jnp.float32) * (QK_DIM ** -0.5)
        n_past = c.shape[0] - t
        mask = jnp.concatenate([jnp.ones((t, n_past), dtype=bool), causal_new], axis=1)
        sc = jnp.where(mask[:, None, :], sc, -jnp.inf)
        p = jax.nn.softmax(sc, axis=-1).astype(c.dtype)
        o_lat = jnp.einsum('thk,kc->thc', p, c)
        return jnp.einsum('thc,chv->thv', o_lat, w_uv)

    o = lax.map(per_seq, (q_lat, q_rope, c_new, kr_new, ks_new, page_table))
    return o.reshape(db, t, MLA_HEADS * V_DIM)


def moe_ffn(h, w_rg_l, w_re_l, w_gate_l, w_up_l, w_down_l):
    lead = h.shape[:-1]
    hf = h.reshape(-1, D_MODEL)
    m = hf.shape[0]
    pg = jax.nn.softmax(jnp.einsum('md,dg->mg', hf, w_rg_l).astype(jnp.float32), axis=-1)
    grp = jnp.argmax(pg, axis=-1)
    p_grp = jnp.take_along_axis(pg, grp[:, None], axis=-1)
    le = jnp.einsum('md,de->me', hf, w_re_l).astype(jnp.float32).reshape(m, N_GROUPS, EXPERTS_PER_GROUP)
    le = jnp.take_along_axis(le, grp[:, None, None], axis=1)[:, 0]
    top_p, top_i = lax.top_k(jax.nn.softmax(le, axis=-1), TOP_K)
    gate = p_grp * top_p / jnp.sum(top_p, axis=-1, keepdims=True)
    expert = grp[:, None].astype(jnp.int32) * EXPERTS_PER_GROUP + top_i.astype(jnp.int32)
    n_slots = m * TOP_K
    e_flat = expert.reshape(-1)
    tok = jnp.repeat(jnp.arange(m, dtype=jnp.int32), TOP_K)
    g_flat = gate.reshape(-1)
    order = jnp.argsort(e_flat)
    e_s, tok_s, g_s = e_flat[order], tok[order], g_flat[order]
    counts = jax.ops.segment_sum(jnp.ones_like(e_flat), e_flat, num_segments=N_EXPERTS)
    starts = jnp.cumsum(counts) - counts
    padded = (counts + MOE_BLOCK - 1) // MOE_BLOCK * MOE_BLOCK
    pad_ends = jnp.cumsum(padded)
    pad_starts = pad_ends - padded
    dest = pad_starts[e_s] + jnp.arange(n_slots, dtype=jnp.int32) - starts[e_s]
    n_blocks = (n_slots + N_EXPERTS * (MOE_BLOCK - 1) + MOE_BLOCK - 1) // MOE_BLOCK
    xd = jnp.zeros((n_blocks * MOE_BLOCK, D_MODEL), h.dtype).at[dest].set(hf[tok_s])
    blk_e = jnp.minimum(jnp.searchsorted(pad_ends, jnp.arange(n_blocks, dtype=jnp.int32) * MOE_BLOCK,
                                         side='right'), N_EXPERTS - 1)

    def expert_block(args):
        xb, e = args
        a = jnp.einsum('td,df->tf', xb, w_gate_l[e])
        u = jnp.einsum('td,df->tf', xb, w_up_l[e])
        return jnp.einsum('tf,fd->td', jax.nn.silu(a) * u, w_down_l[e])

    yd = lax.map(expert_block, (xd.reshape(n_blocks, MOE_BLOCK, D_MODEL), blk_e)).reshape(-1, D_MODEL)
    y = jnp.zeros_like(hf).at[tok_s].add(yd[dest] * g_s[:, None].astype(h.dtype))
    return y.reshape(lead + (D_MODEL,))


def setup_inputs(seed: int = 0) -> dict:
    key = jax.random.key(seed)
    keys = iter(jax.random.split(key, 64))
    f32 = jnp.float32

    def nrm(shape, scale=1.0):
        return jax.random.normal(next(keys), shape, f32) * scale

    def gain(shape):
        return 1.0 + nrm(shape, 0.02)

    n_pages = PAST_LEN // PAGE_SIZE
    n_pool = (DEC_BATCH * n_pages * 5 + 3) // 4
    page_table = jax.random.permutation(next(keys), n_pool)[:DEC_BATCH * n_pages]
    page_table = page_table.reshape(DEC_BATCH, n_pages).astype(jnp.int32)
    na, nb = N_A_LAYERS, N_B_LAYERS
    return {
        'x_prompt': nrm((BATCH, SEQ, D_MODEL)),
        'x_sample': nrm((DEC_BATCH, DEC_SEQ, D_MODEL)),
        'cache_latent': nrm((n_pool, PAGE_SIZE, KV_LORA)),
        'cache_krope': nrm((n_pool, PAGE_SIZE, ROPE_DIM)),
        'cache_kscale': jax.random.uniform(next(keys), (n_pool, PAGE_SIZE, MLA_HEADS), f32, 0.8, 1.2),
        'cache_mem_k': nrm((DEPTH, DEC_BATCH, N_MEM, MEM_HEADS, MEM_HEAD_DIM)),
        'cache_mem_v': nrm((DEPTH, DEC_BATCH, N_MEM, MEM_HEADS, MEM_HEAD_DIM)),
        'state_conv': nrm((na, DEC_BATCH, CONV_W - 1, CONV_CH), 0.5),
        'page_table': page_table,
        'mem_prompt': nrm((BATCH, N_MEM, D_MODEL)),
        'g_mix': gain((DEPTH, D_MODEL)),
        'g_ffn': gain((DEPTH, D_MODEL)),
        'g_mem': gain((DEPTH, D_MODEL)),
        'w_mem_kv': nrm((DEPTH, D_MODEL, 2 * MEM_W), D_MODEL ** -0.5),
        'g_mq': gain((DEPTH, MEM_HEAD_DIM)),
        'g_mk': gain((DEPTH, MEM_HEAD_DIM)),
        'w_in_a': nrm((na, D_MODEL, 2 * CONV_CH + MEM_W), D_MODEL ** -0.5),
        'conv_w': nrm((na, CONV_W, CONV_CH), CONV_W ** -0.5),
        'conv_b': nrm((na, CONV_CH), 0.02),
        'ln_g': gain((na, CONV_CH)),
        'ln_b': nrm((na, CONV_CH), 0.02),
        'w_out_a': nrm((na, CONV_CH + MEM_W, D_MODEL), (CONV_CH + MEM_W) ** -0.5),
        'g_kv_in': gain((D_MODEL,)),
        'w_kv_down': nrm((D_MODEL, KV_LORA + ROPE_DIM), D_MODEL ** -0.5),
        'g_kv_lat': gain((KV_LORA,)),
        'w_uk': nrm((KV_LORA, MLA_HEADS, NOPE_DIM), KV_LORA ** -0.5),
        'w_uv': nrm((KV_LORA, MLA_HEADS, V_DIM), KV_LORA ** -0.5),
        'g_k': gain((QK_DIM,)),
        'w_in_b': nrm((nb, D_MODEL, Q_LORA + MEM_W), D_MODEL ** -0.5),
        'g_qlat': gain((nb, Q_LORA)),
        'w_q_up': nrm((nb, Q_LORA, MLA_HEADS, QK_DIM), Q_LORA ** -0.5),
        'g_q': gain((nb, QK_DIM)),
        'w_out_b': nrm((nb, MLA_HEADS * V_DIM + MEM_W, D_MODEL), (MLA_HEADS * V_DIM + MEM_W) ** -0.5),
        'w_rg': nrm((DEPTH, D_MODEL, N_GROUPS), D_MODEL ** -0.5),
        'w_re': nrm((DEPTH, D_MODEL, N_EXPERTS), D_MODEL ** -0.5),
        'w_gate': nrm((DEPTH, N_EXPERTS, D_MODEL, D_EXPERT), D_MODEL ** -0.5),
        'w_up': nrm((DEPTH, N_EXPERTS, D_MODEL, D_EXPERT), D_MODEL ** -0.5),
        'w_down': nrm((DEPTH, N_EXPERTS, D_EXPERT, D_MODEL), D_EXPERT ** -0.5),
    }


def reference(x_prompt, x_sample, cache_latent, cache_krope, cache_kscale, cache_mem_k, cache_mem_v,
              state_conv, page_table, mem_prompt,
              g_mix, g_ffn, g_mem, w_mem_kv, g_mq, g_mk,
              w_in_a, conv_w, conv_b, ln_g, ln_b, w_out_a,
              g_kv_in, w_kv_down, g_kv_lat, w_uk, w_uv, g_k,
              w_in_b, g_qlat, w_q_up, g_q, w_out_b,
              w_rg, w_re, w_gate, w_up, w_down):
    b, s = x_prompt.shape[:2]
    t = x_sample.shape[1]
    pos_p = jnp.arange(s)
    pos_s = PAST_LEN + jnp.arange(t)
    xp, xs = x_prompt, x_sample
    mem_k_list, mem_v_list, conv_p_list, conv_s_list = [], [], [], []
    for l in range(DEPTH):
        mk_p, mv_p = memory_kv(mem_prompt, g_mem[l], w_mem_kv[l], g_mk[l])
        mem_k_list.append(mk_p)
        mem_v_list.append(mv_p)
        hp = rmsnorm(xp, g_mix[l])
        hs = rmsnorm(xs, g_mix[l])
        if l < N_A_LAYERS:
            zero_state = jnp.zeros((b, CONV_W - 1, CONV_CH), xp.dtype)
            yp, st_p = mixer_a(hp, zero_state, mk_p, mv_p, w_in_a[l], conv_w[l], conv_b[l],
                               ln_g[l], ln_b[l], w_out_a[l], g_mq[l])
            ys, st_s = mixer_a(hs, state_conv[l], cache_mem_k[l], cache_mem_v[l], w_in_a[l], conv_w[l],
                               conv_b[l], ln_g[l], ln_b[l], w_out_a[l], g_mq[l])
            conv_p_list.append(st_p)
            conv_s_list.append(st_s)
        else:
            j = l - N_A_LAYERS
            up = jnp.einsum('bsd,de->bse', hp, w_in_b[j])
            qn_p, qr_p = mla_queries(up, pos_p, g_qlat[j], w_q_up[j], g_q[j])
            att_p = mla_attend_prompt(jnp.concatenate([qn_p, qr_p], axis=-1), k_full, v_full)
            mem_p = memory_attend(up[..., Q_LORA:], mk_p, mv_p, g_mq[l])
            yp = jnp.einsum('bse,ed->bsd', jnp.concatenate([att_p, mem_p], axis=-1), w_out_b[j])
            us = jnp.einsum('bsd,de->bse', hs, w_in_b[j])
            qn_s, qr_s = mla_queries(us, pos_s, g_qlat[j], w_q_up[j], g_q[j])
            att_s = mla_attend_sample(qn_s, qr_s, c_s, krope_s, kscale_s, cache_latent, cache_krope,
                                      cache_kscale, page_table, w_uk, w_uv, g_k)
            mem_s = memory_attend(us[..., Q_LORA:], cache_mem_k[l], cache_mem_v[l], g_mq[l])
            ys = jnp.einsum('bse,ed->bsd', jnp.concatenate([att_s, mem_s], axis=-1), w_out_b[j])
        xp = xp + yp
        xs = xs + ys
        xp = xp + moe_ffn(rmsnorm(xp, g_ffn[l]), w_rg[l], w_re[l], w_gate[l], w_up[l], w_down[l])
        xs = xs + moe_ffn(rmsnorm(xs, g_ffn[l]), w_rg[l], w_re[l], w_gate[l], w_up[l], w_down[l])
        if l == N_A_LAYERS - 1:
            c_p, krope_p, kscale_p, knope_p = shared_latent(xp, pos_p, g_kv_in, w_kv_down, g_kv_lat, w_uk, g_k)
            k_full = jnp.concatenate(
                [knope_p * g_k[:NOPE_DIM],
                 jnp.broadcast_to(krope_p[:, :, None, :], knope_p.shape[:3] + (ROPE_DIM,))],
                axis=-1) * kscale_p[..., None]
            v_full = jnp.einsum('bsc,chv->bshv', c_p, w_uv)
            c_s, krope_s, kscale_s, _ = shared_latent(xs, pos_s, g_kv_in, w_kv_down, g_kv_lat, w_uk, g_k)
    new_mem_k = jnp.stack(mem_k_list, axis=0)
    new_mem_v = jnp.stack(mem_v_list, axis=0)
    new_conv_p = jnp.stack(conv_p_list, axis=0)
    new_conv_s = jnp.stack(conv_s_list, axis=0)
    return (xp, xs, c_p, krope_p, kscale_p, new_mem_k, new_mem_v, new_conv_p,
            c_s, krope_s, kscale_s, new_conv_s)
```

```python
import functools
import math

import jax
import jax.numpy as jnp
from jax import lax
from jax.experimental import pallas as pl
from jax.experimental.pallas import tpu as pltpu

F32 = jnp.float32
BF = jnp.bfloat16

D_MODEL = 2048
SEQ = 2048
PAST_LEN = 8192
PAGE_SIZE = 128
CONV_W = 31
N_MEM = 256
MEM_HEADS = 4
MEM_HEAD_DIM = 256
MEM_W = MEM_HEADS * MEM_HEAD_DIM
MLA_HEADS = 16
Q_LORA = 512
KV_LORA = 512
NOPE_DIM = 128
ROPE_DIM = 64
QK_DIM = NOPE_DIM + ROPE_DIM
V_DIM = 128
ROPE_THETA = 10000.0
N_GROUPS = 8
EXPERTS_PER_GROUP = 8
N_EXPERTS = 64
D_EXPERT = 512
MOE_BLOCK = 128
EPS = 1e-6

HEAD_PAD = 256
LANES = 128
VMEM_LIMIT_BYTES = 56 * 1024 * 1024
NEG_BIG = -1e30
NT_DIMS = (((1,), (1,)), ((), ()))


def _params(sem):
    return pltpu.CompilerParams(dimension_semantics=sem, vmem_limit_bytes=VMEM_LIMIT_BYTES)


def _mm_kernel(*refs, nx, w_x, has_gain, n_extra, n_out, epi, tm):
    x_refs = refs[:nx]
    w_refs = refs[nx:nx + len(w_x)]
    pos = nx + len(w_x)
    g_ref = refs[pos] if has_gain else None
    pos += int(has_gain)
    extra_refs = refs[pos:pos + n_extra]
    pos += n_extra
    out_refs = refs[pos:pos + n_out]
    pos += n_out
    h_s = refs[pos] if has_gain else None
    j = pl.program_id(1)

    if has_gain:
        chunk = min(tm, 128)

        @pl.when(j == 0)
        def _():
            def body(c, carry):
                r = pl.multiple_of(c * chunk, chunk)
                xf = x_refs[0][pl.ds(r, chunk), :].astype(F32)
                ms = jnp.mean(xf * xf, axis=-1, keepdims=True)
                h_s[pl.ds(r, chunk), :] = (xf * lax.rsqrt(ms + EPS) * g_ref[...]).astype(BF)
                return carry
            lax.fori_loop(0, tm // chunk, body, 0)

    ds = []
    for wi, xi in enumerate(w_x):
        if has_gain and xi == 0:
            lhs = h_s[...]
        else:
            lhs = x_refs[xi][...].astype(BF)
        ds.append(jnp.dot(lhs, w_refs[wi][...].astype(BF), preferred_element_type=F32))
    epi(ds, extra_refs, out_refs, j)


def _fused_mm(name, xs, ws, *, m, tm, nj, gain=None, extras=(), outs, epi):
    in_arrays, in_specs = [], []
    for arr, k, cb in xs:
        in_arrays.append(arr)
        in_specs.append(pl.BlockSpec((tm, k), lambda i, j, cb=cb: (i, cb)))
    for xi, arr, layer, k, tn, rb, cb0 in ws:
        in_arrays.append(arr)
        if layer is None:
            in_specs.append(pl.BlockSpec((k, tn), lambda i, j, rb=rb, cb0=cb0: (rb, cb0 + j)))
        else:
            in_specs.append(pl.BlockSpec((None, k, tn),
                                         lambda i, j, l=layer, rb=rb, cb0=cb0: (l, rb, cb0 + j)))
    scratch = []
    if gain is not None:
        k0 = xs[0][1]
        in_arrays.append(gain.reshape(1, k0).astype(F32))
        in_specs.append(pl.BlockSpec((1, k0), lambda i, j: (0, 0)))
        scratch.append(pltpu.VMEM((tm, k0), BF))
    for arr, spec in extras:
        in_arrays.append(arr)
        in_specs.append(spec)
    kern = functools.partial(_mm_kernel, nx=len(xs), w_x=tuple(w[0] for w in ws),
                             has_gain=gain is not None, n_extra=len(extras),
                             n_out=len(outs), epi=epi, tm=tm)
    res = pl.pallas_call(
        kern,
        grid=(m // tm, nj),
        in_specs=in_specs,
        out_specs=[o[1] for o in outs],
        out_shape=[o[0] for o in outs],
        scratch_shapes=scratch,
        compiler_params=_params(("arbitrary", "arbitrary")),
        name=name,
    )(*in_arrays)
    return res


def _epi_plain(ds, ex, outs, j):
    acc = ds[0]
    for d in ds[1:]:
        acc = acc + d
    outs[0][...] = acc.astype(outs[0].dtype)


def _epi_residual(ds, ex, outs, j):
    acc = ds[0]
    for d in ds[1:]:
        acc = acc + d
    outs[0][...] = (ex[0][...] + acc).astype(outs[0].dtype)


def _epi_split(ds, ex, outs, j):
    @pl.when(j == 0)
    def _():
        outs[0][...] = ds[0]

    @pl.when(j > 0)
    def _():
        outs[1][...] = ds[0]


def _epi_glu(ds, ex, outs, j):
    outs[0][...] = (ds[0] * jax.nn.sigmoid(ds[1])).astype(outs[0].dtype)


def _rope_tile(t, a_ref, b_ref, c_ref):
    return (t * a_ref[...] + pltpu.roll(t, 32, 1) * b_ref[...]
            + pltpu.roll(t, 96, 1) * c_ref[...])


def _epi_qhead(ds, ex, outs, j):
    acc = ds[0]
    gq_ref, a_ref, b_ref, c_ref = ex
    ssq = jnp.sum(acc * acc, axis=-1, keepdims=True)
    qn = acc * lax.rsqrt(ssq * (1.0 / QK_DIM) + EPS) * gq_ref[...]
    outs[0][:, :LANES] = qn[:, :LANES].astype(outs[0].dtype)
    outs[0][:, LANES:] = _rope_tile(qn[:, LANES:], a_ref, b_ref, c_ref).astype(outs[0].dtype)


def _epi_latent(ds, ex, outs, j):
    ck = ds[0]
    glat_ref, gkr_ref, a_ref, b_ref, c_ref = ex
    c_raw = ck[:, :KV_LORA]
    ms = jnp.mean(c_raw * c_raw, axis=-1, keepdims=True)
    outs[0][...] = c_raw * lax.rsqrt(ms + EPS) * glat_ref[...]
    krt = ck[:, KV_LORA:]
    ssq = jnp.sum(krt * krt, axis=-1, keepdims=True)
    outs[1][...] = _rope_tile(krt * gkr_ref[...], a_ref, b_ref, c_ref)
    outs[2][...] = jnp.broadcast_to(ssq, outs[2].shape)


def _epi_knope(ds, ex, outs, j):
    acc = ds[0]
    kr_ref, ssq_ref, gkn_ref = ex
    ssq_r = ssq_ref[:, 0:1]
    kr = kr_ref[...]
    lane = lax.broadcasted_iota(jnp.int32, outs[1].shape, 1)

    @pl.when(j == 0)
    def _():
        outs[1][...] = jnp.zeros(outs[1].shape, F32)

    ks_all = outs[1][...]
    for t in range(2):
        kn = acc[:, t * NOPE_DIM:(t + 1) * NOPE_DIM]
        ssq = jnp.sum(kn * kn, axis=-1, keepdims=True) + ssq_r
        ks = lax.rsqrt(ssq * (1.0 / QK_DIM) + EPS)
        outs[0][:, t * HEAD_PAD:t * HEAD_PAD + LANES] = (kn * gkn_ref[...] * ks).astype(outs[0].dtype)
        outs[0][:, t * HEAD_PAD + LANES:(t + 1) * HEAD_PAD] = (kr * ks).astype(outs[0].dtype)
        ks_all = jnp.where(lane == 2 * j + t, ks, ks_all)
    outs[1][...] = ks_all


def _memkv_kernel(x_ref, g_ref, wk_ref, wv_ref, gk_ref, k_out, v_out, h_s):
    j = pl.program_id(1)
    rows = x_ref.shape[0]
    chunk = 128

    @pl.when(j == 0)
    def _():
        def body(c, carry):
            r = pl.multiple_of(c * chunk, chunk)
            xf = x_ref[pl.ds(r, chunk), :]
            ms = jnp.mean(xf * xf, axis=-1, keepdims=True)
            h_s[pl.ds(r, chunk), :] = (xf * lax.rsqrt(ms + EPS) * g_ref[...]).astype(BF)
            return carry
        lax.fori_loop(0, rows // chunk, body, 0)

    h = h_s[...]
    k = jnp.dot(h, wk_ref[...].astype(BF), preferred_element_type=F32)
    ms = jnp.mean(k * k, axis=-1, keepdims=True)
    k_out[...] = k * lax.rsqrt(ms + EPS) * gk_ref[...]
    v_out[...] = jnp.dot(h, wv_ref[...].astype(BF), preferred_element_type=F32)


def _memory_kv_all(mem2d, g_mem, w_mem_kv, g_mk):
    depth = w_mem_kv.shape[0]
    rows = mem2d.shape[0]
    hd = MEM_HEAD_DIM
    out = pl.pallas_call(
        _memkv_kernel,
        grid=(depth, MEM_HEADS),
        in_specs=[
            pl.BlockSpec((rows, D_MODEL), lambda l, j: (0, 0)),
            pl.BlockSpec((None, 1, D_MODEL), lambda l, j: (l, 0, 0)),
            pl.BlockSpec((None, D_MODEL, hd), lambda l, j: (l, 0, j)),
            pl.BlockSpec((None, D_MODEL, hd), lambda l, j: (l, 0, MEM_HEADS + j)),
            pl.BlockSpec((None, 1, hd), lambda l, j: (l, 0, 0)),
        ],
        out_specs=[
            pl.BlockSpec((None, rows, hd), lambda l, j: (l, 0, j)),
            pl.BlockSpec((None, rows, hd), lambda l, j: (l, 0, j)),
        ],
        out_shape=[jax.ShapeDtypeStruct((depth, rows, MEM_W), F32)] * 2,
        scratch_shapes=[pltpu.VMEM((rows, D_MODEL), BF)],
        compiler_params=_params(("arbitrary", "arbitrary")),
        name="memory_kv",
    )(mem2d, g_mem.reshape(depth, 1, D_MODEL), w_mem_kv, w_mem_kv, g_mk.reshape(depth, 1, hd))
    return out


CONV_TT = 256
CONV_RC = 16
CONV_HALO = 32
SUBLANES = 8


def _conv_prompt_kernel(a_ref, w_ref, cb_ref, lg_ref, lb_ref, o_ref, sh):
    t = pl.program_id(1)
    tt = CONV_TT
    span = tt + CONV_HALO

    @pl.when(t == 0)
    def _():
        sh[0, 0:CONV_HALO, :] = jnp.zeros((CONV_HALO, D_MODEL), F32)
        sh[0, span:span + SUBLANES, :] = jnp.zeros((SUBLANES, D_MODEL), F32)

    sh[0, CONV_HALO:span, :] = a_ref[...]
    for p in range(1, SUBLANES):
        sh[p, 0:span, :] = sh[0, p:p + span, :]

    def body(c, carry):
        r = c * CONV_RC
        acc = jnp.zeros((CONV_RC, D_MODEL), F32)
        for k in range(CONV_W):
            off = 2 + k
            p = off % SUBLANES
            start = pl.multiple_of(r + (off - p), SUBLANES)
            acc = acc + sh[p, pl.ds(start, CONV_RC), :] * w_ref[k:k + 1, :]
        y = acc + cb_ref[...]
        mu = jnp.mean(y, axis=-1, keepdims=True)
        yc = y - mu
        var = jnp.mean(yc * yc, axis=-1, keepdims=True)
        z = yc * lax.rsqrt(var + EPS) * lg_ref[...] + lb_ref[...]
        o_ref[pl.ds(pl.multiple_of(r, CONV_RC), CONV_RC), :] = (z * jax.nn.sigmoid(z)).astype(o_ref.dtype)
        return carry

    lax.fori_loop(0, tt // CONV_RC, body, 0)
    sh[0, 0:CONV_HALO, :] = sh[0, tt:span, :]


def _conv_prompt(a, w32, cb, lg, lb, batch):
    m = a.shape[0]
    nt = SEQ // CONV_TT
    vec = lambda v: v.reshape(1, D_MODEL)
    return pl.pallas_call(
        _conv_prompt_kernel,
        grid=(batch, nt),
        in_specs=[
            pl.BlockSpec((CONV_TT, D_MODEL), lambda b, t: (b * nt + t, 0)),
            pl.BlockSpec((32, D_MODEL), lambda b, t: (0, 0)),
            pl.BlockSpec((1, D_MODEL), lambda b, t: (0, 0)),
            pl.BlockSpec((1, D_MODEL), lambda b, t: (0, 0)),
            pl.BlockSpec((1, D_MODEL), lambda b, t: (0, 0)),
        ],
        out_specs=pl.BlockSpec((CONV_TT, D_MODEL), lambda b, t: (b * nt + t, 0)),
        out_shape=jax.ShapeDtypeStruct((m, D_MODEL), BF),
        scratch_shapes=[pltpu.VMEM((SUBLANES, CONV_TT + CONV_HALO + SUBLANES, D_MODEL), F32)],
        compiler_params=_params(("arbitrary", "arbitrary")),
        name="conv_prompt",
    )(a, w32, vec(cb), vec(lg), vec(lb))


CONV_SB = 8


def _conv_sample_kernel(st_ref, a_ref, w_ref, cb_ref, lg_ref, lb_ref, y_ref, ns_ref):
    nstate = CONV_W - 1
    for s in range(CONV_SB):
        st = st_ref[s]
        a = a_ref[s]
        y = (jnp.sum(st * w_ref[0:nstate, :], axis=0, keepdims=True)
             + a * w_ref[nstate:nstate + 1, :] + cb_ref[...])
        mu = jnp.mean(y, axis=-1, keepdims=True)
        yc = y - mu
        var = jnp.mean(yc * yc, axis=-1, keepdims=True)
        z = yc * lax.rsqrt(var + EPS) * lg_ref[...] + lb_ref[...]
        y_ref[s] = (z * jax.nn.sigmoid(z)).astype(y_ref.dtype)
        ns_ref[s, 0:nstate - 1, :] = st_ref[s, 1:nstate, :]
        ns_ref[s, nstate - 1:nstate, :] = a


def _conv_sample(state, a, w32, cb, lg, lb):
    nb = state.shape[0]
    nstate = CONV_W - 1
    vec = lambda v: v.reshape(1, D_MODEL)
    y, ns = pl.pallas_call(
        _conv_sample_kernel,
        grid=(nb // CONV_SB,),
        in_specs=[
            pl.BlockSpec((CONV_SB, nstate, D_MODEL), lambda i: (i, 0, 0)),
            pl.BlockSpec((CONV_SB, 1, D_MODEL), lambda i: (i, 0, 0)),
            pl.BlockSpec((32, D_MODEL), lambda i: (0, 0)),
            pl.BlockSpec((1, D_MODEL), lambda i: (0, 0)),
            pl.BlockSpec((1, D_MODEL), lambda i: (0, 0)),
            pl.BlockSpec((1, D_MODEL), lambda i: (0, 0)),
        ],
        out_specs=[
            pl.BlockSpec((CONV_SB, 1, D_MODEL), lambda i: (i, 0, 0)),
            pl.BlockSpec((CONV_SB, nstate, D_MODEL), lambda i: (i, 0, 0)),
        ],
        out_shape=[jax.ShapeDtypeStruct((nb, 1, D_MODEL), BF),
                   jax.ShapeDtypeStruct((nb, nstate, D_MODEL), F32)],
        compiler_params=_params(("arbitrary",)),
        name="conv_sample",
    )(state, a.reshape(nb, 1, D_MODEL), w32, vec(cb), vec(lg), vec(lb))
    return y.reshape(nb, D_MODEL), ns


MEM_TQ = 512
MEM_SCALE = MEM_HEAD_DIM ** -0.5


def _memattn_prompt_kernel(q_ref, k_ref, v_ref, g_ref, o_ref):
    for h in range(MEM_HEADS):
        sl = slice(h * MEM_HEAD_DIM, (h + 1) * MEM_HEAD_DIM)
        qh = q_ref[:, sl]
        ms = jnp.mean(qh * qh, axis=-1, keepdims=True)
        qn = (qh * lax.rsqrt(ms + EPS) * g_ref[...]).astype(BF)
        s = lax.dot_general(qn, k_ref[:, sl].astype(BF), NT_DIMS,
                            preferred_element_type=F32) * MEM_SCALE
        mx = jnp.max(s, axis=-1, keepdims=True)
        p = jnp.exp(s - mx)
        l = jnp.sum(p, axis=-1, keepdims=True)
        o = jnp.dot(p.astype(BF), v_ref[:, sl].astype(BF), preferred_element_type=F32)
        o_ref[:, sl] = (o / l).astype(o_ref.dtype)


def _memattn_prompt(q, q_colblk, memk, memv, layer, g_mq_l, batch):
    m = q.shape[0]
    nq = SEQ // MEM_TQ
    return pl.pallas_call(
        _memattn_prompt_kernel,
        grid=(batch, nq),
        in_specs=[
            pl.BlockSpec((MEM_TQ, MEM_W), lambda b, i: (b * nq + i, q_colblk)),
            pl.BlockSpec((None, N_MEM, MEM_W), lambda b, i: (layer, b, 0)),
            pl.BlockSpec((None, N_MEM, MEM_W), lambda b, i: (layer, b, 0)),
            pl.BlockSpec((1, MEM_HEAD_DIM), lambda b, i: (0, 0)),
        ],
        out_specs=pl.BlockSpec((MEM_TQ, MEM_W), lambda b, i: (b * nq + i, 0)),
        out_shape=jax.ShapeDtypeStruct((m, MEM_W), BF),
        compiler_params=_params(("arbitrary", "arbitrary")),
        name="memattn_prompt",
    )(q, memk, memv, g_mq_l.reshape(1, MEM_HEAD_DIM))


MEM_SB = 4


def _memattn_sample_kernel(q_ref, k_ref, v_ref, g_ref, o_ref):
    for s in range(MEM_SB):
        for h in range(MEM_HEADS):
            sl = slice(h * MEM_HEAD_DIM, (h + 1) * MEM_HEAD_DIM)
            qh = q_ref[s, :, sl]
            ms = jnp.mean(qh * qh, axis=-1, keepdims=True)
            qn = qh * lax.rsqrt(ms + EPS) * g_ref[...]
            sc = jnp.sum(k_ref[s, :, sl] * qn, axis=-1, keepdims=True) * MEM_SCALE
            mx = jnp.max(sc, axis=0, keepdims=True)
            p = jnp.exp(sc - mx)
            l = jnp.sum(p, axis=0, keepdims=True)
            o = jnp.sum(p * v_ref[s, :, sl], axis=0, keepdims=True) / l
            o_ref[s, :, sl] = o.astype(o_ref.dtype)


def _memattn_sample(q3, q_colblk, cmk, cmv, layer, g_mq_l):
    nb = q3.shape[0]
    out = pl.pallas_call(
        _memattn_sample_kernel,
        grid=(nb // MEM_SB,),
        in_specs=[
            pl.BlockSpec((MEM_SB, 1, MEM_W), lambda i: (i, 0, q_colblk)),
            pl.BlockSpec((None, MEM_SB, N_MEM, MEM_W), lambda i: (layer, i, 0, 0)),
            pl.BlockSpec((None, MEM_SB, N_MEM, MEM_W), lambda i: (layer, i, 0, 0)),
            pl.BlockSpec((1, MEM_HEAD_DIM), lambda i: (0, 0)),
        ],
        out_specs=pl.BlockSpec((MEM_SB, 1, MEM_W), lambda i: (i, 0, 0)),
        out_shape=jax.ShapeDtypeStruct((nb, 1, MEM_W), BF),
        compiler_params=_params(("arbitrary",)),
        name="memattn_sample",
    )(q3, cmk, cmv, g_mq_l.reshape(1, MEM_HEAD_DIM))
    return out.reshape(nb, MEM_W)


FLASH_T = 256
MLA_SCALE = QK_DIM ** -0.5


def _flash_kernel(q_ref, k_ref, v_ref, o_ref):
    qi = pl.program_id(2)
    t = FLASH_T
    q = q_ref[...]
    row = lax.broadcasted_iota(jnp.int32, (t, t), 0)
    col = lax.broadcasted_iota(jnp.int32, (t, t), 1)

    def body(ki, carry):
        m, l, acc = carry
        r = pl.multiple_of(ki * t, t)
        s = lax.dot_general(q, k_ref[pl.ds(r, t), :], NT_DIMS,
                            preferred_element_type=F32) * MLA_SCALE
        s = jnp.where(col + ki * t <= row + qi * t, s, NEG_BIG)
        m_new = jnp.maximum(m, jnp.max(s, axis=-1, keepdims=True))
        alpha = jnp.exp(m - m_new)
        p = jnp.exp(s - m_new)
        l = alpha * l + jnp.sum(p, axis=-1, keepdims=True)
        acc = alpha * acc + jnp.dot(p.astype(BF), v_ref[pl.ds(r, t), :],
                                    preferred_element_type=F32)
        return m_new, l, acc

    m0 = jnp.full((t, 1), NEG_BIG, F32)
    l0 = jnp.zeros((t, 1), F32)
    a0 = jnp.zeros((t, V_DIM), F32)
    m, l, acc = lax.fori_loop(0, qi + 1, body, (m0, l0, a0))
    o_ref[...] = (acc / l).astype(o_ref.dtype)


def _mla_prompt_attention(q, kfull, vfull, batch):
    m = q.shape[0]
    nq = SEQ // FLASH_T
    return pl.pallas_call(
        _flash_kernel,
        grid=(batch, MLA_HEADS, nq),
        in_specs=[
            pl.BlockSpec((FLASH_T, HEAD_PAD), lambda b, h, i: (b * nq + i, h)),
            pl.BlockSpec((SEQ, HEAD_PAD), lambda b, h, i: (b, h)),
            pl.BlockSpec((SEQ, V_DIM), lambda b, h, i: (b, h)),
        ],
        out_specs=pl.BlockSpec((FLASH_T, V_DIM), lambda b, h, i: (b * nq + i, h)),
        out_shape=jax.ShapeDtypeStruct((m, MLA_HEADS * V_DIM), BF),
        compiler_params=_params(("arbitrary", "arbitrary", "arbitrary")),
        name="mla_prompt_attention",
    )(q, kfull, vfull)


def _headwise_kernel(x_ref, w_ref, g_ref, o_ref):
    x = (x_ref[...].astype(F32) * g_ref[...]).astype(BF)
    o_ref[...] = jnp.dot(x, w_ref[...].astype(BF), preferred_element_type=F32).astype(o_ref.dtype)


def _headwise_mm(name, x, x_spec, w, w_spec, g, out_shape, out_spec):
    kx = g.shape[-1]
    return pl.pallas_call(
        _headwise_kernel,
        grid=(MLA_HEADS,),
        in_specs=[x_spec, w_spec, pl.BlockSpec((1, kx), lambda h: (0, 0))],
        out_specs=out_spec,
        out_shape=out_shape,
        compiler_params=_params(("arbitrary",)),
        name=name,
    )(x, w, g)


PAGES_PER_STEP = 8


def _paged_kernel(pt_ref, ql_ref, q_ref, *refs):
    pp = PAGES_PER_STEP
    lat = refs[:pp]
    kr = refs[pp:2 * pp]
    ks = refs[2 * pp:3 * pp]
    cn_ref, krn_ref, ksn_ref, o_ref, m_s, l_s, acc_s = refs[3 * pp:]
    c = pl.program_id(1)
    nc = pl.num_programs(1)

    @pl.when(c == 0)
    def _():
        m_s[...] = jnp.full(m_s.shape, NEG_BIG, F32)
        l_s[...] = jnp.zeros(l_s.shape, F32)
        acc_s[...] = jnp.zeros(acc_s.shape, F32)

    ql = ql_ref[0]
    qr32 = q_ref[0][:, LANES:LANES + ROPE_DIM]
    qr = qr32.astype(BF)
    cps, ss = [], []
    for i in range(pp):
        cp = lat[i][0].astype(BF)
        s = (lax.dot_general(ql, cp, NT_DIMS, preferred_element_type=F32)
             + lax.dot_general(qr, kr[i][0].astype(BF), NT_DIMS, preferred_element_type=F32))
        ss.append(s * ks[i][0] * MLA_SCALE)
        cps.append(cp)
    s = jnp.concatenate(ss, axis=1)
    m_old = m_s[...]
    m_new = jnp.maximum(m_old, jnp.max(s, axis=-1, keepdims=True))
    alpha = jnp.exp(m_old - m_new)
    p = jnp.exp(s - m_new)
    l_new = alpha * l_s[...] + jnp.sum(p, axis=-1, keepdims=True)
    pv = jnp.dot(p[:, 0:PAGE_SIZE].astype(BF), cps[0], preferred_element_type=F32)
    for i in range(1, pp):
        pv = pv + jnp.dot(p[:, i * PAGE_SIZE:(i + 1) * PAGE_SIZE].astype(BF), cps[i],
                          preferred_element_type=F32)
    acc_new = alpha * acc_s[...] + pv
    m_s[...] = m_new
    l_s[...] = l_new
    acc_s[...] = acc_new

    @pl.when(c == nc - 1)
    def _():
        cn = cn_ref[0]
        krn = krn_ref[0][:, 0:ROPE_DIM]
        s_new = (jnp.sum(ql.astype(F32) * cn, axis=-1, keepdims=True)
                 + jnp.sum(qr32 * krn, axis=-1, keepdims=True)) * ksn_ref[0] * MLA_SCALE
        m2 = jnp.maximum(m_new, s_new)
        a2 = jnp.exp(m_new - m2)
        p2 = jnp.exp(s_new - m2)
        l2 = a2 * l_new + p2
        o_ref[0] = ((a2 * acc_new + p2 * cn) / l2).astype(o_ref.dtype)


def _mla_sample_attention(page_table, qlat3, q3, cache_latent, cache_krope, cache_kscale_t,
                          cn3, krn3, ksn3):
    nb, n_pages = page_table.shape
    pp = PAGES_PER_STEP
    nc = n_pages // pp

    def page_spec(shape, i):
        return pl.BlockSpec(shape, lambda b, c, pt, i=i: (pt[b, c * pp + i], 0, 0))

    in_specs = [
        pl.BlockSpec((1, MLA_HEADS, KV_LORA), lambda b, c, pt: (b, 0, 0)),
        pl.BlockSpec((1, MLA_HEADS, HEAD_PAD), lambda b, c, pt: (b, 0, 0)),
    ]
    in_specs += [page_spec((1, PAGE_SIZE, KV_LORA), i) for i in range(pp)]
    in_specs += [page_spec((1, PAGE_SIZE, ROPE_DIM), i) for i in range(pp)]
    in_specs += [page_spec((1, MLA_HEADS, PAGE_SIZE), i) for i in range(pp)]
    in_specs += [
        pl.BlockSpec((1, 1, KV_LORA), lambda b, c, pt: (b, 0, 0)),
        pl.BlockSpec((1, 1, LANES), lambda b, c, pt: (b, 0, 0)),
        pl.BlockSpec((1, MLA_HEADS, 1), lambda b, c, pt: (b, 0, 0)),
    ]
    grid_spec = pltpu.PrefetchScalarGridSpec(
        num_scalar_prefetch=1,
        grid=(nb, nc),
        in_specs=in_specs,
        out_specs=pl.BlockSpec((1, MLA_HEADS, KV_LORA), lambda b, c, pt: (b, 0, 0)),
        scratch_shapes=[pltpu.VMEM((MLA_HEADS, 1), F32), pltpu.VMEM((MLA_HEADS, 1), F32),
                        pltpu.VMEM((MLA_HEADS, KV_LORA), F32)],
    )
    return pl.pallas_call(
        _paged_kernel,
        grid_spec=grid_spec,
        out_shape=jax.ShapeDtypeStruct((nb, MLA_HEADS, KV_LORA), F32),
        compiler_params=_params(("arbitrary", "arbitrary")),
        name="mla_sample_attention",
    )(page_table, qlat3, q3, *([cache_latent] * pp), *([cache_krope] * pp),
      *([cache_kscale_t] * pp), cn3, krn3, ksn3)


def _router_kernel(x_ref, g_ref, w_ref, h_ref, gate_ref, exp_ref):
    xf = x_ref[...]
    ms = jnp.mean(xf * xf, axis=-1, keepdims=True)
    h = xf * lax.rsqrt(ms + EPS) * g_ref[...]
    h_ref[...] = h.astype(h_ref.dtype)
    logits = jnp.dot(h, w_ref[...], preferred_element_type=F32,
                     precision=lax.Precision.HIGHEST)
    lane = lax.broadcasted_iota(jnp.int32, logits.shape, 1)
    big = jnp.int32(LANES)

    is_g = lane < N_GROUPS
    lg = jnp.where(is_g, logits, NEG_BIG)
    eg = jnp.where(is_g, jnp.exp(lg - jnp.max(lg, axis=-1, keepdims=True)), 0.0)
    pg = eg / jnp.sum(eg, axis=-1, keepdims=True)
    p_grp = jnp.max(pg, axis=-1, keepdims=True)
    grp = jnp.min(jnp.where(is_g & (pg == p_grp), lane, big), axis=-1, keepdims=True)

    e_idx = lane - N_GROUPS
    sel = (e_idx >= 0) & (e_idx < N_EXPERTS) & ((e_idx >> 3) == grp)
    le = jnp.where(sel, logits, NEG_BIG)
    ee = jnp.where(sel, jnp.exp(le - jnp.max(le, axis=-1, keepdims=True)), 0.0)
    pe = ee / jnp.sum(ee, axis=-1, keepdims=True)
    top1 = jnp.max(jnp.where(sel, pe, -1.0), axis=-1, keepdims=True)
    i1 = jnp.min(jnp.where(sel & (pe == top1), lane, big), axis=-1, keepdims=True)
    rest = sel & (lane != i1)
    top2 = jnp.max(jnp.where(rest, pe, -1.0), axis=-1, keepdims=True)
    i2 = jnp.min(jnp.where(rest & (pe == top2), lane, big), axis=-1, keepdims=True)
    denom = top1 + top2
    g1 = p_grp * top1 / denom
    g2 = p_grp * top2 / denom
    gate_ref[...] = jnp.where(lane == 0, g1, jnp.where(lane == 1, g2, 0.0))
    exp_ref[...] = jnp.where(lane == 0, i1 - N_GROUPS, jnp.where(lane == 1, i2 - N_GROUPS, 0))


def _router(x, g, w_router, tm):
    m = x.shape[0]
    return pl.pallas_call(
        _router_kernel,
        grid=(m // tm,),
        in_specs=[
            pl.BlockSpec((tm, D_MODEL), lambda i: (i, 0)),
            pl.BlockSpec((1, D_MODEL), lambda i: (0, 0)),
            pl.BlockSpec((D_MODEL, LANES), lambda i: (0, 0)),
        ],
        out_specs=[
            pl.BlockSpec((tm, D_MODEL), lambda i: (i, 0)),
            pl.BlockSpec((tm, LANES), lambda i: (i, 0)),
            pl.BlockSpec((tm, LANES), lambda i: (i, 0)),
        ],
        out_shape=[jax.ShapeDtypeStruct((m, D_MODEL), BF),
                   jax.ShapeDtypeStruct((m, LANES), F32),
                   jax.ShapeDtypeStruct((m, LANES), jnp.int32)],
        compiler_params=_params(("arbitrary",)),
        name="moe_router",
    )(x, g.reshape(1, D_MODEL), w_router)


def _expert_kernel(be_ref, nu_ref, x_ref, wg_ref, wu_ref, wd_ref, y_ref, wg_s, wu_s, wd_s):
    b = pl.program_id(0)
    prev = be_ref[jnp.maximum(b - 1, 0)]
    fresh = (b == 0) | (be_ref[b] != prev)

    @pl.when(fresh & (b < nu_ref[0]))
    def _():
        wg_s[...] = wg_ref[...].astype(BF)
        wu_s[...] = wu_ref[...].astype(BF)
        wd_s[...] = wd_ref[...].astype(BF)

    @pl.when(b < nu_ref[0])
    def _():
        x = x_ref[...]
        a = jnp.dot(x, wg_s[...], preferred_element_type=F32)
        u = jnp.dot(x, wu_s[...], preferred_element_type=F32)
        hmid = (a * jax.nn.sigmoid(a) * u).astype(BF)
        y_ref[...] = jnp.dot(hmid, wd_s[...], preferred_element_type=F32)


def _experts(blk_e, n_used, xd, w_gate, w_up, w_down, layer):
    rows = xd.shape[0]
    nblk = rows // MOE_BLOCK
    row_map = lambda b, be, nu: (jnp.minimum(b, nu[0] - 1), 0)
    grid_spec = pltpu.PrefetchScalarGridSpec(
        num_scalar_prefetch=2,
        grid=(nblk,),
        in_specs=[
            pl.BlockSpec((MOE_BLOCK, D_MODEL), row_map),
            pl.BlockSpec((None, None, D_MODEL, D_EXPERT), lambda b, be, nu: (layer, be[b], 0, 0)),
            pl.BlockSpec((None, None, D_MODEL, D_EXPERT), lambda b, be, nu: (layer, be[b], 0, 0)),
            pl.BlockSpec((None, None, D_EXPERT, D_MODEL), lambda b, be, nu: (layer, be[b], 0, 0)),
        ],
        out_specs=pl.BlockSpec((MOE_BLOCK, D_MODEL), row_map),
        scratch_shapes=[pltpu.VMEM((D_MODEL, D_EXPERT), BF), pltpu.VMEM((D_MODEL, D_EXPERT), BF),
                        pltpu.VMEM((D_EXPERT, D_MODEL), BF)],
    )
    return pl.pallas_call(
        _expert_kernel,
        grid_spec=grid_spec,
        out_shape=jax.ShapeDtypeStruct((rows, D_MODEL), F32),
        compiler_params=_params(("arbitrary",)),
        name="moe_experts",
    )(blk_e, n_used, xd, w_gate, w_up, w_down)


def _moe(xp, xs, g_ffn_l, w_router, w_gate, w_up, w_down, layer):
    mp, ms_ = xp.shape[0], xs.shape[0]
    hp, gp, ep = _router(xp, g_ffn_l, w_router, 512)
    hs, gs, es = _router(xs, g_ffn_l, w_router, ms_)
    h = jnp.concatenate([hp, hs], axis=0)
    gate = jnp.concatenate([gp[:, :2], gs[:, :2]], axis=0)
    expert = jnp.concatenate([ep[:, :2], es[:, :2]], axis=0)
    mt = mp + ms_
    n_slots = 2 * mt
    nblk = (n_slots + N_EXPERTS * (MOE_BLOCK - 1) + MOE_BLOCK - 1) // MOE_BLOCK

    e_flat = expert.reshape(-1)
    onehot = (e_flat[:, None] == jnp.arange(N_EXPERTS, dtype=jnp.int32)[None, :]).astype(jnp.int32)
    csum = jnp.cumsum(onehot, axis=0)
    rank = jnp.take_along_axis(csum, e_flat[:, None], axis=1)[:, 0] - 1
    counts = csum[-1]
    padded = (counts + MOE_BLOCK - 1) // MOE_BLOCK * MOE_BLOCK
    pad_ends = jnp.cumsum(padded)
    pad_starts = pad_ends - padded
    dest = pad_starts[e_flat] + rank
    n_used = (pad_ends[-1] // MOE_BLOCK).astype(jnp.int32)
    blk_ids = jnp.arange(nblk, dtype=jnp.int32)
    blk_e = jnp.minimum(jnp.searchsorted(pad_ends, blk_ids * MOE_BLOCK, side='right'),
                        N_EXPERTS - 1).astype(jnp.int32)
    blk_e = jnp.where(blk_ids < n_used, blk_e, blk_e[jnp.maximum(n_used - 1, 0)])
    row_tok = jnp.zeros((nblk * MOE_BLOCK,), jnp.int32).at[dest].set(
        jnp.arange(n_slots, dtype=jnp.int32) // 2)

    xd = h[row_tok]
    yd = _experts(blk_e, n_used.reshape(1), xd, w_gate, w_up, w_down, layer)
    d2 = dest.reshape(mt, 2)
    y = yd[d2[:, 0]] * gate[:, 0:1] + yd[d2[:, 1]] * gate[:, 1:2]
    return xp + y[:mp], xs + y[mp:]


def _rope_tables(pos):
    half = ROPE_DIM // 2
    inv = ROPE_THETA ** (-jnp.arange(half, dtype=F32) / half)
    ang = pos.astype(F32)[:, None] * inv[None, :]
    cos, sin = jnp.cos(ang), jnp.sin(ang)
    z = jnp.zeros_like(cos)
    a = jnp.concatenate([cos, cos, z, z], axis=1)
    b = jnp.concatenate([z, sin, z, z], axis=1)
    c = jnp.concatenate([-sin, z, z, z], axis=1)
    return a, b, c


def _out_spec(tm, tn):
    return pl.BlockSpec((tm, tn), lambda i, j: (i, j))


def _table_specs(tabs, tm, nper):
    return [(t, pl.BlockSpec((tm, LANES), lambda i, j, nper=nper: (i % nper, 0))) for t in tabs]


def _vec_extra(v):
    n = v.shape[-1]
    return (v.reshape(1, n), pl.BlockSpec((1, n), lambda i, j: (0, 0)))


def kernel(x_prompt, x_sample, cache_latent, cache_krope, cache_kscale, cache_mem_k, cache_mem_v, state_conv, page_table, mem_prompt, g_mix, g_ffn, g_mem, w_mem_kv, g_mq, g_mk, w_in_a, conv_w, conv_b, ln_g, ln_b, w_out_a, g_kv_in, w_kv_down, g_kv_lat, w_uk, w_uv, g_k, w_in_b, g_qlat, w_q_up, g_q, w_out_b, w_rg, w_re, w_gate, w_up, w_down):
    batch, seq, _ = x_prompt.shape
    nb = x_sample.shape[0]
    depth = g_mix.shape[0]
    n_a = w_in_a.shape[0]
    mp = batch * seq
    tmp = 1024
    xp = x_prompt.reshape(mp, D_MODEL)
    xs = x_sample.reshape(nb, D_MODEL)

    tabs_p = _rope_tables(jnp.arange(seq))
    tabs_s = _rope_tables(jnp.full((nb,), PAST_LEN))
    w_kvd = jnp.pad(w_kv_down, ((0, 0), (0, LANES - ROPE_DIM)))
    gk_n = g_k[:NOPE_DIM]
    gk_r = jnp.pad(g_k[NOPE_DIM:], (0, LANES - ROPE_DIM))
    w_uk2 = w_uk.reshape(KV_LORA, MLA_HEADS * NOPE_DIM)
    w_uk_t = jnp.transpose(w_uk, (1, 2, 0))
    w_uv2 = w_uv.reshape(KV_LORA, MLA_HEADS * V_DIM)
    w_qu = jnp.pad(w_q_up, ((0, 0), (0, 0), (0, 0), (0, HEAD_PAD - QK_DIM)))
    w_qu = w_qu.reshape(w_q_up.shape[0], Q_LORA, MLA_HEADS * HEAD_PAD)
    gq_pad = jnp.pad(g_q, ((0, 0), (0, HEAD_PAD - QK_DIM)))
    w_router = jnp.pad(jnp.concatenate([w_rg, w_re], axis=-1),
                       ((0, 0), (0, 0), (0, LANES - N_GROUPS - N_EXPERTS)))
    conv_w32 = jnp.pad(conv_w, ((0, 0), (0, 32 - CONV_W), (0, 0)))
    cache_kscale_t = jnp.transpose(cache_kscale, (0, 2, 1))
    cmk = cache_mem_k.reshape(depth, nb, N_MEM, MEM_W)
    cmv = cache_mem_v.reshape(depth, nb, N_MEM, MEM_W)

    memk, memv = _memory_kv_all(mem_prompt.reshape(batch * N_MEM, D_MODEL), g_mem, w_mem_kv, g_mk)

    conv_p_list, conv_s_list = [], []
    shared = None
    for l in range(depth):
        if l < n_a:
            n_glu = D_MODEL
            outs_p, outs_s = [], []
            for x, m, tm, dst in ((xp, mp, tmp, outs_p), (xs, nb, nb, outs_s)):
                tn = 256
                a = _fused_mm(
                    "in_a_glu", [(x, D_MODEL, 0)],
                    [(0, w_in_a, l, D_MODEL, tn, 0, 0), (0, w_in_a, l, D_MODEL, tn, 0, n_glu // tn)],
                    m=m, tm=tm, nj=n_glu // tn, gain=g_mix[l],
                    outs=[(jax.ShapeDtypeStruct((m, n_glu), F32), _out_spec(tm, tn))],
                    epi=_epi_glu)[0]
                qm = _fused_mm(
                    "in_a_memq", [(x, D_MODEL, 0)],
                    [(0, w_in_a, l, D_MODEL, tn, 0, 2 * n_glu // tn)],
                    m=m, tm=tm, nj=MEM_W // tn, gain=g_mix[l],
                    outs=[(jax.ShapeDtypeStruct((m, MEM_W), F32), _out_spec(tm, tn))],
                    epi=_epi_plain)[0]
                dst.extend([a, qm])
            a_p, qm_p = outs_p
            a_s, qm_s = outs_s
            yc_p = _conv_prompt(a_p, conv_w32[l], conv_b[l], ln_g[l], ln_b[l], batch)
            conv_p_list.append(a_p.reshape(batch, seq, D_MODEL)[:, seq - (CONV_W - 1):])
            yc_s, ns = _conv_sample(state_conv[l], a_s, conv_w32[l], conv_b[l], ln_g[l], ln_b[l])
            conv_s_list.append(ns)
            ym_p = _memattn_prompt(qm_p, 0, memk, memv, l, g_mq[l], batch)
            ym_s = _memattn_sample(qm_s.reshape(nb, 1, MEM_W), 0, cmk, cmv, l, g_mq[l])
            new = []
            for x, m, tm, y1, y2 in ((xp, mp, tmp, yc_p, ym_p), (xs, nb, nb, yc_s, ym_s)):
                tn = 512
                new.append(_fused_mm(
                    "out_a", [(y1, D_MODEL, 0), (y2, MEM_W, 0)],
                    [(0, w_out_a, l, D_MODEL, tn, 0, 0), (1, w_out_a, l, MEM_W, tn, D_MODEL // MEM_W, 0)],
                    m=m, tm=tm, nj=D_MODEL // tn,
                    extras=[(x, _out_spec(tm, tn))],
                    outs=[(jax.ShapeDtypeStruct((m, D_MODEL), F32), _out_spec(tm, tn))],
                    epi=_epi_residual)[0])
            xp, xs = new
        else:
            jb = l - n_a
            kfull, vfull, c_s, kr_s, ks_s = shared
            att = []
            for x, m, tm, tabs, nper, qdt in ((xp, mp, tmp, tabs_p, seq // tmp, BF),
                                              (xs, nb, nb, tabs_s, 1, F32)):
                tn = Q_LORA
                n_in = Q_LORA + MEM_W
                u, qm = _fused_mm(
                    "in_b", [(x, D_MODEL, 0)], [(0, w_in_b, jb, D_MODEL, tn, 0, 0)],
                    m=m, tm=tm, nj=n_in // tn, gain=g_mix[l],
                    outs=[(jax.ShapeDtypeStruct((m, Q_LORA), F32),
                           pl.BlockSpec((tm, tn), lambda i, j: (i, 0))),
                          (jax.ShapeDtypeStruct((m, MEM_W), F32),
                           pl.BlockSpec((tm, tn), lambda i, j: (i, jnp.maximum(j - 1, 0))))],
                    epi=_epi_split)
                qh = _fused_mm(
                    "q_up", [(u, Q_LORA, 0)], [(0, w_qu, jb, Q_LORA, HEAD_PAD, 0, 0)],
                    m=m, tm=tm, nj=MLA_HEADS, gain=g_qlat[jb],
                    extras=[_vec_extra(gq_pad[jb])] + _table_specs(tabs, tm, nper),
                    outs=[(jax.ShapeDtypeStruct((m, MLA_HEADS * HEAD_PAD), qdt), _out_spec(tm, HEAD_PAD))],
                    epi=_epi_qhead)[0]
                att.append((qm, qh))
            (qm_p, qh_p), (qm_s, qh_s) = att
            att_p = _mla_prompt_attention(qh_p, kfull, vfull, batch)
            ym_p = _memattn_prompt(qm_p, 0, memk, memv, l, g_mq[l], batch)
            ym_s = _memattn_sample(qm_s.reshape(nb, 1, MEM_W), 0, cmk, cmv, l, g_mq[l])

            qlat = _headwise_mm(
                "q_latent", qh_s, pl.BlockSpec((nb, NOPE_DIM), lambda h: (0, 2 * h)),
                w_uk_t, pl.BlockSpec((None, NOPE_DIM, KV_LORA), lambda h: (h, 0, 0)),
                gk_n.reshape(1, NOPE_DIM),
                jax.ShapeDtypeStruct((nb, MLA_HEADS * KV_LORA), BF),
                pl.BlockSpec((nb, KV_LORA), lambda h: (0, h)))
            o_lat = _mla_sample_attention(
                page_table, qlat.reshape(nb, MLA_HEADS, KV_LORA),
                qh_s.reshape(nb, MLA_HEADS, HEAD_PAD), cache_latent, cache_krope, cache_kscale_t,
                c_s.reshape(nb, 1, KV_LORA), kr_s.reshape(nb, 1, LANES),
                ks_s[:, :MLA_HEADS].reshape(nb, MLA_HEADS, 1))
            att_s = _headwise_mm(
                "v_expand", o_lat.reshape(nb, MLA_HEADS * KV_LORA),
                pl.BlockSpec((nb, KV_LORA), lambda h: (0, h)),
                w_uv2, pl.BlockSpec((KV_LORA, V_DIM), lambda h: (0, h)),
                jnp.ones((1, KV_LORA), F32),
                jax.ShapeDtypeStruct((nb, MLA_HEADS * V_DIM), BF),
                pl.BlockSpec((nb, V_DIM), lambda h: (0, h)))
            new = []
            n_att = MLA_HEADS * V_DIM
            for x, m, tm, y1, y2 in ((xp, mp, tmp, att_p, ym_p), (xs, nb, nb, att_s, ym_s)):
                tn = 512
                new.append(_fused_mm(
                    "out_b", [(y1, n_att, 0), (y2, MEM_W, 0)],
                    [(0, w_out_b, jb, n_att, tn, 0, 0), (1, w_out_b, jb, MEM_W, tn, n_att // MEM_W, 0)],
                    m=m, tm=tm, nj=D_MODEL // tn,
                    extras=[(x, _out_spec(tm, tn))],
                    outs=[(jax.ShapeDtypeStruct((m, D_MODEL), F32), _out_spec(tm, tn))],
                    epi=_epi_residual)[0])
            xp, xs = new

        xp, xs = _moe(xp, xs, g_ffn[l], w_router[l], w_gate, w_up, w_down, l)

        if l == n_a - 1:
            lat = []
            for x, m, tm, tabs, nper in ((xp, mp, tmp, tabs_p, seq // tmp), (xs, nb, nb, tabs_s, 1)):
                nck = KV_LORA + LANES
                c, kr128, ssq128 = _fused_mm(
                    "kv_latent", [(x, D_MODEL, 0)], [(0, w_kvd, None, D_MODEL, nck, 0, 0)],
                    m=m, tm=tm, nj=1, gain=g_kv_in,
                    extras=[_vec_extra(g_kv_lat), _vec_extra(gk_r)] + _table_specs(tabs, tm, nper),
                    outs=[(jax.ShapeDtypeStruct((m, KV_LORA), F32), _out_spec(tm, KV_LORA)),
                          (jax.ShapeDtypeStruct((m, LANES), F32), pl.BlockSpec((tm, LANES), lambda i, j: (i, 0))),
                          (jax.ShapeDtypeStruct((m, LANES), F32), pl.BlockSpec((tm, LANES), lambda i, j: (i, 0)))],
                    epi=_epi_latent)
                kf, ks128 = _fused_mm(
                    "k_nope", [(c, KV_LORA, 0)], [(0, w_uk2, None, KV_LORA, 2 * NOPE_DIM, 0, 0)],
                    m=m, tm=tm, nj=MLA_HEADS // 2,
                    extras=[(kr128, pl.BlockSpec((tm, LANES), lambda i, j: (i, 0))),
                            (ssq128, pl.BlockSpec((tm, LANES), lambda i, j: (i, 0))),
                            _vec_extra(gk_n)],
                    outs=[(jax.ShapeDtypeStruct((m, MLA_HEADS * HEAD_PAD), BF), _out_spec(tm, 2 * HEAD_PAD)),
                          (jax.ShapeDtypeStruct((m, LANES), F32), pl.BlockSpec((tm, LANES), lambda i, j: (i, 0)))],
                    epi=_epi_knope)
                lat.append((c, kr128, ks128, kf))
            (c_p, kr_p, ks_p, kfull), (c_s, kr_s, ks_s, _) = lat
            vfull = _fused_mm(
                "v_full", [(c_p, KV_LORA, 0)], [(0, w_uv2, None, KV_LORA, 512, 0, 0)],
                m=mp, tm=tmp, nj=MLA_HEADS * V_DIM // 512,
                outs=[(jax.ShapeDtypeStruct((mp, MLA_HEADS * V_DIM), BF), _out_spec(tmp, 512))],
                epi=_epi_plain)[0]
            shared = (kfull, vfull, c_s, kr_s, ks_s)

    shape5 = (depth, batch, N_MEM, MEM_HEADS, MEM_HEAD_DIM)
    return (xp.reshape(batch, seq, D_MODEL), xs.reshape(nb, 1, D_MODEL),
            c_p.reshape(batch, seq, KV_LORA), kr_p[:, :ROPE_DIM].reshape(batch, seq, ROPE_DIM),
            ks_p[:, :MLA_HEADS].reshape(batch, seq, MLA_HEADS),
            memk.reshape(shape5), memv.reshape(shape5),
            jnp.stack(conv_p_list, axis=0),
            c_s.reshape(nb, 1, KV_LORA), kr_s[:, :ROPE_DIM].reshape(nb, 1, ROPE_DIM),
            ks_s[:, :MLA_HEADS].reshape(nb, 1, MLA_HEADS),
            jnp.stack(conv_s_list, axis=0))
```

```python
import functools
import math

import jax
import jax.numpy as jnp
from jax import lax
from jax.experimental import pallas as pl
from jax.experimental.pallas import tpu as pltpu

F32 = jnp.float32
BF = jnp.bfloat16

D_MODEL = 2048
SEQ = 2048
PAST_LEN = 8192
PAGE_SIZE = 128
CONV_W = 31
N_MEM = 256
MEM_HEADS = 4
MEM_HEAD_DIM = 256
MEM_W = MEM_HEADS * MEM_HEAD_DIM
MLA_HEADS = 16
Q_LORA = 512
KV_LORA = 512
NOPE_DIM = 128
ROPE_DIM = 64
QK_DIM = NOPE_DIM + ROPE_DIM
V_DIM = 128
ROPE_THETA = 10000.0
N_GROUPS = 8
EXPERTS_PER_GROUP = 8
N_EXPERTS = 64
D_EXPERT = 512
MOE_BLOCK = 128
EPS = 1e-6

HEAD_PAD = 256
LANES = 128
VMEM_LIMIT_BYTES = 56 * 1024 * 1024
NEG_BIG = -1e30
NT_DIMS = (((1,), (1,)), ((), ()))


def _params(sem):
    return pltpu.CompilerParams(dimension_semantics=sem, vmem_limit_bytes=VMEM_LIMIT_BYTES)


def _mm_kernel(*refs, nx, w_x, has_gain, n_extra, n_out, epi, tm):
    x_refs = refs[:nx]
    w_refs = refs[nx:nx + len(w_x)]
    pos = nx + len(w_x)
    g_ref = refs[pos] if has_gain else None
    pos += int(has_gain)
    extra_refs = refs[pos:pos + n_extra]
    pos += n_extra
    out_refs = refs[pos:pos + n_out]
    pos += n_out
    h_s = refs[pos] if has_gain else None
    j = pl.program_id(1)

    if has_gain:
        chunk = min(tm, 128)

        @pl.when(j == 0)
        def _():
            def body(c, carry):
                r = pl.multiple_of(c * chunk, chunk)
                xf = x_refs[0][pl.ds(r, chunk), :].astype(F32)
                ms = jnp.mean(xf * xf, axis=-1, keepdims=True)
                h_s[pl.ds(r, chunk), :] = (xf * lax.rsqrt(ms + EPS) * g_ref[...]).astype(BF)
                return carry
            lax.fori_loop(0, tm // chunk, body, 0)

    ds = []
    for wi, xi in enumerate(w_x):
        if has_gain and xi == 0:
            lhs = h_s[...]
        else:
            lhs = x_refs[xi][...].astype(BF)
        ds.append(jnp.dot(lhs, w_refs[wi][...].astype(BF), preferred_element_type=F32))
    epi(ds, extra_refs, out_refs, j)


def _fused_mm(name, xs, ws, *, m, tm, nj, gain=None, extras=(), outs, epi):
    in_arrays, in_specs = [], []
    for arr, k, cb in xs:
        in_arrays.append(arr)
        in_specs.append(pl.BlockSpec((tm, k), lambda i, j, cb=cb: (i, cb)))
    for xi, arr, layer, k, tn, rb, cb0 in ws:
        in_arrays.append(arr)
        if layer is None:
            in_specs.append(pl.BlockSpec((k, tn), lambda i, j, rb=rb, cb0=cb0: (rb, cb0 + j)))
        else:
            in_specs.append(pl.BlockSpec((None, k, tn),
                                         lambda i, j, l=layer, rb=rb, cb0=cb0: (l, rb, cb0 + j)))
    scratch = []
    if gain is not None:
        k0 = xs[0][1]
        in_arrays.append(gain.reshape(1, k0).astype(F32))
        in_specs.append(pl.BlockSpec((1, k0), lambda i, j: (0, 0)))
        scratch.append(pltpu.VMEM((tm, k0), BF))
    for arr, spec in extras:
        in_arrays.append(arr)
        in_specs.append(spec)
    kern = functools.partial(_mm_kernel, nx=len(xs), w_x=tuple(w[0] for w in ws),
                             has_gain=gain is not None, n_extra=len(extras),
                             n_out=len(outs), epi=epi, tm=tm)
    res = pl.pallas_call(
        kern,
        grid=(m // tm, nj),
        in_specs=in_specs,
        out_specs=[o[1] for o in outs],
        out_shape=[o[0] for o in outs],
        scratch_shapes=scratch,
        compiler_params=_params(("arbitrary", "arbitrary")),
        name=name,
    )(*in_arrays)
    return res


def _epi_plain(ds, ex, outs, j):
    acc = ds[0]
    for d in ds[1:]:
        acc = acc + d
    outs[0][...] = acc.astype(outs[0].dtype)


def _epi_residual(ds, ex, outs, j):
    acc = ds[0]
    for d in ds[1:]:
        acc = acc + d
    outs[0][...] = (ex[0][...] + acc).astype(outs[0].dtype)


def _epi_split(ds, ex, outs, j):
    @pl.when(j == 0)
    def _():
        outs[0][...] = ds[0]

    @pl.when(j > 0)
    def _():
        outs[1][...] = ds[0]


def _epi_glu(ds, ex, outs, j):
    outs[0][...] = (ds[0] * jax.nn.sigmoid(ds[1])).astype(outs[0].dtype)


def _rope_tile(t, a_ref, b_ref, c_ref):
    return (t * a_ref[...] + pltpu.roll(t, 32, 1) * b_ref[...]
            + pltpu.roll(t, 96, 1) * c_ref[...])


def _epi_qhead(ds, ex, outs, j):
    acc = ds[0]
    gq_ref, a_ref, b_ref, c_ref = ex
    ssq = jnp.sum(acc * acc, axis=-1, keepdims=True)
    qn = acc * lax.rsqrt(ssq * (1.0 / QK_DIM) + EPS) * gq_ref[...]
    outs[0][:, :LANES] = qn[:, :LANES].astype(outs[0].dtype)
    outs[0][:, LANES:] = _rope_tile(qn[:, LANES:], a_ref, b_ref, c_ref).astype(outs[0].dtype)


def _epi_latent(ds, ex, outs, j):
    ck = ds[0]
    glat_ref, gkr_ref, a_ref, b_ref, c_ref = ex
    c_raw = ck[:, :KV_LORA]
    ms = jnp.mean(c_raw * c_raw, axis=-1, keepdims=True)
    outs[0][...] = c_raw * lax.rsqrt(ms + EPS) * glat_ref[...]
    krt = ck[:, KV_LORA:]
    ssq = jnp.sum(krt * krt, axis=-1, keepdims=True)
    outs[1][...] = _rope_tile(krt * gkr_ref[...], a_ref, b_ref, c_ref)
    outs[2][...] = jnp.broadcast_to(ssq, outs[2].shape)


def _epi_knope(ds, ex, outs, j):
    acc = ds[0]
    kr_ref, ssq_ref, gkn_ref = ex
    ssq_r = ssq_ref[:, 0:1]
    kr = kr_ref[...]
    lane = lax.broadcasted_iota(jnp.int32, outs[1].shape, 1)

    @pl.when(j == 0)
    def _():
        outs[1][...] = jnp.zeros(outs[1].shape, F32)

    ks_all = outs[1][...]
    for t in range(2):
        kn = acc[:, t * NOPE_DIM:(t + 1) * NOPE_DIM]
        ssq = jnp.sum(kn * kn, axis=-1, keepdims=True) + ssq_r
        ks = lax.rsqrt(ssq * (1.0 / QK_DIM) + EPS)
        outs[0][:, t * HEAD_PAD:t * HEAD_PAD + LANES] = (kn * gkn_ref[...] * ks).astype(outs[0].dtype)
        outs[0][:, t * HEAD_PAD + LANES:(t + 1) * HEAD_PAD] = (kr * ks).astype(outs[0].dtype)
        ks_all = jnp.where(lane == 2 * j + t, ks, ks_all)
    outs[1][...] = ks_all


def _memkv_kernel(x_ref, g_ref, wk_ref, wv_ref, gk_ref, k_out, v_out, h_s):
    j = pl.program_id(1)
    rows = x_ref.shape[0]
    chunk = 128

    @pl.when(j == 0)
    def _():
        def body(c, carry):
            r = pl.multiple_of(c * chunk, chunk)
            xf = x_ref[pl.ds(r, chunk), :]
            ms = jnp.mean(xf * xf, axis=-1, keepdims=True)
            h_s[pl.ds(r, chunk), :] = (xf * lax.rsqrt(ms + EPS) * g_ref[...]).astype(BF)
            return carry
        lax.fori_loop(0, rows // chunk, body, 0)

    h = h_s[...]
    k = jnp.dot(h, wk_ref[...].astype(BF), preferred_element_type=F32)
    ms = jnp.mean(k * k, axis=-1, keepdims=True)
    k_out[...] = k * lax.rsqrt(ms + EPS) * gk_ref[...]
    v_out[...] = jnp.dot(h, wv_ref[...].astype(BF), preferred_element_type=F32)


def _memory_kv_all(mem2d, g_mem, w_mem_kv, g_mk):
    depth = w_mem_kv.shape[0]
    rows = mem2d.shape[0]
    hd = MEM_HEAD_DIM
    out = pl.pallas_call(
        _memkv_kernel,
        grid=(depth, MEM_HEADS),
        in_specs=[
            pl.BlockSpec((rows, D_MODEL), lambda l, j: (0, 0)),
            pl.BlockSpec((None, 1, D_MODEL), lambda l, j: (l, 0, 0)),
            pl.BlockSpec((None, D_MODEL, hd), lambda l, j: (l, 0, j)),
            pl.BlockSpec((None, D_MODEL, hd), lambda l, j: (l, 0, MEM_HEADS + j)),
            pl.BlockSpec((None, 1, hd), lambda l, j: (l, 0, 0)),
        ],
        out_specs=[
            pl.BlockSpec((None, rows, hd), lambda l, j: (l, 0, j)),
            pl.BlockSpec((None, rows, hd), lambda l, j: (l, 0, j)),
        ],
        out_shape=[jax.ShapeDtypeStruct((depth, rows, MEM_W), F32)] * 2,
        scratch_shapes=[pltpu.VMEM((rows, D_MODEL), BF)],
        compiler_params=_params(("arbitrary", "arbitrary")),
        name="memory_kv",
    )(mem2d, g_mem.reshape(depth, 1, D_MODEL), w_mem_kv, w_mem_kv, g_mk.reshape(depth, 1, hd))
    return out


CONV_TT = 256
CONV_RC = 16
CONV_HALO = 32
SUBLANES = 8


def _conv_prompt_kernel(a_ref, w_ref, cb_ref, lg_ref, lb_ref, o_ref, sh):
    t = pl.program_id(1)
    tt = CONV_TT
    span = tt + CONV_HALO

    @pl.when(t == 0)
    def _():
        sh[0, 0:CONV_HALO, :] = jnp.zeros((CONV_HALO, D_MODEL), F32)
        sh[0, span:span + SUBLANES, :] = jnp.zeros((SUBLANES, D_MODEL), F32)

    sh[0, CONV_HALO:span, :] = a_ref[...]
    for p in range(1, SUBLANES):
        sh[p, 0:span, :] = sh[0, p:p + span, :]

    def body(c, carry):
        r = c * CONV_RC
        acc = jnp.zeros((CONV_RC, D_MODEL), F32)
        for k in range(CONV_W):
            off = 2 + k
            p = off % SUBLANES
            start = pl.multiple_of(r + (off - p), SUBLANES)
            acc = acc + sh[p, pl.ds(start, CONV_RC), :] * w_ref[k:k + 1, :]
        y = acc + cb_ref[...]
        mu = jnp.mean(y, axis=-1, keepdims=True)
        yc = y - mu
        var = jnp.mean(yc * yc, axis=-1, keepdims=True)
        z = yc * lax.rsqrt(var + EPS) * lg_ref[...] + lb_ref[...]
        o_ref[pl.ds(pl.multiple_of(r, CONV_RC), CONV_RC), :] = (z * jax.nn.sigmoid(z)).astype(o_ref.dtype)
        return carry

    lax.fori_loop(0, tt // CONV_RC, body, 0)
    sh[0, 0:CONV_HALO, :] = sh[0, tt:span, :]


def _conv_prompt(a, w32, cb, lg, lb, batch):
    m = a.shape[0]
    nt = SEQ // CONV_TT
    vec = lambda v: v.reshape(1, D_MODEL)
    return pl.pallas_call(
        _conv_prompt_kernel,
        grid=(batch, nt),
        in_specs=[
            pl.BlockSpec((CONV_TT, D_MODEL), lambda b, t: (b * nt + t, 0)),
            pl.BlockSpec((32, D_MODEL), lambda b, t: (0, 0)),
            pl.BlockSpec((1, D_MODEL), lambda b, t: (0, 0)),
            pl.BlockSpec((1, D_MODEL), lambda b, t: (0, 0)),
            pl.BlockSpec((1, D_MODEL), lambda b, t: (0, 0)),
        ],
        out_specs=pl.BlockSpec((CONV_TT, D_MODEL), lambda b, t: (b * nt + t, 0)),
        out_shape=jax.ShapeDtypeStruct((m, D_MODEL), BF),
        scratch_shapes=[pltpu.VMEM((SUBLANES, CONV_TT + CONV_HALO + SUBLANES, D_MODEL), F32)],
        compiler_params=_params(("arbitrary", "arbitrary")),
        name="conv_prompt",
    )(a, w32, vec(cb), vec(lg), vec(lb))


CONV_SB = 8


def _conv_sample_kernel(st_ref, a_ref, w_ref, cb_ref, lg_ref, lb_ref, y_ref, ns_ref):
    nstate = CONV_W - 1
    for s in range(CONV_SB):
        st = st_ref[s]
        a = a_ref[s]
        y = (jnp.sum(st * w_ref[0:nstate, :], axis=0, keepdims=True)
             + a * w_ref[nstate:nstate + 1, :] + cb_ref[...])
        mu = jnp.mean(y, axis=-1, keepdims=True)
        yc = y - mu
        var = jnp.mean(yc * yc, axis=-1, keepdims=True)
        z = yc * lax.rsqrt(var + EPS) * lg_ref[...] + lb_ref[...]
        y_ref[s] = (z * jax.nn.sigmoid(z)).astype(y_ref.dtype)
        ns_ref[s, 0:nstate - 1, :] = st_ref[s, 1:nstate, :]
        ns_ref[s, nstate - 1:nstate, :] = a


def _conv_sample(state, a, w32, cb, lg, lb):
    nb = state.shape[0]
    nstate = CONV_W - 1
    vec = lambda v: v.reshape(1, D_MODEL)
    y, ns = pl.pallas_call(
        _conv_sample_kernel,
        grid=(nb // CONV_SB,),
        in_specs=[
            pl.BlockSpec((CONV_SB, nstate, D_MODEL), lambda i: (i, 0, 0)),
            pl.BlockSpec((CONV_SB, 1, D_MODEL), lambda i: (i, 0, 0)),
            pl.BlockSpec((32, D_MODEL), lambda i: (0, 0)),
            pl.BlockSpec((1, D_MODEL), lambda i: (0, 0)),
            pl.BlockSpec((1, D_MODEL), lambda i: (0, 0)),
            pl.BlockSpec((1, D_MODEL), lambda i: (0, 0)),
        ],
        out_specs=[
            pl.BlockSpec((CONV_SB, 1, D_MODEL), lambda i: (i, 0, 0)),
            pl.BlockSpec((CONV_SB, nstate, D_MODEL), lambda i: (i, 0, 0)),
        ],
        out_shape=[jax.ShapeDtypeStruct((nb, 1, D_MODEL), BF),
                   jax.ShapeDtypeStruct((nb, nstate, D_MODEL), F32)],
        compiler_params=_params(("arbitrary",)),
        name="conv_sample",
    )(state, a.reshape(nb, 1, D_MODEL), w32, vec(cb), vec(lg), vec(lb))
    return y.reshape(nb, D_MODEL), ns


MEM_TQ = 512
MEM_SCALE = MEM_HEAD_DIM ** -0.5


def _memattn_prompt_kernel(q_ref, k_ref, v_ref, g_ref, o_ref):
    for h in range(MEM_HEADS):
        sl = slice(h * MEM_HEAD_DIM, (h + 1) * MEM_HEAD_DIM)
        qh = q_ref[:, sl]
        ms = jnp.mean(qh * qh, axis=-1, keepdims=True)
        qn = (qh * lax.rsqrt(ms + EPS) * g_ref[...]).astype(BF)
        s = lax.dot_general(qn, k_ref[:, sl].astype(BF), NT_DIMS,
                            preferred_element_type=F32) * MEM_SCALE
        mx = jnp.max(s, axis=-1, keepdims=True)
        p = jnp.exp(s - mx)
        l = jnp.sum(p, axis=-1, keepdims=True)
        o = jnp.dot(p.astype(BF), v_ref[:, sl].astype(BF), preferred_element_type=F32)
        o_ref[:, sl] = (o / l).astype(o_ref.dtype)


def _memattn_prompt(q, q_colblk, memk, memv, layer, g_mq_l, batch):
    m = q.shape[0]
    nq = SEQ // MEM_TQ
    return pl.pallas_call(
        _memattn_prompt_kernel,
        grid=(batch, nq),
        in_specs=[
            pl.BlockSpec((MEM_TQ, MEM_W), lambda b, i: (b * nq + i, q_colblk)),
            pl.BlockSpec((None, N_MEM, MEM_W), lambda b, i: (layer, b, 0)),
            pl.BlockSpec((None, N_MEM, MEM_W), lambda b, i: (layer, b, 0)),
            pl.BlockSpec((1, MEM_HEAD_DIM), lambda b, i: (0, 0)),
        ],
        out_specs=pl.BlockSpec((MEM_TQ, MEM_W), lambda b, i: (b * nq + i, 0)),
        out_shape=jax.ShapeDtypeStruct((m, MEM_W), BF),
        compiler_params=_params(("arbitrary", "arbitrary")),
        name="memattn_prompt",
    )(q, memk, memv, g_mq_l.reshape(1, MEM_HEAD_DIM))


MEM_SB = 4


def _memattn_sample_kernel(q_ref, k_ref, v_ref, g_ref, o_ref):
    for s in range(MEM_SB):
        q = q_ref[s]
        ms = jnp.mean(q * q, axis=-1, keepdims=True)
        qn = q * lax.rsqrt(ms + EPS) * g_ref[...]
        sc = jnp.sum(k_ref[s] * qn[None], axis=-1, keepdims=True) * MEM_SCALE
        mx = jnp.max(sc, axis=0, keepdims=True)
        p = jnp.exp(sc - mx)
        l = jnp.sum(p, axis=0)
        o = jnp.sum(p * v_ref[s], axis=0) / l
        o_ref[s] = o.astype(o_ref.dtype)


def _memattn_sample(q3, cmk, cmv, layer, g_mq_l):
    nb = q3.shape[0]
    kv_spec = pl.BlockSpec((None, MEM_SB, N_MEM, MEM_HEADS, MEM_HEAD_DIM), lambda i: (layer, i, 0, 0, 0))
    out = pl.pallas_call(
        _memattn_sample_kernel,
        grid=(nb // MEM_SB,),
        in_specs=[
            pl.BlockSpec((MEM_SB, MEM_HEADS, MEM_HEAD_DIM), lambda i: (i, 0, 0)),
            kv_spec, kv_spec,
            pl.BlockSpec((1, MEM_HEAD_DIM), lambda i: (0, 0)),
        ],
        out_specs=pl.BlockSpec((MEM_SB, MEM_HEADS, MEM_HEAD_DIM), lambda i: (i, 0, 0)),
        out_shape=jax.ShapeDtypeStruct((nb, MEM_HEADS, MEM_HEAD_DIM), BF),
        compiler_params=_params(("arbitrary",)),
        name="memattn_sample",
    )(q3, cmk, cmv, g_mq_l.reshape(1, MEM_HEAD_DIM))
    return out.reshape(nb, MEM_W)


FLASH_T = 512
FLASH_HP = 2
MLA_SCALE = QK_DIM ** -0.5


def _flash_kernel(q_ref, k_ref, v_ref, o_ref):
    qi = pl.program_id(2)
    t = FLASH_T
    row = lax.broadcasted_iota(jnp.int32, (t, t), 0)
    col = lax.broadcasted_iota(jnp.int32, (t, t), 1)

    for hh in range(FLASH_HP):
        q = q_ref[:, hh * HEAD_PAD:(hh + 1) * HEAD_PAD]

        def tile(ki, carry, diagonal, hh=hh, q=q):
            m, l, acc = carry
            r = pl.multiple_of(ki * t, t)
            k = k_ref[pl.ds(r, t), hh * HEAD_PAD:(hh + 1) * HEAD_PAD]
            s = lax.dot_general(q, k, NT_DIMS, preferred_element_type=F32) * MLA_SCALE
            if diagonal:
                s = jnp.where(col <= row, s, NEG_BIG)
            m_new = jnp.maximum(m, jnp.max(s, axis=-1, keepdims=True))
            alpha = jnp.exp(m - m_new)
            p = jnp.exp(s - m_new)
            l = alpha * l + jnp.sum(p, axis=-1, keepdims=True)
            v = v_ref[pl.ds(r, t), hh * V_DIM:(hh + 1) * V_DIM]
            acc = alpha * acc + jnp.dot(p.astype(BF), v, preferred_element_type=F32)
            return m_new, l, acc

        init = (jnp.full((t, 1), NEG_BIG, F32), jnp.zeros((t, 1), F32), jnp.zeros((t, V_DIM), F32))
        carry = lax.fori_loop(0, qi, lambda ki, c, tile=tile: tile(ki, c, False), init)
        m, l, acc = tile(qi, carry, True)
        o_ref[:, hh * V_DIM:(hh + 1) * V_DIM] = (acc / l).astype(o_ref.dtype)


def _mla_prompt_attention(q, kfull, vfull, batch):
    m = q.shape[0]
    nq = SEQ // FLASH_T
    hp = FLASH_HP
    return pl.pallas_call(
        _flash_kernel,
        grid=(batch, MLA_HEADS // hp, nq),
        in_specs=[
            pl.BlockSpec((FLASH_T, hp * HEAD_PAD), lambda b, h, i: (b * nq + i, h)),
            pl.BlockSpec((SEQ, hp * HEAD_PAD), lambda b, h, i: (b, h)),
            pl.BlockSpec((SEQ, hp * V_DIM), lambda b, h, i: (b, h)),
        ],
        out_specs=pl.BlockSpec((FLASH_T, hp * V_DIM), lambda b, h, i: (b * nq + i, h)),
        out_shape=jax.ShapeDtypeStruct((m, MLA_HEADS * V_DIM), BF),
        compiler_params=_params(("arbitrary", "arbitrary", "arbitrary")),
        name="mla_prompt_attention",
    )(q, kfull, vfull)


def _headwise_kernel(x_ref, w_ref, g_ref, o_ref):
    x = (x_ref[...].astype(F32) * g_ref[...]).astype(BF)
    o_ref[...] = jnp.dot(x, w_ref[...].astype(BF), preferred_element_type=F32).astype(o_ref.dtype)


def _headwise_mm(name, x, x_spec, w, w_spec, g, out_shape, out_spec):
    kx = g.shape[-1]
    return pl.pallas_call(
        _headwise_kernel,
        grid=(MLA_HEADS,),
        in_specs=[x_spec, w_spec, pl.BlockSpec((1, kx), lambda h: (0, 0))],
        out_specs=out_spec,
        out_shape=out_shape,
        compiler_params=_params(("arbitrary",)),
        name=name,
    )(x, w, g)


PAGES_PER_STEP = 16


def _paged_kernel(pt_ref, ql_ref, q_ref, *refs):
    pp = PAGES_PER_STEP
    lat = refs[:pp]
    kr = refs[pp:2 * pp]
    ks = refs[2 * pp:3 * pp]
    cn_ref, krn_ref, ksn_ref, o_ref, m_s, l_s, acc_s = refs[3 * pp:]
    c = pl.program_id(1)
    nc = pl.num_programs(1)

    @pl.when(c == 0)
    def _():
        m_s[...] = jnp.full(m_s.shape, NEG_BIG, F32)
        l_s[...] = jnp.zeros(l_s.shape, F32)
        acc_s[...] = jnp.zeros(acc_s.shape, F32)

    ql = ql_ref[0]
    qr32 = q_ref[0][:, LANES:LANES + ROPE_DIM]
    qr = qr32.astype(BF)
    cps, ss = [], []
    for i in range(pp):
        cp = lat[i][0].astype(BF)
        s = (lax.dot_general(ql, cp, NT_DIMS, preferred_element_type=F32)
             + lax.dot_general(qr, kr[i][0].astype(BF), NT_DIMS, preferred_element_type=F32))
        ss.append(s * ks[i][0] * MLA_SCALE)
        cps.append(cp)
    s = jnp.concatenate(ss, axis=1)
    m_old = m_s[...]
    m_new = jnp.maximum(m_old, jnp.max(s, axis=-1, keepdims=True))
    alpha = jnp.exp(m_old - m_new)
    p = jnp.exp(s - m_new)
    l_new = alpha * l_s[...] + jnp.sum(p, axis=-1, keepdims=True)
    pv = jnp.dot(p[:, 0:PAGE_SIZE].astype(BF), cps[0], preferred_element_type=F32)
    for i in range(1, pp):
        pv = pv + jnp.dot(p[:, i * PAGE_SIZE:(i + 1) * PAGE_SIZE].astype(BF), cps[i],
                          preferred_element_type=F32)
    acc_new = alpha * acc_s[...] + pv
    m_s[...] = m_new
    l_s[...] = l_new
    acc_s[...] = acc_new

    @pl.when(c == nc - 1)
    def _():
        cn = cn_ref[0]
        krn = krn_ref[0][:, 0:ROPE_DIM]
        s_new = (jnp.sum(ql.astype(F32) * cn, axis=-1, keepdims=True)
                 + jnp.sum(qr32 * krn, axis=-1, keepdims=True)) * ksn_ref[0] * MLA_SCALE
        m2 = jnp.maximum(m_new, s_new)
        a2 = jnp.exp(m_new - m2)
        p2 = jnp.exp(s_new - m2)
        l2 = a2 * l_new + p2
        o_ref[0] = ((a2 * acc_new + p2 * cn) / l2).astype(o_ref.dtype)


def _mla_sample_attention(page_table, qlat3, q3, cache_latent, cache_krope, cache_kscale_t,
                          cn3, krn3, ksn3):
    nb, n_pages = page_table.shape
    pp = PAGES_PER_STEP
    nc = n_pages // pp

    def page_spec(shape, i):
        return pl.BlockSpec(shape, lambda b, c, pt, i=i: (pt[b, c * pp + i], 0, 0))

    in_specs = [
        pl.BlockSpec((1, MLA_HEADS, KV_LORA), lambda b, c, pt: (b, 0, 0)),
        pl.BlockSpec((1, MLA_HEADS, HEAD_PAD), lambda b, c, pt: (b, 0, 0)),
    ]
    in_specs += [page_spec((1, PAGE_SIZE, KV_LORA), i) for i in range(pp)]
    in_specs += [page_spec((1, PAGE_SIZE, ROPE_DIM), i) for i in range(pp)]
    in_specs += [page_spec((1, MLA_HEADS, PAGE_SIZE), i) for i in range(pp)]
    in_specs += [
        pl.BlockSpec((1, 1, KV_LORA), lambda b, c, pt: (b, 0, 0)),
        pl.BlockSpec((1, 1, LANES), lambda b, c, pt: (b, 0, 0)),
        pl.BlockSpec((1, MLA_HEADS, 1), lambda b, c, pt: (b, 0, 0)),
    ]
    grid_spec = pltpu.PrefetchScalarGridSpec(
        num_scalar_prefetch=1,
        grid=(nb, nc),
        in_specs=in_specs,
        out_specs=pl.BlockSpec((1, MLA_HEADS, KV_LORA), lambda b, c, pt: (b, 0, 0)),
        scratch_shapes=[pltpu.VMEM((MLA_HEADS, 1), F32), pltpu.VMEM((MLA_HEADS, 1), F32),
                        pltpu.VMEM((MLA_HEADS, KV_LORA), F32)],
    )
    return pl.pallas_call(
        _paged_kernel,
        grid_spec=grid_spec,
        out_shape=jax.ShapeDtypeStruct((nb, MLA_HEADS, KV_LORA), F32),
        compiler_params=_params(("arbitrary", "arbitrary")),
        name="mla_sample_attention",
    )(page_table, qlat3, q3, *([cache_latent] * pp), *([cache_krope] * pp),
      *([cache_kscale_t] * pp), cn3, krn3, ksn3)


def _router_kernel(xp_ref, xs_ref, g_ref, w_ref, h_ref, gate_ref, info_ref, cnt_ref, cnt_s, *, nbp):
    i = pl.program_id(0)

    @pl.when(i == 0)
    def _():
        cnt_s[...] = jnp.zeros(cnt_s.shape, F32)

    xf = jnp.where(i < nbp, xp_ref[...], xs_ref[...])
    ms = jnp.mean(xf * xf, axis=-1, keepdims=True)
    h = xf * lax.rsqrt(ms + EPS) * g_ref[...]
    h_ref[...] = h
    logits = jnp.dot(h.astype(BF), w_ref[...].astype(BF),
                     preferred_element_type=F32)
    lane = lax.broadcasted_iota(jnp.int32, logits.shape, 1)
    big = jnp.int32(LANES)

    is_g = lane < N_GROUPS
    lg = jnp.where(is_g, logits, NEG_BIG)
    eg = jnp.where(is_g, jnp.exp(lg - jnp.max(lg, axis=-1, keepdims=True)), 0.0)
    pg = eg / jnp.sum(eg, axis=-1, keepdims=True)
    p_grp = jnp.max(pg, axis=-1, keepdims=True)
    grp = jnp.min(jnp.where(is_g & (pg == p_grp), lane, big), axis=-1, keepdims=True)

    e_idx = lane - N_GROUPS
    sel = (e_idx >= 0) & (e_idx < N_EXPERTS) & ((e_idx >> 3) == grp)
    le = jnp.where(sel, logits, NEG_BIG)
    ee = jnp.where(sel, jnp.exp(le - jnp.max(le, axis=-1, keepdims=True)), 0.0)
    pe = ee / jnp.sum(ee, axis=-1, keepdims=True)
    top1 = jnp.max(jnp.where(sel, pe, -1.0), axis=-1, keepdims=True)
    i1 = jnp.min(jnp.where(sel & (pe == top1), lane, big), axis=-1, keepdims=True)
    rest = sel & (lane != i1)
    top2 = jnp.max(jnp.where(rest, pe, -1.0), axis=-1, keepdims=True)
    i2 = jnp.min(jnp.where(rest & (pe == top2), lane, big), axis=-1, keepdims=True)
    denom = top1 + top2
    g1 = p_grp * top1 / denom
    g2 = p_grp * top2 / denom
    gate_ref[...] = jnp.where(lane == 0, g1, jnp.where(lane == 1, g2, 0.0))

    onehot = ((lane == i1) | (lane == i2)).astype(F32)
    r_i = lax.broadcasted_iota(jnp.int32, (MOE_BLOCK, MOE_BLOCK), 0)
    c_i = lax.broadcasted_iota(jnp.int32, (MOE_BLOCK, MOE_BLOCK), 1)
    tri = (c_i < r_i).astype(BF)
    before = jnp.dot(tri, onehot.astype(BF), preferred_element_type=F32) + cnt_s[...]
    rank1 = jnp.sum(jnp.where(lane == i1, before, 0.0), axis=-1, keepdims=True).astype(jnp.int32)
    rank2 = jnp.sum(jnp.where(lane == i2, before, 0.0), axis=-1, keepdims=True).astype(jnp.int32)
    info_ref[...] = jnp.where(lane == 0, i1 - N_GROUPS, jnp.where(lane == 1, i2 - N_GROUPS,
                              jnp.where(lane == 2, rank1, jnp.where(lane == 3, rank2, 0))))
    cnt_new = cnt_s[...] + jnp.sum(onehot, axis=0, keepdims=True)
    cnt_s[...] = cnt_new
    cnt_ref[...] = jnp.broadcast_to(cnt_new, cnt_ref.shape)


def _router(xp, xs, g, w_router):
    nbp = xp.shape[0] // MOE_BLOCK
    assert xs.shape[0] == MOE_BLOCK
    mt = xp.shape[0] + xs.shape[0]
    tok_spec = lambda width: pl.BlockSpec((MOE_BLOCK, width), lambda i: (i, 0))
    return pl.pallas_call(
        functools.partial(_router_kernel, nbp=nbp),
        grid=(nbp + 1,),
        in_specs=[
            pl.BlockSpec((MOE_BLOCK, D_MODEL), lambda i: (jnp.minimum(i, nbp - 1), 0)),
            pl.BlockSpec((MOE_BLOCK, D_MODEL), lambda i: (0, 0)),
            pl.BlockSpec((1, D_MODEL), lambda i: (0, 0)),
            pl.BlockSpec((D_MODEL, LANES), lambda i: (0, 0)),
        ],
        out_specs=[tok_spec(D_MODEL), tok_spec(LANES), tok_spec(LANES),
                   pl.BlockSpec((SUBLANES, LANES), lambda i: (0, 0))],
        out_shape=[jax.ShapeDtypeStruct((mt, D_MODEL), F32),
                   jax.ShapeDtypeStruct((mt, LANES), F32),
                   jax.ShapeDtypeStruct((mt, LANES), jnp.int32),
                   jax.ShapeDtypeStruct((SUBLANES, LANES), F32)],
        scratch_shapes=[pltpu.VMEM((1, LANES), F32)],
        compiler_params=_params(("arbitrary",)),
        name="moe_router",
    )(xp, xs, g.reshape(1, D_MODEL), w_router)


def _dispatch_kernel(dest_ref, h_ref, xd_in, xd_ref, sem):
    del xd_in
    n = 2 * MOE_BLOCK

    def copy(t):
        row = lax.rem(t, MOE_BLOCK)
        return pltpu.make_async_copy(h_ref.at[pl.ds(row, 1)], xd_ref.at[pl.ds(dest_ref[0, 0, t], 1)], sem)

    def start(t, carry):
        copy(t).start()
        return carry

    def wait(t, carry):
        copy(t).wait()
        return carry

    lax.fori_loop(0, n, start, 0)
    lax.fori_loop(0, n, wait, 0)


def _dispatch(dest3, h, rows):
    nblk = h.shape[0] // MOE_BLOCK
    xd0 = jnp.zeros((rows, D_MODEL), F32)
    return pl.pallas_call(
        _dispatch_kernel,
        grid=(nblk,),
        in_specs=[
            pl.BlockSpec((1, 1, 2 * MOE_BLOCK), lambda i: (i, 0, 0), memory_space=pltpu.SMEM),
            pl.BlockSpec((MOE_BLOCK, D_MODEL), lambda i: (i, 0)),
            pl.BlockSpec(memory_space=pl.ANY),
        ],
        out_specs=pl.BlockSpec(memory_space=pl.ANY),
        out_shape=jax.ShapeDtypeStruct((rows, D_MODEL), F32),
        scratch_shapes=[pltpu.SemaphoreType.DMA(())],
        input_output_aliases={2: 0},
        compiler_params=_params(("arbitrary",)),
        name="moe_dispatch",
    )(dest3, h, xd0)


def _combine_kernel(dcur_ref, dnxt_ref, gate_ref, xp_ref, xs_ref, yd_ref, op_ref, os_ref, buf, sem, *, nbp):
    i = pl.program_id(0)
    n = 2 * MOE_BLOCK
    slot = lax.rem(i, 2)

    def copy(dref, s, t):
        return pltpu.make_async_copy(yd_ref.at[pl.ds(dref[0, 0, t], 1)], buf.at[s, pl.ds(t, 1)], sem.at[s])

    def start_all(dref, s):
        def body(t, carry):
            copy(dref, s, t).start()
            return carry
        lax.fori_loop(0, n, body, 0)

    @pl.when(i == 0)
    def _():
        start_all(dcur_ref, 0)

    @pl.when(i < nbp)
    def _():
        start_all(dnxt_ref, 1 - slot)

    def wait(t, carry):
        copy(dcur_ref, slot, t).wait()
        return carry
    lax.fori_loop(0, n, wait, 0)

    y = (buf[slot, 0:MOE_BLOCK, :] * gate_ref[:, 0:1]
         + buf[slot, MOE_BLOCK:n, :] * gate_ref[:, 1:2])

    @pl.when(i < nbp)
    def _():
        op_ref[...] = xp_ref[...] + y

    @pl.when(i == nbp)
    def _():
        os_ref[...] = xs_ref[...] + y


def _combine(dest3, gate, xp, xs, yd):
    nbp = xp.shape[0] // MOE_BLOCK
    dest_spec = lambda f: pl.BlockSpec((1, 1, 2 * MOE_BLOCK), f, memory_space=pltpu.SMEM)
    p_spec = pl.BlockSpec((MOE_BLOCK, D_MODEL), lambda i: (jnp.minimum(i, nbp - 1), 0))
    s_spec = pl.BlockSpec((MOE_BLOCK, D_MODEL), lambda i: (0, 0))
    return pl.pallas_call(
        functools.partial(_combine_kernel, nbp=nbp),
        grid=(nbp + 1,),
        in_specs=[
            dest_spec(lambda i: (i, 0, 0)),
            dest_spec(lambda i: (jnp.minimum(i + 1, nbp), 0, 0)),
            pl.BlockSpec((MOE_BLOCK, LANES), lambda i: (i, 0)),
            p_spec, s_spec,
            pl.BlockSpec(memory_space=pl.ANY),
        ],
        out_specs=[p_spec, s_spec],
        out_shape=[jax.ShapeDtypeStruct(xp.shape, F32), jax.ShapeDtypeStruct(xs.shape, F32)],
        scratch_shapes=[pltpu.VMEM((2, 2 * MOE_BLOCK, D_MODEL), F32), pltpu.SemaphoreType.DMA((2,))],
        compiler_params=_params(("arbitrary",)),
        name="moe_combine",
    )(dest3, dest3, gate, xp, xs, yd)


def _expert_kernel(be_ref, nu_ref, x_ref, wg_ref, wu_ref, wd_ref, y_ref, wg_s, wu_s, wd_s):
    b = pl.program_id(0)
    prev = be_ref[jnp.maximum(b - 1, 0)]
    fresh = (b == 0) | (be_ref[b] != prev)

    @pl.when(fresh & (b < nu_ref[0]))
    def _():
        wg_s[...] = wg_ref[...].astype(BF)
        wu_s[...] = wu_ref[...].astype(BF)
        wd_s[...] = wd_ref[...].astype(BF)

    @pl.when(b < nu_ref[0])
    def _():
        x = x_ref[...].astype(BF)
        a = jnp.dot(x, wg_s[...], preferred_element_type=F32)
        u = jnp.dot(x, wu_s[...], preferred_element_type=F32)
        hmid = (a * jax.nn.sigmoid(a) * u).astype(BF)
        y_ref[...] = jnp.dot(hmid, wd_s[...], preferred_element_type=F32)

    @pl.when(b >= nu_ref[0])
    def _():
        y_ref[...] = jnp.zeros(y_ref.shape, F32)


def _experts(blk_e, n_used, xd, w_gate, w_up, w_down, layer):
    rows = xd.shape[0]
    nblk = rows // MOE_BLOCK
    row_map = lambda b, be, nu: (jnp.minimum(b, nu[0] - 1), 0)
    grid_spec = pltpu.PrefetchScalarGridSpec(
        num_scalar_prefetch=2,
        grid=(nblk,),
        in_specs=[
            pl.BlockSpec((MOE_BLOCK, D_MODEL), row_map),
            pl.BlockSpec((None, None, D_MODEL, D_EXPERT), lambda b, be, nu: (layer, be[b], 0, 0)),
            pl.BlockSpec((None, None, D_MODEL, D_EXPERT), lambda b, be, nu: (layer, be[b], 0, 0)),
            pl.BlockSpec((None, None, D_EXPERT, D_MODEL), lambda b, be, nu: (layer, be[b], 0, 0)),
        ],
        out_specs=pl.BlockSpec((MOE_BLOCK, D_MODEL), lambda b, be, nu: (b, 0)),
        scratch_shapes=[pltpu.VMEM((D_MODEL, D_EXPERT), BF), pltpu.VMEM((D_MODEL, D_EXPERT), BF),
                        pltpu.VMEM((D_EXPERT, D_MODEL), BF)],
    )
    return pl.pallas_call(
        _expert_kernel,
        grid_spec=grid_spec,
        out_shape=jax.ShapeDtypeStruct((rows, D_MODEL), F32),
        compiler_params=_params(("arbitrary",)),
        name="moe_experts",
    )(blk_e, n_used, xd, w_gate, w_up, w_down)


def _moe(xp, xs, g_ffn_l, w_router, w_gate, w_up, w_down, layer):
    mt = xp.shape[0] + xs.shape[0]
    n_slots = 2 * mt
    nblk = (n_slots + N_EXPERTS * (MOE_BLOCK - 1) + MOE_BLOCK - 1) // MOE_BLOCK
    h, gate, info, cnt = _router(xp, xs, g_ffn_l, w_router)

    ids = jnp.arange(N_EXPERTS, dtype=jnp.int32)
    counts = cnt[0, N_GROUPS:N_GROUPS + N_EXPERTS].astype(jnp.int32)
    padded = (counts + MOE_BLOCK - 1) // MOE_BLOCK * MOE_BLOCK
    pad_ends = jnp.cumsum(padded)
    pad_starts = pad_ends - padded
    n_used = (pad_ends[-1] // MOE_BLOCK).astype(jnp.int32)
    blk_ids = jnp.arange(nblk, dtype=jnp.int32)
    blk_e = jnp.sum((pad_ends[None, :] <= blk_ids[:, None] * MOE_BLOCK).astype(jnp.int32), axis=1)
    last_e = jnp.max(jnp.where(counts > 0, ids, 0))
    blk_e = jnp.where(blk_ids < n_used, jnp.minimum(blk_e, N_EXPERTS - 1), last_e).astype(jnp.int32)
    e2, rank2 = info[:, 0:2], info[:, 2:4]
    start2 = jnp.sum(jnp.where(e2[:, :, None] == ids[None, None, :], pad_starts[None, None, :], 0), axis=-1)
    dest = start2 + rank2
    dest3 = dest.reshape(mt // MOE_BLOCK, MOE_BLOCK, 2).transpose(0, 2, 1).reshape(
        mt // MOE_BLOCK, 1, 2 * MOE_BLOCK)

    xd = _dispatch(dest3, h, nblk * MOE_BLOCK)
    yd = _experts(blk_e, n_used.reshape(1), xd, w_gate, w_up, w_down, layer)
    return _combine(dest3, gate, xp, xs, yd)


def _rope_tables(pos):
    half = ROPE_DIM // 2
    inv = ROPE_THETA ** (-jnp.arange(half, dtype=F32) / half)
    ang = pos.astype(F32)[:, None] * inv[None, :]
    cos, sin = jnp.cos(ang), jnp.sin(ang)
    z = jnp.zeros_like(cos)
    a = jnp.concatenate([cos, cos, z, z], axis=1)
    b = jnp.concatenate([z, sin, z, z], axis=1)
    c = jnp.concatenate([-sin, z, z, z], axis=1)
    return a, b, c


def _out_spec(tm, tn):
    return pl.BlockSpec((tm, tn), lambda i, j: (i, j))


def _table_specs(tabs, tm, nper):
    return [(t, pl.BlockSpec((tm, LANES), lambda i, j, nper=nper: (i % nper, 0))) for t in tabs]


def _vec_extra(v):
    n = v.shape[-1]
    return (v.reshape(1, n), pl.BlockSpec((1, n), lambda i, j: (0, 0)))


def kernel(x_prompt, x_sample, cache_latent, cache_krope, cache_kscale, cache_mem_k, cache_mem_v, state_conv, page_table, mem_prompt, g_mix, g_ffn, g_mem, w_mem_kv, g_mq, g_mk, w_in_a, conv_w, conv_b, ln_g, ln_b, w_out_a, g_kv_in, w_kv_down, g_kv_lat, w_uk, w_uv, g_k, w_in_b, g_qlat, w_q_up, g_q, w_out_b, w_rg, w_re, w_gate, w_up, w_down):
    batch, seq, _ = x_prompt.shape
    nb = x_sample.shape[0]
    depth = g_mix.shape[0]
    n_a = w_in_a.shape[0]
    mp = batch * seq
    tmp = 1024
    xp = x_prompt.reshape(mp, D_MODEL)
    xs = x_sample.reshape(nb, D_MODEL)

    tabs_p = _rope_tables(jnp.arange(seq))
    tabs_s = _rope_tables(jnp.full((nb,), PAST_LEN))
    w_kvd = jnp.pad(w_kv_down, ((0, 0), (0, LANES - ROPE_DIM)))
    gk_n = g_k[:NOPE_DIM]
    gk_r = jnp.pad(g_k[NOPE_DIM:], (0, LANES - ROPE_DIM))
    w_uk2 = w_uk.reshape(KV_LORA, MLA_HEADS * NOPE_DIM)
    w_uk_t = jnp.transpose(w_uk, (1, 2, 0))
    w_uv2 = w_uv.reshape(KV_LORA, MLA_HEADS * V_DIM)
    w_qu = jnp.pad(w_q_up, ((0, 0), (0, 0), (0, 0), (0, HEAD_PAD - QK_DIM)))
    w_qu = w_qu.reshape(w_q_up.shape[0], Q_LORA, MLA_HEADS * HEAD_PAD)
    gq_pad = jnp.pad(g_q, ((0, 0), (0, HEAD_PAD - QK_DIM)))
    w_router = jnp.pad(jnp.concatenate([w_rg, w_re], axis=-1),
                       ((0, 0), (0, 0), (0, LANES - N_GROUPS - N_EXPERTS)))
    conv_w32 = jnp.pad(conv_w, ((0, 0), (0, 32 - CONV_W), (0, 0)))
    cache_kscale_t = jnp.transpose(cache_kscale, (0, 2, 1))

    memk, memv = _memory_kv_all(mem_prompt.reshape(batch * N_MEM, D_MODEL), g_mem, w_mem_kv, g_mk)

    conv_p_list, conv_s_list = [], []
    shared = None
    for l in range(depth):
        if l < n_a:
            n_glu = D_MODEL
            outs_p, outs_s = [], []
            for x, m, tm, dst in ((xp, mp, tmp, outs_p), (xs, nb, nb, outs_s)):
                tn = 256
                a = _fused_mm(
                    "in_a_glu", [(x, D_MODEL, 0)],
                    [(0, w_in_a, l, D_MODEL, tn, 0, 0), (0, w_in_a, l, D_MODEL, tn, 0, n_glu // tn)],
                    m=m, tm=tm, nj=n_glu // tn, gain=g_mix[l],
                    outs=[(jax.ShapeDtypeStruct((m, n_glu), F32), _out_spec(tm, tn))],
                    epi=_epi_glu)[0]
                qm = _fused_mm(
                    "in_a_memq", [(x, D_MODEL, 0)],
                    [(0, w_in_a, l, D_MODEL, tn, 0, 2 * n_glu // tn)],
                    m=m, tm=tm, nj=MEM_W // tn, gain=g_mix[l],
                    outs=[(jax.ShapeDtypeStruct((m, MEM_W), F32), _out_spec(tm, tn))],
                    epi=_epi_plain)[0]
                dst.extend([a, qm])
            a_p, qm_p = outs_p
            a_s, qm_s = outs_s
            yc_p = _conv_prompt(a_p, conv_w32[l], conv_b[l], ln_g[l], ln_b[l], batch)
            conv_p_list.append(a_p.reshape(batch, seq, D_MODEL)[:, seq - (CONV_W - 1):])
            yc_s, ns = _conv_sample(state_conv[l], a_s, conv_w32[l], conv_b[l], ln_g[l], ln_b[l])
            conv_s_list.append(ns)
            ym_p = _memattn_prompt(qm_p, 0, memk, memv, l, g_mq[l], batch)
            ym_s = _memattn_sample(qm_s.reshape(nb, MEM_HEADS, MEM_HEAD_DIM), cache_mem_k, cache_mem_v,
                                   l, g_mq[l])
            new = []
            for x, m, tm, y1, y2 in ((xp, mp, tmp, yc_p, ym_p), (xs, nb, nb, yc_s, ym_s)):
                tn = 512
                new.append(_fused_mm(
                    "out_a", [(y1, D_MODEL, 0), (y2, MEM_W, 0)],
                    [(0, w_out_a, l, D_MODEL, tn, 0, 0), (1, w_out_a, l, MEM_W, tn, D_MODEL // MEM_W, 0)],
                    m=m, tm=tm, nj=D_MODEL // tn,
                    extras=[(x, _out_spec(tm, tn))],
                    outs=[(jax.ShapeDtypeStruct((m, D_MODEL), F32), _out_spec(tm, tn))],
                    epi=_epi_residual)[0])
            xp, xs = new
        else:
            jb = l - n_a
            kfull, vfull, c_s, kr_s, ks_s = shared
            att = []
            for x, m, tm, tabs, nper, qdt in ((xp, mp, tmp, tabs_p, seq // tmp, BF),
                                              (xs, nb, nb, tabs_s, 1, F32)):
                tn = Q_LORA
                n_in = Q_LORA + MEM_W
                u, qm = _fused_mm(
                    "in_b", [(x, D_MODEL, 0)], [(0, w_in_b, jb, D_MODEL, tn, 0, 0)],
                    m=m, tm=tm, nj=n_in // tn, gain=g_mix[l],
                    outs=[(jax.ShapeDtypeStruct((m, Q_LORA), F32),
                           pl.BlockSpec((tm, tn), lambda i, j: (i, 0))),
                          (jax.ShapeDtypeStruct((m, MEM_W), F32),
                           pl.BlockSpec((tm, tn), lambda i, j: (i, jnp.maximum(j - 1, 0))))],
                    epi=_epi_split)
                qh = _fused_mm(
                    "q_up", [(u, Q_LORA, 0)], [(0, w_qu, jb, Q_LORA, HEAD_PAD, 0, 0)],
                    m=m, tm=tm, nj=MLA_HEADS, gain=g_qlat[jb],
                    extras=[_vec_extra(gq_pad[jb])] + _table_specs(tabs, tm, nper),
                    outs=[(jax.ShapeDtypeStruct((m, MLA_HEADS * HEAD_PAD), qdt), _out_spec(tm, HEAD_PAD))],
                    epi=_epi_qhead)[0]
                att.append((qm, qh))
            (qm_p, qh_p), (qm_s, qh_s) = att
            att_p = _mla_prompt_attention(qh_p, kfull, vfull, batch)
            ym_p = _memattn_prompt(qm_p, 0, memk, memv, l, g_mq[l], batch)
            ym_s = _memattn_sample(qm_s.reshape(nb, MEM_HEADS, MEM_HEAD_DIM), cache_mem_k, cache_mem_v,
                                   l, g_mq[l])

            qlat = _headwise_mm(
                "q_latent", qh_s, pl.BlockSpec((nb, NOPE_DIM), lambda h: (0, 2 * h)),
                w_uk_t, pl.BlockSpec((None, NOPE_DIM, KV_LORA), lambda h: (h, 0, 0)),
                gk_n.reshape(1, NOPE_DIM),
                jax.ShapeDtypeStruct((nb, MLA_HEADS * KV_LORA), BF),
                pl.BlockSpec((nb, KV_LORA), lambda h: (0, h)))
            o_lat = _mla_sample_attention(
                page_table, qlat.reshape(nb, MLA_HEADS, KV_LORA),
                qh_s.reshape(nb, MLA_HEADS, HEAD_PAD), cache_latent, cache_krope, cache_kscale_t,
                c_s.reshape(nb, 1, KV_LORA), kr_s.reshape(nb, 1, LANES),
                ks_s[:, :MLA_HEADS].reshape(nb, MLA_HEADS, 1))
            att_s = _headwise_mm(
                "v_expand", o_lat.reshape(nb, MLA_HEADS * KV_LORA),
                pl.BlockSpec((nb, KV_LORA), lambda h: (0, h)),
                w_uv2, pl.BlockSpec((KV_LORA, V_DIM), lambda h: (0, h)),
                jnp.ones((1, KV_LORA), F32),
                jax.ShapeDtypeStruct((nb, MLA_HEADS * V_DIM), BF),
                pl.BlockSpec((nb, V_DIM), lambda h: (0, h)))
            new = []
            n_att = MLA_HEADS * V_DIM
            for x, m, tm, y1, y2 in ((xp, mp, tmp, att_p, ym_p), (xs, nb, nb, att_s, ym_s)):
                tn = 512
                new.append(_fused_mm(
                    "out_b", [(y1, n_att, 0), (y2, MEM_W, 0)],
                    [(0, w_out_b, jb, n_att, tn, 0, 0), (1, w_out_b, jb, MEM_W, tn, n_att // MEM_W, 0)],
                    m=m, tm=tm, nj=D_MODEL // tn,
                    extras=[(x, _out_spec(tm, tn))],
                    outs=[(jax.ShapeDtypeStruct((m, D_MODEL), F32), _out_spec(tm, tn))],
                    epi=_epi_residual)[0])
            xp, xs = new

        xp, xs = _moe(xp, xs, g_ffn[l], w_router[l], w_gate, w_up, w_down, l)

        if l == n_a - 1:
            lat = []
            for x, m, tm, tabs, nper in ((xp, mp, tmp, tabs_p, seq // tmp), (xs, nb, nb, tabs_s, 1)):
                nck = KV_LORA + LANES
                c, kr128, ssq128 = _fused_mm(
                    "kv_latent", [(x, D_MODEL, 0)], [(0, w_kvd, None, D_MODEL, nck, 0, 0)],
                    m=m, tm=tm, nj=1, gain=g_kv_in,
                    extras=[_vec_extra(g_kv_lat), _vec_extra(gk_r)] + _table_specs(tabs, tm, nper),
                    outs=[(jax.ShapeDtypeStruct((m, KV_LORA), F32), _out_spec(tm, KV_LORA)),
                          (jax.ShapeDtypeStruct((m, LANES), F32), pl.BlockSpec((tm, LANES), lambda i, j: (i, 0))),
                          (jax.ShapeDtypeStruct((m, LANES), F32), pl.BlockSpec((tm, LANES), lambda i, j: (i, 0)))],
                    epi=_epi_latent)
                kf, ks128 = _fused_mm(
                    "k_nope", [(c, KV_LORA, 0)], [(0, w_uk2, None, KV_LORA, 2 * NOPE_DIM, 0, 0)],
                    m=m, tm=tm, nj=MLA_HEADS // 2,
                    extras=[(kr128, pl.BlockSpec((tm, LANES), lambda i, j: (i, 0))),
                            (ssq128, pl.BlockSpec((tm, LANES), lambda i, j: (i, 0))),
                            _vec_extra(gk_n)],
                    outs=[(jax.ShapeDtypeStruct((m, MLA_HEADS * HEAD_PAD), BF), _out_spec(tm, 2 * HEAD_PAD)),
                          (jax.ShapeDtypeStruct((m, LANES), F32), pl.BlockSpec((tm, LANES), lambda i, j: (i, 0)))],
                    epi=_epi_knope)
                lat.append((c, kr128, ks128, kf))
            (c_p, kr_p, ks_p, kfull), (c_s, kr_s, ks_s, _) = lat
            vfull = _fused_mm(
                "v_full", [(c_p, KV_LORA, 0)], [(0, w_uv2, None, KV_LORA, 512, 0, 0)],
                m=mp, tm=tmp, nj=MLA_HEADS * V_DIM // 512,
                outs=[(jax.ShapeDtypeStruct((mp, MLA_HEADS * V_DIM), BF), _out_spec(tmp, 512))],
                epi=_epi_plain)[0]
            shared = (kfull, vfull, c_s, kr_s, ks_s)

    shape5 = (depth, batch, N_MEM, MEM_HEADS, MEM_HEAD_DIM)
    return (xp.reshape(batch, seq, D_MODEL), xs.reshape(nb, 1, D_MODEL),
            c_p.reshape(batch, seq, KV_LORA), kr_p[:, :ROPE_DIM].reshape(batch, seq, ROPE_DIM),
            ks_p[:, :MLA_HEADS].reshape(batch, seq, MLA_HEADS),
            memk.reshape(shape5), memv.reshape(shape5),
            jnp.stack(conv_p_list, axis=0),
            c_s.reshape(nb, 1, KV_LORA), kr_s[:, :ROPE_DIM].reshape(nb, 1, ROPE_DIM),
            ks_s[:, :MLA_HEADS].reshape(nb, 1, MLA_HEADS),
            jnp.stack(conv_s_list, axis=0))
```

```python
import functools
import math

import jax
import jax.numpy as jnp
from jax import lax
from jax.experimental import pallas as pl
from jax.experimental.pallas import tpu as pltpu

F32 = jnp.float32
BF = jnp.bfloat16

D_MODEL = 2048
SEQ = 2048
PAST_LEN = 8192
PAGE_SIZE = 128
CONV_W = 31
N_MEM = 256
MEM_HEADS = 4
MEM_HEAD_DIM = 256
MEM_W = MEM_HEADS * MEM_HEAD_DIM
MLA_HEADS = 16
Q_LORA = 512
KV_LORA = 512
NOPE_DIM = 128
ROPE_DIM = 64
QK_DIM = NOPE_DIM + ROPE_DIM
V_DIM = 128
ROPE_THETA = 10000.0
N_GROUPS = 8
EXPERTS_PER_GROUP = 8
N_EXPERTS = 64
D_EXPERT = 512
MOE_BLOCK = 128
EPS = 1e-6

HEAD_PAD = 256
LANES = 128
VMEM_LIMIT_BYTES = 56 * 1024 * 1024
NEG_BIG = -1e30
NT_DIMS = (((1,), (1,)), ((), ()))


def _params(sem):
    return pltpu.CompilerParams(dimension_semantics=sem, vmem_limit_bytes=VMEM_LIMIT_BYTES)


def _mm_kernel(*refs, nx, w_x, has_gain, n_extra, n_out, epi, tm):
    x_refs = refs[:nx]
    w_refs = refs[nx:nx + len(w_x)]
    pos = nx + len(w_x)
    g_ref = refs[pos] if has_gain else None
    pos += int(has_gain)
    extra_refs = refs[pos:pos + n_extra]
    pos += n_extra
    out_refs = refs[pos:pos + n_out]
    pos += n_out
    h_s = refs[pos] if has_gain else None
    j = pl.program_id(1)

    if has_gain:
        chunk = min(tm, 128)

        @pl.when(j == 0)
        def _():
            def body(c, carry):
                r = pl.multiple_of(c * chunk, chunk)
                xf = x_refs[0][pl.ds(r, chunk), :].astype(F32)
                ms = jnp.mean(xf * xf, axis=-1, keepdims=True)
                h_s[pl.ds(r, chunk), :] = (xf * lax.rsqrt(ms + EPS) * g_ref[...]).astype(BF)
                return carry
            lax.fori_loop(0, tm // chunk, body, 0)

    ds = []
    for wi, xi in enumerate(w_x):
        if has_gain and xi == 0:
            lhs = h_s[...]
        else:
            lhs = x_refs[xi][...].astype(BF)
        ds.append(jnp.dot(lhs, w_refs[wi][...].astype(BF), preferred_element_type=F32))
    epi(ds, extra_refs, out_refs, j)


def _fused_mm(name, xs, ws, *, m, tm, nj, gain=None, extras=(), outs, epi):
    in_arrays, in_specs = [], []
    for arr, k, cb in xs:
        in_arrays.append(arr)
        in_specs.append(pl.BlockSpec((tm, k), lambda i, j, cb=cb: (i, cb)))
    for xi, arr, layer, k, tn, rb, cb0 in ws:
        in_arrays.append(arr)
        if layer is None:
            in_specs.append(pl.BlockSpec((k, tn), lambda i, j, rb=rb, cb0=cb0: (rb, cb0 + j)))
        else:
            in_specs.append(pl.BlockSpec((None, k, tn),
                                         lambda i, j, l=layer, rb=rb, cb0=cb0: (l, rb, cb0 + j)))
    scratch = []
    if gain is not None:
        k0 = xs[0][1]
        in_arrays.append(gain.reshape(1, k0).astype(F32))
        in_specs.append(pl.BlockSpec((1, k0), lambda i, j: (0, 0)))
        scratch.append(pltpu.VMEM((tm, k0), BF))
    for arr, spec in extras:
        in_arrays.append(arr)
        in_specs.append(spec)
    kern = functools.partial(_mm_kernel, nx=len(xs), w_x=tuple(w[0] for w in ws),
                             has_gain=gain is not None, n_extra=len(extras),
                             n_out=len(outs), epi=epi, tm=tm)
    res = pl.pallas_call(
        kern,
        grid=(m // tm, nj),
        in_specs=in_specs,
        out_specs=[o[1] for o in outs],
        out_shape=[o[0] for o in outs],
        scratch_shapes=scratch,
        compiler_params=_params(("arbitrary", "arbitrary")),
        name=name,
    )(*in_arrays)
    return res


def _epi_plain(ds, ex, outs, j):
    acc = ds[0]
    for d in ds[1:]:
        acc = acc + d
    outs[0][...] = acc.astype(outs[0].dtype)


def _epi_residual(ds, ex, outs, j):
    acc = ds[0]
    for d in ds[1:]:
        acc = acc + d
    outs[0][...] = (ex[0][...] + acc).astype(outs[0].dtype)


def _epi_split(ds, ex, outs, j):
    @pl.when(j == 0)
    def _():
        outs[0][...] = ds[0]

    @pl.when(j > 0)
    def _():
        outs[1][...] = ds[0]


def _epi_glu(ds, ex, outs, j):
    outs[0][...] = (ds[0] * jax.nn.sigmoid(ds[1])).astype(outs[0].dtype)


def _rope_tile(t, a_ref, b_ref, c_ref):
    return (t * a_ref[...] + pltpu.roll(t, 32, 1) * b_ref[...]
            + pltpu.roll(t, 96, 1) * c_ref[...])


def _epi_qhead(ds, ex, outs, j):
    acc = ds[0]
    gq_ref, a_ref, b_ref, c_ref = ex
    ssq = jnp.sum(acc * acc, axis=-1, keepdims=True)
    qn = acc * lax.rsqrt(ssq * (1.0 / QK_DIM) + EPS) * gq_ref[...]
    outs[0][:, :LANES] = qn[:, :LANES].astype(outs[0].dtype)
    outs[0][:, LANES:] = _rope_tile(qn[:, LANES:], a_ref, b_ref, c_ref).astype(outs[0].dtype)


def _epi_latent(ds, ex, outs, j):
    ck = ds[0]
    glat_ref, gkr_ref, a_ref, b_ref, c_ref = ex
    c_raw = ck[:, :KV_LORA]
    ms = jnp.mean(c_raw * c_raw, axis=-1, keepdims=True)
    outs[0][...] = c_raw * lax.rsqrt(ms + EPS) * glat_ref[...]
    krt = ck[:, KV_LORA:]
    ssq = jnp.sum(krt * krt, axis=-1, keepdims=True)
    outs[1][...] = _rope_tile(krt * gkr_ref[...], a_ref, b_ref, c_ref)
    outs[2][...] = jnp.broadcast_to(ssq, outs[2].shape)


def _epi_knope(ds, ex, outs, j):
    acc = ds[0]
    kr_ref, ssq_ref, gkn_ref = ex
    ssq_r = ssq_ref[:, 0:1]
    kr = kr_ref[...]
    lane = lax.broadcasted_iota(jnp.int32, outs[1].shape, 1)

    @pl.when(j == 0)
    def _():
        outs[1][...] = jnp.zeros(outs[1].shape, F32)

    ks_all = outs[1][...]
    for t in range(2):
        kn = acc[:, t * NOPE_DIM:(t + 1) * NOPE_DIM]
        ssq = jnp.sum(kn * kn, axis=-1, keepdims=True) + ssq_r
        ks = lax.rsqrt(ssq * (1.0 / QK_DIM) + EPS)
        outs[0][:, t * HEAD_PAD:t * HEAD_PAD + LANES] = (kn * gkn_ref[...] * ks).astype(outs[0].dtype)
        outs[0][:, t * HEAD_PAD + LANES:(t + 1) * HEAD_PAD] = (kr * ks).astype(outs[0].dtype)
        ks_all = jnp.where(lane == 2 * j + t, ks, ks_all)
    outs[1][...] = ks_all


def _memkv_kernel(x_ref, g_ref, wk_ref, wv_ref, gk_ref, k_out, v_out, h_s):
    j = pl.program_id(1)
    rows = x_ref.shape[0]
    chunk = 128

    @pl.when(j == 0)
    def _():
        def body(c, carry):
            r = pl.multiple_of(c * chunk, chunk)
            xf = x_ref[pl.ds(r, chunk), :]
            ms = jnp.mean(xf * xf, axis=-1, keepdims=True)
            h_s[pl.ds(r, chunk), :] = (xf * lax.rsqrt(ms + EPS) * g_ref[...]).astype(BF)
            return carry
        lax.fori_loop(0, rows // chunk, body, 0)

    h = h_s[...]
    k = jnp.dot(h, wk_ref[...].astype(BF), preferred_element_type=F32)
    ms = jnp.mean(k * k, axis=-1, keepdims=True)
    k_out[...] = k * lax.rsqrt(ms + EPS) * gk_ref[...]
    v_out[...] = jnp.dot(h, wv_ref[...].astype(BF), preferred_element_type=F32)


def _memory_kv_all(mem2d, g_mem, w_mem_kv, g_mk):
    depth = w_mem_kv.shape[0]
    rows = mem2d.shape[0]
    hd = MEM_HEAD_DIM
    out = pl.pallas_call(
        _memkv_kernel,
        grid=(depth, MEM_HEADS),
        in_specs=[
            pl.BlockSpec((rows, D_MODEL), lambda l, j: (0, 0)),
            pl.BlockSpec((None, 1, D_MODEL), lambda l, j: (l, 0, 0)),
            pl.BlockSpec((None, D_MODEL, hd), lambda l, j: (l, 0, j)),
            pl.BlockSpec((None, D_MODEL, hd), lambda l, j: (l, 0, MEM_HEADS + j)),
            pl.BlockSpec((None, 1, hd), lambda l, j: (l, 0, 0)),
        ],
        out_specs=[
            pl.BlockSpec((None, rows, hd), lambda l, j: (l, 0, j)),
            pl.BlockSpec((None, rows, hd), lambda l, j: (l, 0, j)),
        ],
        out_shape=[jax.ShapeDtypeStruct((depth, rows, MEM_W), F32)] * 2,
        scratch_shapes=[pltpu.VMEM((rows, D_MODEL), BF)],
        compiler_params=_params(("arbitrary", "arbitrary")),
        name="memory_kv",
    )(mem2d, g_mem.reshape(depth, 1, D_MODEL), w_mem_kv, w_mem_kv, g_mk.reshape(depth, 1, hd))
    return out


CONV_TT = 256
CONV_RC = 16
CONV_HALO = 32
SUBLANES = 8


def _conv_prompt_kernel(a_ref, w_ref, cb_ref, lg_ref, lb_ref, o_ref, sh):
    t = pl.program_id(1)
    tt = CONV_TT
    span = tt + CONV_HALO

    @pl.when(t == 0)
    def _():
        sh[0, 0:CONV_HALO, :] = jnp.zeros((CONV_HALO, D_MODEL), F32)
        sh[0, span:span + SUBLANES, :] = jnp.zeros((SUBLANES, D_MODEL), F32)

    sh[0, CONV_HALO:span, :] = a_ref[...]
    for p in range(1, SUBLANES):
        sh[p, 0:span, :] = sh[0, p:p + span, :]

    def body(c, carry):
        r = c * CONV_RC
        accs = [jnp.zeros((SUBLANES, D_MODEL), F32) for _ in range(CONV_RC // SUBLANES)]
        for k in range(CONV_W):
            off = 2 + k
            p = off % SUBLANES
            w = w_ref[k * SUBLANES:(k + 1) * SUBLANES, :]
            for g in range(CONV_RC // SUBLANES):
                start = pl.multiple_of(r + (off - p) + g * SUBLANES, SUBLANES)
                accs[g] = accs[g] + sh[p, pl.ds(start, SUBLANES), :] * w
        y = jnp.concatenate(accs, axis=0) + cb_ref[...]
        mu = jnp.mean(y, axis=-1, keepdims=True)
        yc = y - mu
        var = jnp.mean(yc * yc, axis=-1, keepdims=True)
        z = yc * lax.rsqrt(var + EPS) * lg_ref[...] + lb_ref[...]
        o_ref[pl.ds(pl.multiple_of(r, CONV_RC), CONV_RC), :] = (z * jax.nn.sigmoid(z)).astype(o_ref.dtype)
        return carry

    lax.fori_loop(0, tt // CONV_RC, body, 0)
    sh[0, 0:CONV_HALO, :] = sh[0, tt:span, :]


def _conv_prompt(a, w32, cb, lg, lb, batch):
    m = a.shape[0]
    nt = SEQ // CONV_TT
    vec = lambda v: v.reshape(1, D_MODEL)
    w_rep = jnp.repeat(w32, SUBLANES, axis=0)
    return pl.pallas_call(
        _conv_prompt_kernel,
        grid=(batch, nt),
        in_specs=[
            pl.BlockSpec((CONV_TT, D_MODEL), lambda b, t: (b * nt + t, 0)),
            pl.BlockSpec((32 * SUBLANES, D_MODEL), lambda b, t: (0, 0)),
            pl.BlockSpec((1, D_MODEL), lambda b, t: (0, 0)),
            pl.BlockSpec((1, D_MODEL), lambda b, t: (0, 0)),
            pl.BlockSpec((1, D_MODEL), lambda b, t: (0, 0)),
        ],
        out_specs=pl.BlockSpec((CONV_TT, D_MODEL), lambda b, t: (b * nt + t, 0)),
        out_shape=jax.ShapeDtypeStruct((m, D_MODEL), BF),
        scratch_shapes=[pltpu.VMEM((SUBLANES, CONV_TT + CONV_HALO + SUBLANES, D_MODEL), F32)],
        compiler_params=_params(("arbitrary", "arbitrary")),
        name="conv_prompt",
    )(a, w_rep, vec(cb), vec(lg), vec(lb))


CONV_SB = 8


def _conv_sample_kernel(st_ref, a_ref, w_ref, cb_ref, lg_ref, lb_ref, y_ref, ns_ref):
    nstate = CONV_W - 1
    for s in range(CONV_SB):
        st = st_ref[s]
        a = a_ref[s]
        y = (jnp.sum(st * w_ref[0:nstate, :], axis=0, keepdims=True)
             + a * w_ref[nstate:nstate + 1, :] + cb_ref[...])
        mu = jnp.mean(y, axis=-1, keepdims=True)
        yc = y - mu
        var = jnp.mean(yc * yc, axis=-1, keepdims=True)
        z = yc * lax.rsqrt(var + EPS) * lg_ref[...] + lb_ref[...]
        y_ref[s] = (z * jax.nn.sigmoid(z)).astype(y_ref.dtype)
        ns_ref[s, 0:nstate - 1, :] = st_ref[s, 1:nstate, :]
        ns_ref[s, nstate - 1:nstate, :] = a


def _conv_sample(state, a, w32, cb, lg, lb):
    nb = state.shape[0]
    nstate = CONV_W - 1
    vec = lambda v: v.reshape(1, D_MODEL)
    y, ns = pl.pallas_call(
        _conv_sample_kernel,
        grid=(nb // CONV_SB,),
        in_specs=[
            pl.BlockSpec((CONV_SB, nstate, D_MODEL), lambda i: (i, 0, 0)),
            pl.BlockSpec((CONV_SB, 1, D_MODEL), lambda i: (i, 0, 0)),
            pl.BlockSpec((32, D_MODEL), lambda i: (0, 0)),
            pl.BlockSpec((1, D_MODEL), lambda i: (0, 0)),
            pl.BlockSpec((1, D_MODEL), lambda i: (0, 0)),
            pl.BlockSpec((1, D_MODEL), lambda i: (0, 0)),
        ],
        out_specs=[
            pl.BlockSpec((CONV_SB, 1, D_MODEL), lambda i: (i, 0, 0)),
            pl.BlockSpec((CONV_SB, nstate, D_MODEL), lambda i: (i, 0, 0)),
        ],
        out_shape=[jax.ShapeDtypeStruct((nb, 1, D_MODEL), BF),
                   jax.ShapeDtypeStruct((nb, nstate, D_MODEL), F32)],
        compiler_params=_params(("arbitrary",)),
        name="conv_sample",
    )(state, a.reshape(nb, 1, D_MODEL), w32, vec(cb), vec(lg), vec(lb))
    return y.reshape(nb, D_MODEL), ns


MEM_TQ = 512
MEM_SCALE = MEM_HEAD_DIM ** -0.5


def _memattn_prompt_kernel(q_ref, k_ref, v_ref, g_ref, o_ref):
    for h in range(MEM_HEADS):
        sl = slice(h * MEM_HEAD_DIM, (h + 1) * MEM_HEAD_DIM)
        qh = q_ref[:, sl]
        ms = jnp.mean(qh * qh, axis=-1, keepdims=True)
        qn = (qh * lax.rsqrt(ms + EPS) * g_ref[...]).astype(BF)
        s = lax.dot_general(qn, k_ref[:, sl].astype(BF), NT_DIMS,
                            preferred_element_type=F32) * MEM_SCALE
        mx = jnp.max(s, axis=-1, keepdims=True)
        p = jnp.exp(s - mx)
        l = jnp.sum(p, axis=-1, keepdims=True)
        o = jnp.dot(p.astype(BF), v_ref[:, sl].astype(BF), preferred_element_type=F32)
        o_ref[:, sl] = (o / l).astype(o_ref.dtype)


def _memattn_prompt(q, q_colblk, memk, memv, layer, g_mq_l, batch):
    m = q.shape[0]
    nq = SEQ // MEM_TQ
    return pl.pallas_call(
        _memattn_prompt_kernel,
        grid=(batch, nq),
        in_specs=[
            pl.BlockSpec((MEM_TQ, MEM_W), lambda b, i: (b * nq + i, q_colblk)),
            pl.BlockSpec((None, N_MEM, MEM_W), lambda b, i: (layer, b, 0)),
            pl.BlockSpec((None, N_MEM, MEM_W), lambda b, i: (layer, b, 0)),
            pl.BlockSpec((1, MEM_HEAD_DIM), lambda b, i: (0, 0)),
        ],
        out_specs=pl.BlockSpec((MEM_TQ, MEM_W), lambda b, i: (b * nq + i, 0)),
        out_shape=jax.ShapeDtypeStruct((m, MEM_W), BF),
        compiler_params=_params(("arbitrary", "arbitrary")),
        name="memattn_prompt",
    )(q, memk, memv, g_mq_l.reshape(1, MEM_HEAD_DIM))


MEM_SB = 4


def _memattn_sample_kernel(q_ref, k_ref, v_ref, g_ref, o_ref):
    for s in range(MEM_SB):
        q = q_ref[s]
        ms = jnp.mean(q * q, axis=-1, keepdims=True)
        qn = q * lax.rsqrt(ms + EPS) * g_ref[...]
        sc = jnp.sum(k_ref[s] * qn[None], axis=-1, keepdims=True) * MEM_SCALE
        mx = jnp.max(sc, axis=0, keepdims=True)
        p = jnp.exp(sc - mx)
        l = jnp.sum(p, axis=0)
        o = jnp.sum(p * v_ref[s], axis=0) / l
        o_ref[s] = o.astype(o_ref.dtype)


def _memattn_sample(q3, cmk, cmv, layer, g_mq_l):
    nb = q3.shape[0]
    kv_spec = pl.BlockSpec((None, MEM_SB, N_MEM, MEM_HEADS, MEM_HEAD_DIM), lambda i: (layer, i, 0, 0, 0))
    out = pl.pallas_call(
        _memattn_sample_kernel,
        grid=(nb // MEM_SB,),
        in_specs=[
            pl.BlockSpec((MEM_SB, MEM_HEADS, MEM_HEAD_DIM), lambda i: (i, 0, 0)),
            kv_spec, kv_spec,
            pl.BlockSpec((1, MEM_HEAD_DIM), lambda i: (0, 0)),
        ],
        out_specs=pl.BlockSpec((MEM_SB, MEM_HEADS, MEM_HEAD_DIM), lambda i: (i, 0, 0)),
        out_shape=jax.ShapeDtypeStruct((nb, MEM_HEADS, MEM_HEAD_DIM), BF),
        compiler_params=_params(("arbitrary",)),
        name="memattn_sample",
    )(q3, cmk, cmv, g_mq_l.reshape(1, MEM_HEAD_DIM))
    return out.reshape(nb, MEM_W)


FLASH_T = 512
FLASH_HP = 2
MLA_SCALE = QK_DIM ** -0.5


def _flash_kernel(q_ref, k_ref, v_ref, o_ref):
    qi = pl.program_id(2)
    t = FLASH_T
    row = lax.broadcasted_iota(jnp.int32, (t, t), 0)
    col = lax.broadcasted_iota(jnp.int32, (t, t), 1)

    for hh in range(FLASH_HP):
        q = q_ref[:, hh * HEAD_PAD:(hh + 1) * HEAD_PAD]

        def tile(ki, carry, diagonal, hh=hh, q=q):
            m, l, acc = carry
            r = pl.multiple_of(ki * t, t)
            k = k_ref[pl.ds(r, t), hh * HEAD_PAD:(hh + 1) * HEAD_PAD]
            s = lax.dot_general(q, k, NT_DIMS, preferred_element_type=F32) * MLA_SCALE
            if diagonal:
                s = jnp.where(col <= row, s, NEG_BIG)
            m_new = jnp.maximum(m, jnp.max(s, axis=-1, keepdims=True))
            alpha = jnp.exp(m - m_new)
            p = jnp.exp(s - m_new)
            l = alpha * l + jnp.sum(p, axis=-1, keepdims=True)
            v = v_ref[pl.ds(r, t), hh * V_DIM:(hh + 1) * V_DIM]
            acc = alpha * acc + jnp.dot(p.astype(BF), v, preferred_element_type=F32)
            return m_new, l, acc

        init = (jnp.full((t, 1), NEG_BIG, F32), jnp.zeros((t, 1), F32), jnp.zeros((t, V_DIM), F32))
        carry = lax.fori_loop(0, qi, lambda ki, c, tile=tile: tile(ki, c, False), init)
        m, l, acc = tile(qi, carry, True)
        o_ref[:, hh * V_DIM:(hh + 1) * V_DIM] = (acc / l).astype(o_ref.dtype)


def _mla_prompt_attention(q, kfull, vfull, batch):
    m = q.shape[0]
    nq = SEQ // FLASH_T
    hp = FLASH_HP
    return pl.pallas_call(
        _flash_kernel,
        grid=(batch, MLA_HEADS // hp, nq),
        in_specs=[
            pl.BlockSpec((FLASH_T, hp * HEAD_PAD), lambda b, h, i: (b * nq + i, h)),
            pl.BlockSpec((SEQ, hp * HEAD_PAD), lambda b, h, i: (b, h)),
            pl.BlockSpec((SEQ, hp * V_DIM), lambda b, h, i: (b, h)),
        ],
        out_specs=pl.BlockSpec((FLASH_T, hp * V_DIM), lambda b, h, i: (b * nq + i, h)),
        out_shape=jax.ShapeDtypeStruct((m, MLA_HEADS * V_DIM), BF),
        compiler_params=_params(("arbitrary", "arbitrary", "arbitrary")),
        name="mla_prompt_attention",
    )(q, kfull, vfull)


def _headwise_kernel(x_ref, w_ref, g_ref, o_ref):
    x = (x_ref[...].astype(F32) * g_ref[...]).astype(BF)
    o_ref[...] = jnp.dot(x, w_ref[...].astype(BF), preferred_element_type=F32).astype(o_ref.dtype)


def _headwise_mm(name, x, x_spec, w, w_spec, g, out_shape, out_spec):
    kx = g.shape[-1]
    return pl.pallas_call(
        _headwise_kernel,
        grid=(MLA_HEADS,),
        in_specs=[x_spec, w_spec, pl.BlockSpec((1, kx), lambda h: (0, 0))],
        out_specs=out_spec,
        out_shape=out_shape,
        compiler_params=_params(("arbitrary",)),
        name=name,
    )(x, w, g)


PAGES_PER_STEP = 32


def _paged_kernel(pt_ref, ql_ref, q_ref, cn_ref, krn_ref, ksn_ref, lat_hbm, kr_hbm, ks_hbm, o_ref,
                  m_s, l_s, acc_s, cbuf, krbuf, ksbuf, lat_buf, kr_buf, ks_buf, sem):
    pp = PAGES_PER_STEP
    b = pl.program_id(0)
    c = pl.program_id(1)
    nb = pl.num_programs(0)
    nc = pl.num_programs(1)
    step = b * nc + c
    slot = lax.rem(step, 2)

    def fetch(bb, cc, s):
        for i in range(pp):
            pid = pt_ref[bb, cc * pp + i]
            pltpu.make_async_copy(lat_hbm.at[pid], lat_buf.at[s, i], sem.at[s]).start()
            pltpu.make_async_copy(kr_hbm.at[pid], kr_buf.at[s, i], sem.at[s]).start()
            pltpu.make_async_copy(ks_hbm.at[pid], ks_buf.at[s, i], sem.at[s]).start()

    def wait(s):
        pltpu.make_async_copy(lat_hbm.at[pl.ds(0, pp)], lat_buf.at[s], sem.at[s]).wait()
        pltpu.make_async_copy(kr_hbm.at[pl.ds(0, pp)], kr_buf.at[s], sem.at[s]).wait()
        pltpu.make_async_copy(ks_hbm.at[pl.ds(0, pp)], ks_buf.at[s], sem.at[s]).wait()

    @pl.when(step == 0)
    def _():
        fetch(b, c, 0)

    last = step == nb * nc - 1
    wrap = c == nc - 1
    nb_ = jnp.where(last, b, jnp.where(wrap, b + 1, b))
    nc_ = jnp.where(last, c, jnp.where(wrap, 0, c + 1))
    fetch(nb_, nc_, 1 - slot)
    wait(slot)

    @pl.when(c == 0)
    def _():
        m_s[...] = jnp.full(m_s.shape, NEG_BIG, F32)
        l_s[...] = jnp.zeros(l_s.shape, F32)
        acc_s[...] = jnp.zeros(acc_s.shape, F32)

    ql = ql_ref[0]
    qr32 = q_ref[0][:, LANES:LANES + ROPE_DIM]
    qr = qr32.astype(BF)
    for i in range(pp):
        rows = slice(i * PAGE_SIZE, (i + 1) * PAGE_SIZE)
        cbuf[rows, :] = lat_buf[slot, i].astype(BF)
        krbuf[rows, :] = kr_buf[slot, i].astype(BF)
        ksbuf[:, rows] = ks_buf[slot, i]
    keys = cbuf[...]
    s = (lax.dot_general(ql, keys, NT_DIMS, preferred_element_type=F32)
         + lax.dot_general(qr, krbuf[...], NT_DIMS, preferred_element_type=F32))
    s = s * ksbuf[...] * MLA_SCALE
    m_old = m_s[...]
    m_new = jnp.maximum(m_old, jnp.max(s, axis=-1, keepdims=True))
    alpha = jnp.exp(m_old - m_new)
    p = jnp.exp(s - m_new)
    l_new = alpha * l_s[...] + jnp.sum(p, axis=-1, keepdims=True)
    pv = jnp.dot(p.astype(BF), keys, preferred_element_type=F32)
    acc_new = alpha * acc_s[...] + pv
    m_s[...] = m_new
    l_s[...] = l_new
    acc_s[...] = acc_new

    @pl.when(c == nc - 1)
    def _():
        cn = cn_ref[0]
        krn = krn_ref[0][:, 0:ROPE_DIM]
        s_new = (jnp.sum(ql.astype(F32) * cn, axis=-1, keepdims=True)
                 + jnp.sum(qr32 * krn, axis=-1, keepdims=True)) * ksn_ref[0] * MLA_SCALE
        m2 = jnp.maximum(m_new, s_new)
        a2 = jnp.exp(m_new - m2)
        p2 = jnp.exp(s_new - m2)
        l2 = a2 * l_new + p2
        o_ref[0] = ((a2 * acc_new + p2 * cn) / l2).astype(o_ref.dtype)

    @pl.when(last)
    def _():
        wait(1 - slot)


def _mla_sample_attention(page_table, qlat3, q3, cache_latent, cache_krope, cache_kscale_t,
                          cn3, krn3, ksn3):
    nb, n_pages = page_table.shape
    pp = PAGES_PER_STEP
    nc = n_pages // pp
    any_spec = pl.BlockSpec(memory_space=pl.ANY)
    in_specs = [
        pl.BlockSpec((1, MLA_HEADS, KV_LORA), lambda b, c, pt: (b, 0, 0)),
        pl.BlockSpec((1, MLA_HEADS, HEAD_PAD), lambda b, c, pt: (b, 0, 0)),
        pl.BlockSpec((1, 1, KV_LORA), lambda b, c, pt: (b, 0, 0)),
        pl.BlockSpec((1, 1, LANES), lambda b, c, pt: (b, 0, 0)),
        pl.BlockSpec((1, MLA_HEADS, 1), lambda b, c, pt: (b, 0, 0)),
        any_spec, any_spec, any_spec,
    ]
    grid_spec = pltpu.PrefetchScalarGridSpec(
        num_scalar_prefetch=1,
        grid=(nb, nc),
        in_specs=in_specs,
        out_specs=pl.BlockSpec((1, MLA_HEADS, KV_LORA), lambda b, c, pt: (b, 0, 0)),
        scratch_shapes=[pltpu.VMEM((MLA_HEADS, 1), F32), pltpu.VMEM((MLA_HEADS, 1), F32),
                        pltpu.VMEM((MLA_HEADS, KV_LORA), F32),
                        pltpu.VMEM((pp * PAGE_SIZE, KV_LORA), BF), pltpu.VMEM((pp * PAGE_SIZE, ROPE_DIM), BF),
                        pltpu.VMEM((MLA_HEADS, pp * PAGE_SIZE), F32),
                        pltpu.VMEM((2, pp, PAGE_SIZE, KV_LORA), F32),
                        pltpu.VMEM((2, pp, PAGE_SIZE, ROPE_DIM), F32),
                        pltpu.VMEM((2, pp, MLA_HEADS, PAGE_SIZE), F32),
                        pltpu.SemaphoreType.DMA((2,))],
    )
    return pl.pallas_call(
        _paged_kernel,
        grid_spec=grid_spec,
        out_shape=jax.ShapeDtypeStruct((nb, MLA_HEADS, KV_LORA), F32),
        compiler_params=_params(("arbitrary", "arbitrary")),
        name="mla_sample_attention",
    )(page_table, qlat3, q3, cn3, krn3, ksn3, cache_latent, cache_krope, cache_kscale_t)


def _router_kernel(xp_ref, xs_ref, g_ref, w_ref, h_ref, gate_ref, info_ref, cnt_ref, cnt_s, *, nbp):
    i = pl.program_id(0)

    @pl.when(i == 0)
    def _():
        cnt_s[...] = jnp.zeros(cnt_s.shape, F32)

    xf = jnp.where(i < nbp, xp_ref[...], xs_ref[...])
    ms = jnp.mean(xf * xf, axis=-1, keepdims=True)
    h = xf * lax.rsqrt(ms + EPS) * g_ref[...]
    h_ref[...] = h
    logits = jnp.dot(h.astype(BF), w_ref[...].astype(BF),
                     preferred_element_type=F32)
    lane = lax.broadcasted_iota(jnp.int32, logits.shape, 1)
    big = jnp.int32(LANES)

    is_g = lane < N_GROUPS
    lg = jnp.where(is_g, logits, NEG_BIG)
    eg = jnp.where(is_g, jnp.exp(lg - jnp.max(lg, axis=-1, keepdims=True)), 0.0)
    pg = eg / jnp.sum(eg, axis=-1, keepdims=True)
    p_grp = jnp.max(pg, axis=-1, keepdims=True)
    grp = jnp.min(jnp.where(is_g & (pg == p_grp), lane, big), axis=-1, keepdims=True)

    e_idx = lane - N_GROUPS
    sel = (e_idx >= 0) & (e_idx < N_EXPERTS) & ((e_idx >> 3) == grp)
    le = jnp.where(sel, logits, NEG_BIG)
    ee = jnp.where(sel, jnp.exp(le - jnp.max(le, axis=-1, keepdims=True)), 0.0)
    pe = ee / jnp.sum(ee, axis=-1, keepdims=True)
    top1 = jnp.max(jnp.where(sel, pe, -1.0), axis=-1, keepdims=True)
    i1 = jnp.min(jnp.where(sel & (pe == top1), lane, big), axis=-1, keepdims=True)
    rest = sel & (lane != i1)
    top2 = jnp.max(jnp.where(rest, pe, -1.0), axis=-1, keepdims=True)
    i2 = jnp.min(jnp.where(rest & (pe == top2), lane, big), axis=-1, keepdims=True)
    denom = top1 + top2
    g1 = p_grp * top1 / denom
    g2 = p_grp * top2 / denom
    gate_ref[...] = jnp.where(lane == 0, g1, jnp.where(lane == 1, g2, 0.0))

    onehot = ((lane == i1) | (lane == i2)).astype(F32)
    r_i = lax.broadcasted_iota(jnp.int32, (MOE_BLOCK, MOE_BLOCK), 0)
    c_i = lax.broadcasted_iota(jnp.int32, (MOE_BLOCK, MOE_BLOCK), 1)
    tri = (c_i < r_i).astype(BF)
    before = jnp.dot(tri, onehot.astype(BF), preferred_element_type=F32) + cnt_s[...]
    rank1 = jnp.sum(jnp.where(lane == i1, before, 0.0), axis=-1, keepdims=True).astype(jnp.int32)
    rank2 = jnp.sum(jnp.where(lane == i2, before, 0.0), axis=-1, keepdims=True).astype(jnp.int32)
    info_ref[...] = jnp.where(lane == 0, i1 - N_GROUPS, jnp.where(lane == 1, i2 - N_GROUPS,
                              jnp.where(lane == 2, rank1, jnp.where(lane == 3, rank2, 0))))
    cnt_new = cnt_s[...] + jnp.sum(onehot, axis=0, keepdims=True)
    cnt_s[...] = cnt_new
    cnt_ref[...] = jnp.broadcast_to(cnt_new, cnt_ref.shape)


def _router(xp, xs, g, w_router):
    nbp = xp.shape[0] // MOE_BLOCK
    assert xs.shape[0] == MOE_BLOCK
    mt = xp.shape[0] + xs.shape[0]
    tok_spec = lambda width: pl.BlockSpec((MOE_BLOCK, width), lambda i: (i, 0))
    return pl.pallas_call(
        functools.partial(_router_kernel, nbp=nbp),
        grid=(nbp + 1,),
        in_specs=[
            pl.BlockSpec((MOE_BLOCK, D_MODEL), lambda i: (jnp.minimum(i, nbp - 1), 0)),
            pl.BlockSpec((MOE_BLOCK, D_MODEL), lambda i: (0, 0)),
            pl.BlockSpec((1, D_MODEL), lambda i: (0, 0)),
            pl.BlockSpec((D_MODEL, LANES), lambda i: (0, 0)),
        ],
        out_specs=[tok_spec(D_MODEL), tok_spec(LANES), tok_spec(LANES),
                   pl.BlockSpec((SUBLANES, LANES), lambda i: (0, 0))],
        out_shape=[jax.ShapeDtypeStruct((mt, D_MODEL), F32),
                   jax.ShapeDtypeStruct((mt, LANES), F32),
                   jax.ShapeDtypeStruct((mt, LANES), jnp.int32),
                   jax.ShapeDtypeStruct((SUBLANES, LANES), F32)],
        scratch_shapes=[pltpu.VMEM((1, LANES), F32)],
        compiler_params=_params(("arbitrary",)),
        name="moe_router",
    )(xp, xs, g.reshape(1, D_MODEL), w_router)


TRASH_ROWS = 2 * MOE_BLOCK


def _expert_kernel(be_ref, nu_ref, par_ref, nxt_ref, src_ref, srcn_ref, dst_ref,
                   h_ref, wg_hbm, wu_hbm, wd_hbm, ys_ref,
                   xbuf, ybuf, wg_f, wu_f, wd_f, wg_s, wu_s, wd_s, gsem, ssem, wsem, *, layer, n_slots):
    b = pl.program_id(0)
    nu = nu_ref[0]
    rows = MOE_BLOCK

    def gather_start(tab_ref, slot):
        def body(t, carry):
            pltpu.make_async_copy(h_ref.at[pl.ds(tab_ref[0, 0, t], 1)], xbuf.at[slot, pl.ds(t, 1)],
                                  gsem.at[slot]).start()
            return carry
        lax.fori_loop(0, rows, body, 0, unroll=8)

    def weight_copies(e, slot):
        return [pltpu.make_async_copy(w.at[layer, e], f.at[slot], wsem.at[slot])
                for w, f in ((wg_hbm, wg_f), (wu_hbm, wu_f), (wd_hbm, wd_f))]

    def scatter_wait():
        pltpu.make_async_copy(ybuf, ys_ref.at[pl.ds(0, rows)], ssem).wait()

    @pl.when(b < nu)
    def _():
        slot = lax.rem(b, 2)

        @pl.when(b == 0)
        def _():
            gather_start(src_ref, 0)

        @pl.when(b + 1 < nu)
        def _():
            gather_start(srcn_ref, 1 - slot)

        e = be_ref[b]
        p = par_ref[b]
        fresh = (b == 0) | (e != be_ref[jnp.maximum(b - 1, 0)])

        @pl.when(fresh)
        def _():
            @pl.when(b == 0)
            def _():
                for c in weight_copies(e, p):
                    c.start()

            for c in weight_copies(e, p):
                c.wait()
            nxt = nxt_ref[b]

            @pl.when(nxt >= 0)
            def _():
                for c in weight_copies(nxt, 1 - p):
                    c.start()

            wg_s[...] = wg_f[p].astype(BF)
            wu_s[...] = wu_f[p].astype(BF)
            wd_s[...] = wd_f[p].astype(BF)

        pltpu.make_async_copy(h_ref.at[pl.ds(0, rows)], xbuf.at[slot], gsem.at[slot]).wait()
        x = xbuf[slot].astype(BF)
        a = jnp.dot(x, wg_s[...], preferred_element_type=F32)
        u = jnp.dot(x, wu_s[...], preferred_element_type=F32)
        hmid = (a * jax.nn.sigmoid(a) * u).astype(BF)

        @pl.when(b >= 1)
        def _():
            scatter_wait()

        ybuf[...] = jnp.dot(hmid, wd_s[...], preferred_element_type=F32)

        def scatter_body(t, carry):
            pltpu.make_async_copy(ybuf.at[pl.ds(t, 1)], ys_ref.at[pl.ds(dst_ref[0, 0, t], 1)], ssem).start()
            return carry
        lax.fori_loop(0, rows, scatter_body, 0, unroll=8)

        @pl.when(b == nu - 1)
        def _():
            scatter_wait()
            ybuf[...] = jnp.zeros(ybuf.shape, F32)
            for c in range(TRASH_ROWS // rows):
                cp = pltpu.make_async_copy(ybuf, ys_ref.at[pl.ds(n_slots + c * rows, rows)], ssem)
                cp.start()
                cp.wait()


def _experts(blk_e, n_used, par, nxt, row_src3, row_dst3, h, w_gate, w_up, w_down, layer, n_slots):
    nblk = row_src3.shape[0]
    tab_spec = lambda f: pl.BlockSpec((1, 1, MOE_BLOCK), f, memory_space=pltpu.SMEM)
    any_spec = pl.BlockSpec(memory_space=pl.ANY)
    grid_spec = pltpu.PrefetchScalarGridSpec(
        num_scalar_prefetch=4,
        grid=(nblk,),
        in_specs=[
            tab_spec(lambda b, *_: (b, 0, 0)),
            tab_spec(lambda b, *_: (jnp.minimum(b + 1, nblk - 1), 0, 0)),
            tab_spec(lambda b, *_: (b, 0, 0)),
            any_spec, any_spec, any_spec, any_spec,
        ],
        out_specs=any_spec,
        scratch_shapes=[
            pltpu.VMEM((2, MOE_BLOCK, D_MODEL), F32), pltpu.VMEM((MOE_BLOCK, D_MODEL), F32),
            pltpu.VMEM((2, D_MODEL, D_EXPERT), F32), pltpu.VMEM((2, D_MODEL, D_EXPERT), F32),
            pltpu.VMEM((2, D_EXPERT, D_MODEL), F32),
            pltpu.VMEM((D_MODEL, D_EXPERT), BF), pltpu.VMEM((D_MODEL, D_EXPERT), BF),
            pltpu.VMEM((D_EXPERT, D_MODEL), BF),
            pltpu.SemaphoreType.DMA((2,)), pltpu.SemaphoreType.DMA(()), pltpu.SemaphoreType.DMA((2,)),
        ],
    )
    return pl.pallas_call(
        functools.partial(_expert_kernel, layer=layer, n_slots=n_slots),
        grid_spec=grid_spec,
        out_shape=jax.ShapeDtypeStruct((n_slots + TRASH_ROWS, D_MODEL), F32),
        compiler_params=_params(("arbitrary",)),
        name="moe_experts",
    )(blk_e, n_used, par, nxt, row_src3, row_src3, row_dst3, h, w_gate, w_up, w_down)


def _combine_kernel(gate_ref, xp_ref, xs_ref, y0_ref, y1_ref, op_ref, os_ref, *, nbp):
    i = pl.program_id(0)
    y = y0_ref[...] * gate_ref[:, 0:1] + y1_ref[...] * gate_ref[:, 1:2]

    @pl.when(i < nbp)
    def _():
        op_ref[...] = xp_ref[...] + y

    @pl.when(i == nbp)
    def _():
        os_ref[...] = xs_ref[...] + y


def _combine(gate, xp, xs, ys):
    nbp = xp.shape[0] // MOE_BLOCK
    nb_tok = nbp + 1
    p_spec = pl.BlockSpec((MOE_BLOCK, D_MODEL), lambda i: (jnp.minimum(i, nbp - 1), 0))
    s_spec = pl.BlockSpec((MOE_BLOCK, D_MODEL), lambda i: (0, 0))
    return pl.pallas_call(
        functools.partial(_combine_kernel, nbp=nbp),
        grid=(nb_tok,),
        in_specs=[
            pl.BlockSpec((MOE_BLOCK, LANES), lambda i: (i, 0)),
            p_spec, s_spec,
            pl.BlockSpec((MOE_BLOCK, D_MODEL), lambda i: (i, 0)),
            pl.BlockSpec((MOE_BLOCK, D_MODEL), lambda i: (nb_tok + i, 0)),
        ],
        out_specs=[p_spec, s_spec],
        out_shape=[jax.ShapeDtypeStruct(xp.shape, F32), jax.ShapeDtypeStruct(xs.shape, F32)],
        compiler_params=_params(("arbitrary",)),
        name="moe_combine",
    )(gate, xp, xs, ys, ys)


def _moe(xp, xs, g_ffn_l, w_router, w_gate, w_up, w_down, layer):
    mt = xp.shape[0] + xs.shape[0]
    n_slots = 2 * mt
    nblk = (n_slots + N_EXPERTS * (MOE_BLOCK - 1) + MOE_BLOCK - 1) // MOE_BLOCK
    h, gate, info, cnt = _router(xp, xs, g_ffn_l, w_router)

    ids = jnp.arange(N_EXPERTS, dtype=jnp.int32)
    counts = cnt[0, N_GROUPS:N_GROUPS + N_EXPERTS].astype(jnp.int32)
    used = counts > 0
    padded = (counts + MOE_BLOCK - 1) // MOE_BLOCK * MOE_BLOCK
    pad_ends = jnp.cumsum(padded)
    pad_starts = pad_ends - padded
    n_used = (pad_ends[-1] // MOE_BLOCK).astype(jnp.int32)
    blk_ids = jnp.arange(nblk, dtype=jnp.int32)
    blk_e = jnp.sum((pad_ends[None, :] <= blk_ids[:, None] * MOE_BLOCK).astype(jnp.int32), axis=1)
    last_e = jnp.max(jnp.where(used, ids, 0))
    blk_e = jnp.where(blk_ids < n_used, jnp.minimum(blk_e, N_EXPERTS - 1), last_e).astype(jnp.int32)
    ordinal = jnp.cumsum(used.astype(jnp.int32)) - 1
    next_e = jnp.min(jnp.where((ids[None, :] > ids[:, None]) & used[None, :], ids[None, :], N_EXPERTS), axis=1)
    next_e = jnp.where(next_e == N_EXPERTS, -1, next_e)
    blk_onehot = blk_e[:, None] == ids[None, :]
    par = jnp.sum(jnp.where(blk_onehot, (ordinal % 2)[None, :], 0), axis=1).astype(jnp.int32)
    nxt = jnp.sum(jnp.where(blk_onehot, next_e[None, :], 0), axis=1).astype(jnp.int32)

    e2, rank2 = info[:, 0:2], info[:, 2:4]
    start2 = jnp.sum(jnp.where(e2[:, :, None] == ids[None, None, :], pad_starts[None, None, :], 0), axis=-1)
    dest = start2 + rank2
    slot_code = jnp.arange(mt, dtype=jnp.int32)[:, None] + jnp.array([[0, mt]], jnp.int32)
    rows = jnp.arange(nblk * MOE_BLOCK, dtype=jnp.int32)
    row_dst = (n_slots + rows % TRASH_ROWS).at[dest.reshape(-1)].set(slot_code.reshape(-1))
    row_src = jnp.where(row_dst < n_slots, row_dst % mt, 0)
    shape3 = (nblk, 1, MOE_BLOCK)

    ys = _experts(blk_e, n_used.reshape(1), par, nxt, row_src.reshape(shape3), row_dst.reshape(shape3),
                  h, w_gate, w_up, w_down, layer, n_slots)
    return _combine(gate, xp, xs, ys)


def _rope_tables(pos):
    half = ROPE_DIM // 2
    inv = ROPE_THETA ** (-jnp.arange(half, dtype=F32) / half)
    ang = pos.astype(F32)[:, None] * inv[None, :]
    cos, sin = jnp.cos(ang), jnp.sin(ang)
    z = jnp.zeros_like(cos)
    a = jnp.concatenate([cos, cos, z, z], axis=1)
    b = jnp.concatenate([z, sin, z, z], axis=1)
    c = jnp.concatenate([-sin, z, z, z], axis=1)
    return a, b, c


def _out_spec(tm, tn):
    return pl.BlockSpec((tm, tn), lambda i, j: (i, j))


def _table_specs(tabs, tm, nper):
    return [(t, pl.BlockSpec((tm, LANES), lambda i, j, nper=nper: (i % nper, 0))) for t in tabs]


def _vec_extra(v):
    n = v.shape[-1]
    return (v.reshape(1, n), pl.BlockSpec((1, n), lambda i, j: (0, 0)))


def kernel(x_prompt, x_sample, cache_latent, cache_krope, cache_kscale, cache_mem_k, cache_mem_v, state_conv, page_table, mem_prompt, g_mix, g_ffn, g_mem, w_mem_kv, g_mq, g_mk, w_in_a, conv_w, conv_b, ln_g, ln_b, w_out_a, g_kv_in, w_kv_down, g_kv_lat, w_uk, w_uv, g_k, w_in_b, g_qlat, w_q_up, g_q, w_out_b, w_rg, w_re, w_gate, w_up, w_down):
    batch, seq, _ = x_prompt.shape
    nb = x_sample.shape[0]
    depth = g_mix.shape[0]
    n_a = w_in_a.shape[0]
    mp = batch * seq
    tmp = 1024
    xp = x_prompt.reshape(mp, D_MODEL)
    xs = x_sample.reshape(nb, D_MODEL)

    tabs_p = _rope_tables(jnp.arange(seq))
    tabs_s = _rope_tables(jnp.full((nb,), PAST_LEN))
    w_kvd = jnp.pad(w_kv_down, ((0, 0), (0, LANES - ROPE_DIM)))
    gk_n = g_k[:NOPE_DIM]
    gk_r = jnp.pad(g_k[NOPE_DIM:], (0, LANES - ROPE_DIM))
    w_uk2 = w_uk.reshape(KV_LORA, MLA_HEADS * NOPE_DIM)
    w_uk_t = jnp.transpose(w_uk, (1, 2, 0))
    w_uv2 = w_uv.reshape(KV_LORA, MLA_HEADS * V_DIM)
    w_qu = jnp.pad(w_q_up, ((0, 0), (0, 0), (0, 0), (0, HEAD_PAD - QK_DIM)))
    w_qu = w_qu.reshape(w_q_up.shape[0], Q_LORA, MLA_HEADS * HEAD_PAD)
    gq_pad = jnp.pad(g_q, ((0, 0), (0, HEAD_PAD - QK_DIM)))
    w_router = jnp.pad(jnp.concatenate([w_rg, w_re], axis=-1),
                       ((0, 0), (0, 0), (0, LANES - N_GROUPS - N_EXPERTS)))
    conv_w32 = jnp.pad(conv_w, ((0, 0), (0, 32 - CONV_W), (0, 0)))
    cache_kscale_t = jnp.transpose(cache_kscale, (0, 2, 1))

    memk, memv = _memory_kv_all(mem_prompt.reshape(batch * N_MEM, D_MODEL), g_mem, w_mem_kv, g_mk)

    conv_p_list, conv_s_list = [], []
    shared = None
    for l in range(depth):
        if l < n_a:
            n_glu = D_MODEL
            outs_p, outs_s = [], []
            for x, m, tm, dst in ((xp, mp, tmp, outs_p), (xs, nb, nb, outs_s)):
                tn = 256
                a = _fused_mm(
                    "in_a_glu", [(x, D_MODEL, 0)],
                    [(0, w_in_a, l, D_MODEL, tn, 0, 0), (0, w_in_a, l, D_MODEL, tn, 0, n_glu // tn)],
                    m=m, tm=tm, nj=n_glu // tn, gain=g_mix[l],
                    outs=[(jax.ShapeDtypeStruct((m, n_glu), F32), _out_spec(tm, tn))],
                    epi=_epi_glu)[0]
                qm = _fused_mm(
                    "in_a_memq", [(x, D_MODEL, 0)],
                    [(0, w_in_a, l, D_MODEL, tn, 0, 2 * n_glu // tn)],
                    m=m, tm=tm, nj=MEM_W // tn, gain=g_mix[l],
                    outs=[(jax.ShapeDtypeStruct((m, MEM_W), F32), _out_spec(tm, tn))],
                    epi=_epi_plain)[0]
                dst.extend([a, qm])
            a_p, qm_p = outs_p
            a_s, qm_s = outs_s
            yc_p = _conv_prompt(a_p, conv_w32[l], conv_b[l], ln_g[l], ln_b[l], batch)
            conv_p_list.append(a_p.reshape(batch, seq, D_MODEL)[:, seq - (CONV_W - 1):])
            yc_s, ns = _conv_sample(state_conv[l], a_s, conv_w32[l], conv_b[l], ln_g[l], ln_b[l])
            conv_s_list.append(ns)
            ym_p = _memattn_prompt(qm_p, 0, memk, memv, l, g_mq[l], batch)
            ym_s = _memattn_sample(qm_s.reshape(nb, MEM_HEADS, MEM_HEAD_DIM), cache_mem_k, cache_mem_v,
                                   l, g_mq[l])
            new = []
            for x, m, tm, y1, y2 in ((xp, mp, tmp, yc_p, ym_p), (xs, nb, nb, yc_s, ym_s)):
                tn = 512
                new.append(_fused_mm(
                    "out_a", [(y1, D_MODEL, 0), (y2, MEM_W, 0)],
                    [(0, w_out_a, l, D_MODEL, tn, 0, 0), (1, w_out_a, l, MEM_W, tn, D_MODEL // MEM_W, 0)],
                    m=m, tm=tm, nj=D_MODEL // tn,
                    extras=[(x, _out_spec(tm, tn))],
                    outs=[(jax.ShapeDtypeStruct((m, D_MODEL), F32), _out_spec(tm, tn))],
                    epi=_epi_residual)[0])
            xp, xs = new
        else:
            jb = l - n_a
            kfull, vfull, c_s, kr_s, ks_s = shared
            att = []
            for x, m, tm, tabs, nper, qdt in ((xp, mp, tmp, tabs_p, seq // tmp, BF),
                                              (xs, nb, nb, tabs_s, 1, F32)):
                tn = Q_LORA
                n_in = Q_LORA + MEM_W
                u, qm = _fused_mm(
                    "in_b", [(x, D_MODEL, 0)], [(0, w_in_b, jb, D_MODEL, tn, 0, 0)],
                    m=m, tm=tm, nj=n_in // tn, gain=g_mix[l],
                    outs=[(jax.ShapeDtypeStruct((m, Q_LORA), F32),
                           pl.BlockSpec((tm, tn), lambda i, j: (i, 0))),
                          (jax.ShapeDtypeStruct((m, MEM_W), F32),
                           pl.BlockSpec((tm, tn), lambda i, j: (i, jnp.maximum(j - 1, 0))))],
                    epi=_epi_split)
                qh = _fused_mm(
                    "q_up", [(u, Q_LORA, 0)], [(0, w_qu, jb, Q_LORA, HEAD_PAD, 0, 0)],
                    m=m, tm=tm, nj=MLA_HEADS, gain=g_qlat[jb],
                    extras=[_vec_extra(gq_pad[jb])] + _table_specs(tabs, tm, nper),
                    outs=[(jax.ShapeDtypeStruct((m, MLA_HEADS * HEAD_PAD), qdt), _out_spec(tm, HEAD_PAD))],
                    epi=_epi_qhead)[0]
                att.append((qm, qh))
            (qm_p, qh_p), (qm_s, qh_s) = att
            att_p = _mla_prompt_attention(qh_p, kfull, vfull, batch)
            ym_p = _memattn_prompt(qm_p, 0, memk, memv, l, g_mq[l], batch)
            ym_s = _memattn_sample(qm_s.reshape(nb, MEM_HEADS, MEM_HEAD_DIM), cache_mem_k, cache_mem_v,
                                   l, g_mq[l])

            qlat = _headwise_mm(
                "q_latent", qh_s, pl.BlockSpec((nb, NOPE_DIM), lambda h: (0, 2 * h)),
                w_uk_t, pl.BlockSpec((None, NOPE_DIM, KV_LORA), lambda h: (h, 0, 0)),
                gk_n.reshape(1, NOPE_DIM),
                jax.ShapeDtypeStruct((nb, MLA_HEADS * KV_LORA), BF),
                pl.BlockSpec((nb, KV_LORA), lambda h: (0, h)))
            o_lat = _mla_sample_attention(
                page_table, qlat.reshape(nb, MLA_HEADS, KV_LORA),
                qh_s.reshape(nb, MLA_HEADS, HEAD_PAD), cache_latent, cache_krope, cache_kscale_t,
                c_s.reshape(nb, 1, KV_LORA), kr_s.reshape(nb, 1, LANES),
                ks_s[:, :MLA_HEADS].reshape(nb, MLA_HEADS, 1))
            att_s = _headwise_mm(
                "v_expand", o_lat.reshape(nb, MLA_HEADS * KV_LORA),
                pl.BlockSpec((nb, KV_LORA), lambda h: (0, h)),
                w_uv2, pl.BlockSpec((KV_LORA, V_DIM), lambda h: (0, h)),
                jnp.ones((1, KV_LORA), F32),
                jax.ShapeDtypeStruct((nb, MLA_HEADS * V_DIM), BF),
                pl.BlockSpec((nb, V_DIM), lambda h: (0, h)))
            new = []
            n_att = MLA_HEADS * V_DIM
            for x, m, tm, y1, y2 in ((xp, mp, tmp, att_p, ym_p), (xs, nb, nb, att_s, ym_s)):
                tn = 512
                new.append(_fused_mm(
                    "out_b", [(y1, n_att, 0), (y2, MEM_W, 0)],
                    [(0, w_out_b, jb, n_att, tn, 0, 0), (1, w_out_b, jb, MEM_W, tn, n_att // MEM_W, 0)],
                    m=m, tm=tm, nj=D_MODEL // tn,
                    extras=[(x, _out_spec(tm, tn))],
                    outs=[(jax.ShapeDtypeStruct((m, D_MODEL), F32), _out_spec(tm, tn))],
                    epi=_epi_residual)[0])
            xp, xs = new

        xp, xs = _moe(xp, xs, g_ffn[l], w_router[l], w_gate, w_up, w_down, l)

        if l == n_a - 1:
            lat = []
            for x, m, tm, tabs, nper in ((xp, mp, tmp, tabs_p, seq // tmp), (xs, nb, nb, tabs_s, 1)):
                nck = KV_LORA + LANES
                c, kr128, ssq128 = _fused_mm(
                    "kv_latent", [(x, D_MODEL, 0)], [(0, w_kvd, None, D_MODEL, nck, 0, 0)],
                    m=m, tm=tm, nj=1, gain=g_kv_in,
                    extras=[_vec_extra(g_kv_lat), _vec_extra(gk_r)] + _table_specs(tabs, tm, nper),
                    outs=[(jax.ShapeDtypeStruct((m, KV_LORA), F32), _out_spec(tm, KV_LORA)),
                          (jax.ShapeDtypeStruct((m, LANES), F32), pl.BlockSpec((tm, LANES), lambda i, j: (i, 0))),
                          (jax.ShapeDtypeStruct((m, LANES), F32), pl.BlockSpec((tm, LANES), lambda i, j: (i, 0)))],
                    epi=_epi_latent)
                kf, ks128 = _fused_mm(
                    "k_nope", [(c, KV_LORA, 0)], [(0, w_uk2, None, KV_LORA, 2 * NOPE_DIM, 0, 0)],
                    m=m, tm=tm, nj=MLA_HEADS // 2,
                    extras=[(kr128, pl.BlockSpec((tm, LANES), lambda i, j: (i, 0))),
                            (ssq128, pl.BlockSpec((tm, LANES), lambda i, j: (i, 0))),
                            _vec_extra(gk_n)],
                    outs=[(jax.ShapeDtypeStruct((m, MLA_HEADS * HEAD_PAD), BF), _out_spec(tm, 2 * HEAD_PAD)),
                          (jax.ShapeDtypeStruct((m, LANES), F32), pl.BlockSpec((tm, LANES), lambda i, j: (i, 0)))],
                    epi=_epi_knope)
                lat.append((c, kr128, ks128, kf))
            (c_p, kr_p, ks_p, kfull), (c_s, kr_s, ks_s, _) = lat
            vfull = _fused_mm(
                "v_full", [(c_p, KV_LORA, 0)], [(0, w_uv2, None, KV_LORA, 512, 0, 0)],
                m=mp, tm=tmp, nj=MLA_HEADS * V_DIM // 512,
                outs=[(jax.ShapeDtypeStruct((mp, MLA_HEADS * V_DIM), BF), _out_spec(tmp, 512))],
                epi=_epi_plain)[0]
            shared = (kfull, vfull, c_s, kr_s, ks_s)

    shape5 = (depth, batch, N_MEM, MEM_HEADS, MEM_HEAD_DIM)
    return (xp.reshape(batch, seq, D_MODEL), xs.reshape(nb, 1, D_MODEL),
            c_p.reshape(batch, seq, KV_LORA), kr_p[:, :ROPE_DIM].reshape(batch, seq, ROPE_DIM),
            ks_p[:, :MLA_HEADS].reshape(batch, seq, MLA_HEADS),
            memk.reshape(shape5), memv.reshape(shape5),
            jnp.stack(conv_p_list, axis=0),
            c_s.reshape(nb, 1, KV_LORA), kr_s[:, :ROPE_DIM].reshape(nb, 1, ROPE_DIM),
            ks_s[:, :MLA_HEADS].reshape(nb, 1, MLA_HEADS),
            jnp.stack(conv_s_list, axis=0))
```

```python
import functools
import math

import jax
import jax.numpy as jnp
from jax import lax
from jax.experimental import pallas as pl
from jax.experimental.pallas import tpu as pltpu

F32 = jnp.float32
BF = jnp.bfloat16

D_MODEL = 2048
SEQ = 2048
PAST_LEN = 8192
PAGE_SIZE = 128
CONV_W = 31
N_MEM = 256
MEM_HEADS = 4
MEM_HEAD_DIM = 256
MEM_W = MEM_HEADS * MEM_HEAD_DIM
MLA_HEADS = 16
Q_LORA = 512
KV_LORA = 512
NOPE_DIM = 128
ROPE_DIM = 64
QK_DIM = NOPE_DIM + ROPE_DIM
V_DIM = 128
ROPE_THETA = 10000.0
N_GROUPS = 8
EXPERTS_PER_GROUP = 8
N_EXPERTS = 64
D_EXPERT = 512
MOE_BLOCK = 128
EPS = 1e-6

HEAD_PAD = 256
LANES = 128
VMEM_LIMIT_BYTES = 56 * 1024 * 1024
NEG_BIG = -1e30
NT_DIMS = (((1,), (1,)), ((), ()))


def _params(sem):
    return pltpu.CompilerParams(dimension_semantics=sem, vmem_limit_bytes=VMEM_LIMIT_BYTES)


def _mm_kernel(*refs, nx, w_x, has_gain, n_extra, n_out, epi, tm):
    x_refs = refs[:nx]
    w_refs = refs[nx:nx + len(w_x)]
    pos = nx + len(w_x)
    g_ref = refs[pos] if has_gain else None
    pos += int(has_gain)
    extra_refs = refs[pos:pos + n_extra]
    pos += n_extra
    out_refs = refs[pos:pos + n_out]
    pos += n_out
    h_s = refs[pos] if has_gain else None
    j = pl.program_id(1)

    if has_gain:
        chunk = min(tm, 128)

        @pl.when(j == 0)
        def _():
            def body(c, carry):
                r = pl.multiple_of(c * chunk, chunk)
                xf = x_refs[0][pl.ds(r, chunk), :].astype(F32)
                ms = jnp.mean(xf * xf, axis=-1, keepdims=True)
                h_s[pl.ds(r, chunk), :] = (xf * lax.rsqrt(ms + EPS) * g_ref[...]).astype(BF)
                return carry
            lax.fori_loop(0, tm // chunk, body, 0)

    def product(wi):
        xi = w_x[wi]
        lhs = h_s[...] if has_gain and xi == 0 else x_refs[xi][...].astype(BF)
        return jnp.dot(lhs, w_refs[wi][...].astype(BF), preferred_element_type=F32)

    epi(_Products(product, len(w_x)), extra_refs, out_refs, j)


class _Products:
    def __init__(self, fn, n):
        self._fn, self._n = fn, n

    def __getitem__(self, i):
        if isinstance(i, slice):
            return [self._fn(k) for k in range(*i.indices(self._n))]
        return self._fn(i)


def _fused_mm(name, xs, ws, *, m, tm, nj, gain=None, extras=(), outs, epi):
    in_arrays, in_specs = [], []
    for arr, k, cb in xs:
        in_arrays.append(arr)
        in_specs.append(pl.BlockSpec((tm, k), lambda i, j, cb=cb: (i, cb)))
    for xi, arr, layer, k, tn, rb, cb0 in ws:
        in_arrays.append(arr)
        col = cb0 if callable(cb0) else (lambda j, cb0=cb0: cb0 + j)
        if layer is None:
            in_specs.append(pl.BlockSpec((k, tn), lambda i, j, rb=rb, col=col: (rb, col(j))))
        else:
            in_specs.append(pl.BlockSpec((None, k, tn),
                                         lambda i, j, l=layer, rb=rb, col=col: (l, rb, col(j))))
    scratch = []
    if gain is not None:
        k0 = xs[0][1]
        in_arrays.append(gain.reshape(1, k0).astype(F32))
        in_specs.append(pl.BlockSpec((1, k0), lambda i, j: (0, 0)))
        scratch.append(pltpu.VMEM((tm, k0), BF))
    for arr, spec in extras:
        in_arrays.append(arr)
        in_specs.append(spec)
    kern = functools.partial(_mm_kernel, nx=len(xs), w_x=tuple(w[0] for w in ws),
                             has_gain=gain is not None, n_extra=len(extras),
                             n_out=len(outs), epi=epi, tm=tm)
    res = pl.pallas_call(
        kern,
        grid=(m // tm, nj),
        in_specs=in_specs,
        out_specs=[o[1] for o in outs],
        out_shape=[o[0] for o in outs],
        scratch_shapes=scratch,
        compiler_params=_params(("arbitrary", "arbitrary")),
        name=name,
    )(*in_arrays)
    return res


def _epi_plain(ds, ex, outs, j):
    acc = ds[0]
    for d in ds[1:]:
        acc = acc + d
    outs[0][...] = acc.astype(outs[0].dtype)


def _epi_residual(ds, ex, outs, j):
    acc = ds[0]
    for d in ds[1:]:
        acc = acc + d
    outs[0][...] = (ex[0][...] + acc).astype(outs[0].dtype)


def _epi_split(ds, ex, outs, j):
    @pl.when(j == 0)
    def _():
        outs[0][...] = ds[0]

    @pl.when(j > 0)
    def _():
        outs[1][...] = ds[0]


def _epi_glu_memq(ds, ex, outs, j, *, ng):
    @pl.when(j < ng)
    def _():
        outs[0][...] = (ds[0] * jax.nn.sigmoid(ds[1])).astype(outs[0].dtype)

    @pl.when(j >= ng)
    def _():
        outs[1][...] = ds[2].astype(outs[1].dtype)


def _rope_tile(t, a_ref, b_ref, c_ref):
    return (t * a_ref[...] + pltpu.roll(t, 32, 1) * b_ref[...]
            + pltpu.roll(t, 96, 1) * c_ref[...])


def _epi_qhead(ds, ex, outs, j):
    acc = ds[0]
    gq_ref, a_ref, b_ref, c_ref = ex
    ssq = jnp.sum(acc * acc, axis=-1, keepdims=True)
    qn = acc * lax.rsqrt(ssq * (1.0 / QK_DIM) + EPS) * gq_ref[...]
    outs[0][:, :LANES] = qn[:, :LANES].astype(outs[0].dtype)
    outs[0][:, LANES:] = _rope_tile(qn[:, LANES:], a_ref, b_ref, c_ref).astype(outs[0].dtype)


def _epi_latent(ds, ex, outs, j):
    ck = ds[0]
    glat_ref, gkr_ref, a_ref, b_ref, c_ref = ex
    c_raw = ck[:, :KV_LORA]
    ms = jnp.mean(c_raw * c_raw, axis=-1, keepdims=True)
    outs[0][...] = c_raw * lax.rsqrt(ms + EPS) * glat_ref[...]
    krt = ck[:, KV_LORA:]
    ssq = jnp.sum(krt * krt, axis=-1, keepdims=True)
    outs[1][...] = _rope_tile(krt * gkr_ref[...], a_ref, b_ref, c_ref)
    outs[2][...] = jnp.broadcast_to(ssq, outs[2].shape)


def _epi_knope(ds, ex, outs, j):
    acc = ds[0]
    kr_ref, ssq_ref, gkn_ref = ex
    ssq_r = ssq_ref[:, 0:1]
    kr = kr_ref[...]
    lane = lax.broadcasted_iota(jnp.int32, outs[1].shape, 1)

    @pl.when(j == 0)
    def _():
        outs[1][...] = jnp.zeros(outs[1].shape, F32)

    ks_all = outs[1][...]
    for t in range(2):
        kn = acc[:, t * NOPE_DIM:(t + 1) * NOPE_DIM]
        ssq = jnp.sum(kn * kn, axis=-1, keepdims=True) + ssq_r
        ks = lax.rsqrt(ssq * (1.0 / QK_DIM) + EPS)
        outs[0][:, t * HEAD_PAD:t * HEAD_PAD + LANES] = (kn * gkn_ref[...] * ks).astype(outs[0].dtype)
        outs[0][:, t * HEAD_PAD + LANES:(t + 1) * HEAD_PAD] = (kr * ks).astype(outs[0].dtype)
        ks_all = jnp.where(lane == 2 * j + t, ks, ks_all)
    outs[1][...] = ks_all


def _memkv_kernel(x_ref, g_ref, wk_ref, wv_ref, gk_ref, k_out, v_out, h_s):
    j = pl.program_id(1)
    rows = x_ref.shape[0]
    chunk = 128

    @pl.when(j == 0)
    def _():
        def body(c, carry):
            r = pl.multiple_of(c * chunk, chunk)
            xf = x_ref[pl.ds(r, chunk), :]
            ms = jnp.mean(xf * xf, axis=-1, keepdims=True)
            h_s[pl.ds(r, chunk), :] = (xf * lax.rsqrt(ms + EPS) * g_ref[...]).astype(BF)
            return carry
        lax.fori_loop(0, rows // chunk, body, 0)

    h = h_s[...]
    k = jnp.dot(h, wk_ref[...].astype(BF), preferred_element_type=F32)
    ms = jnp.mean(k * k, axis=-1, keepdims=True)
    k_out[...] = k * lax.rsqrt(ms + EPS) * gk_ref[...]
    v_out[...] = jnp.dot(h, wv_ref[...].astype(BF), preferred_element_type=F32)


def _memory_kv_all(mem2d, g_mem, w_mem_kv, g_mk):
    depth = w_mem_kv.shape[0]
    rows = mem2d.shape[0]
    hd = MEM_HEAD_DIM
    out = pl.pallas_call(
        _memkv_kernel,
        grid=(depth, MEM_HEADS),
        in_specs=[
            pl.BlockSpec((rows, D_MODEL), lambda l, j: (0, 0)),
            pl.BlockSpec((None, 1, D_MODEL), lambda l, j: (l, 0, 0)),
            pl.BlockSpec((None, D_MODEL, hd), lambda l, j: (l, 0, j)),
            pl.BlockSpec((None, D_MODEL, hd), lambda l, j: (l, 0, MEM_HEADS + j)),
            pl.BlockSpec((None, 1, hd), lambda l, j: (l, 0, 0)),
        ],
        out_specs=[
            pl.BlockSpec((None, rows, hd), lambda l, j: (l, 0, j)),
            pl.BlockSpec((None, rows, hd), lambda l, j: (l, 0, j)),
        ],
        out_shape=[jax.ShapeDtypeStruct((depth, rows, MEM_W), F32)] * 2,
        scratch_shapes=[pltpu.VMEM((rows, D_MODEL), BF)],
        compiler_params=_params(("arbitrary", "arbitrary")),
        name="memory_kv",
    )(mem2d, g_mem.reshape(depth, 1, D_MODEL), w_mem_kv, w_mem_kv, g_mk.reshape(depth, 1, hd))
    return out


CONV_TT = 256
CONV_RC = 32
CONV_LW = 256
CONV_NORM_RC = 64
CONV_HALO = 32
SUBLANES = 8


def _conv_prompt_kernel(a_ref, w_ref, cb_ref, lg_ref, lb_ref, o_ref, sh, ybuf):
    t = pl.program_id(1)
    tt = CONV_TT
    span = tt + CONV_HALO

    @pl.when(t == 0)
    def _():
        sh[0, 0:CONV_HALO, :] = jnp.zeros((CONV_HALO, D_MODEL), F32)
        sh[0, span:span + SUBLANES, :] = jnp.zeros((SUBLANES, D_MODEL), F32)

    sh[0, CONV_HALO:span, :] = a_ref[...]
    for p in range(1, SUBLANES):
        sh[p, 0:span, :] = sh[0, p:p + span, :]

    groups = CONV_RC // SUBLANES

    def conv_body(c, carry):
        r = c * CONV_RC
        for lq in range(D_MODEL // CONV_LW):
            ls = slice(lq * CONV_LW, (lq + 1) * CONV_LW)
            accs = [jnp.zeros((SUBLANES, CONV_LW), F32) for _ in range(groups)]
            for p in range(SUBLANES):
                slabs = {}
                for m in range(CONV_W // SUBLANES + 2):
                    k = p - 2 + SUBLANES * m
                    if not 0 <= k < CONV_W:
                        continue
                    w = w_ref[k * SUBLANES:(k + 1) * SUBLANES, ls]
                    for g in range(groups):
                        j = m + g
                        if j not in slabs:
                            start = pl.multiple_of(r + SUBLANES * j, SUBLANES)
                            slabs[j] = sh[p, pl.ds(start, SUBLANES), ls]
                        accs[g] = accs[g] + slabs[j] * w
            ybuf[pl.ds(pl.multiple_of(r, CONV_RC), CONV_RC), ls] = jnp.concatenate(accs, axis=0)
        return carry

    lax.fori_loop(0, tt // CONV_RC, conv_body, 0)

    def norm_body(c, carry):
        rows = pl.ds(pl.multiple_of(c * CONV_NORM_RC, CONV_NORM_RC), CONV_NORM_RC)
        y = ybuf[rows, :] + cb_ref[...]
        mu = jnp.mean(y, axis=-1, keepdims=True)
        yc = y - mu
        var = jnp.mean(yc * yc, axis=-1, keepdims=True)
        z = yc * lax.rsqrt(var + EPS) * lg_ref[...] + lb_ref[...]
        o_ref[rows, :] = (z * jax.nn.sigmoid(z)).astype(o_ref.dtype)
        return carry

    lax.fori_loop(0, tt // CONV_NORM_RC, norm_body, 0)
    sh[0, 0:CONV_HALO, :] = sh[0, tt:span, :]


def _conv_prompt(a, w32, cb, lg, lb, batch):
    m = a.shape[0]
    nt = SEQ // CONV_TT
    vec = lambda v: v.reshape(1, D_MODEL)
    w_rep = jnp.repeat(w32, SUBLANES, axis=0)
    return pl.pallas_call(
        _conv_prompt_kernel,
        grid=(batch, nt),
        in_specs=[
            pl.BlockSpec((CONV_TT, D_MODEL), lambda b, t: (b * nt + t, 0)),
            pl.BlockSpec((32 * SUBLANES, D_MODEL), lambda b, t: (0, 0)),
            pl.BlockSpec((1, D_MODEL), lambda b, t: (0, 0)),
            pl.BlockSpec((1, D_MODEL), lambda b, t: (0, 0)),
            pl.BlockSpec((1, D_MODEL), lambda b, t: (0, 0)),
        ],
        out_specs=pl.BlockSpec((CONV_TT, D_MODEL), lambda b, t: (b * nt + t, 0)),
        out_shape=jax.ShapeDtypeStruct((m, D_MODEL), BF),
        scratch_shapes=[pltpu.VMEM((SUBLANES, CONV_TT + CONV_HALO + SUBLANES, D_MODEL), F32),
                        pltpu.VMEM((CONV_TT, D_MODEL), F32)],
        compiler_params=_params(("arbitrary", "arbitrary")),
        name="conv_prompt",
    )(a, w_rep, vec(cb), vec(lg), vec(lb))


CONV_SB = 8


def _conv_sample_kernel(st_ref, a_ref, w_ref, cb_ref, lg_ref, lb_ref, y_ref, ns_ref):
    nstate = CONV_W - 1
    for s in range(CONV_SB):
        st = st_ref[s]
        a = a_ref[s]
        y = (jnp.sum(st * w_ref[0:nstate, :], axis=0, keepdims=True)
             + a * w_ref[nstate:nstate + 1, :] + cb_ref[...])
        mu = jnp.mean(y, axis=-1, keepdims=True)
        yc = y - mu
        var = jnp.mean(yc * yc, axis=-1, keepdims=True)
        z = yc * lax.rsqrt(var + EPS) * lg_ref[...] + lb_ref[...]
        y_ref[s] = (z * jax.nn.sigmoid(z)).astype(y_ref.dtype)
        ns_ref[s, 0:nstate - 1, :] = st_ref[s, 1:nstate, :]
        ns_ref[s, nstate - 1:nstate, :] = a


def _conv_sample(state, a, w32, cb, lg, lb):
    nb = state.shape[0]
    nstate = CONV_W - 1
    vec = lambda v: v.reshape(1, D_MODEL)
    y, ns = pl.pallas_call(
        _conv_sample_kernel,
        grid=(nb // CONV_SB,),
        in_specs=[
            pl.BlockSpec((CONV_SB, nstate, D_MODEL), lambda i: (i, 0, 0)),
            pl.BlockSpec((CONV_SB, 1, D_MODEL), lambda i: (i, 0, 0)),
            pl.BlockSpec((32, D_MODEL), lambda i: (0, 0)),
            pl.BlockSpec((1, D_MODEL), lambda i: (0, 0)),
            pl.BlockSpec((1, D_MODEL), lambda i: (0, 0)),
            pl.BlockSpec((1, D_MODEL), lambda i: (0, 0)),
        ],
        out_specs=[
            pl.BlockSpec((CONV_SB, 1, D_MODEL), lambda i: (i, 0, 0)),
            pl.BlockSpec((CONV_SB, nstate, D_MODEL), lambda i: (i, 0, 0)),
        ],
        out_shape=[jax.ShapeDtypeStruct((nb, 1, D_MODEL), BF),
                   jax.ShapeDtypeStruct((nb, nstate, D_MODEL), F32)],
        compiler_params=_params(("arbitrary",)),
        name="conv_sample",
    )(state, a.reshape(nb, 1, D_MODEL), w32, vec(cb), vec(lg), vec(lb))
    return y.reshape(nb, D_MODEL), ns


MEM_TQ = 512
MEM_SCALE = MEM_HEAD_DIM ** -0.5


def _memattn_prompt_kernel(q_ref, k_ref, v_ref, g_ref, o_ref):
    for h in range(MEM_HEADS):
        sl = slice(h * MEM_HEAD_DIM, (h + 1) * MEM_HEAD_DIM)
        qh = q_ref[:, sl]
        ms = jnp.mean(qh * qh, axis=-1, keepdims=True)
        qn = (qh * lax.rsqrt(ms + EPS) * g_ref[...]).astype(BF)
        s = lax.dot_general(qn, k_ref[:, sl].astype(BF), NT_DIMS,
                            preferred_element_type=F32) * MEM_SCALE
        mx = jnp.max(s, axis=-1, keepdims=True)
        p = jnp.exp(s - mx)
        l = jnp.sum(p, axis=-1, keepdims=True)
        o = jnp.dot(p.astype(BF), v_ref[:, sl].astype(BF), preferred_element_type=F32)
        o_ref[:, sl] = (o / l).astype(o_ref.dtype)


def _memattn_prompt(q, q_colblk, memk, memv, layer, g_mq_l, batch):
    m = q.shape[0]
    nq = SEQ // MEM_TQ
    return pl.pallas_call(
        _memattn_prompt_kernel,
        grid=(batch, nq),
        in_specs=[
            pl.BlockSpec((MEM_TQ, MEM_W), lambda b, i: (b * nq + i, q_colblk)),
            pl.BlockSpec((None, N_MEM, MEM_W), lambda b, i: (layer, b, 0)),
            pl.BlockSpec((None, N_MEM, MEM_W), lambda b, i: (layer, b, 0)),
            pl.BlockSpec((1, MEM_HEAD_DIM), lambda b, i: (0, 0)),
        ],
        out_specs=pl.BlockSpec((MEM_TQ, MEM_W), lambda b, i: (b * nq + i, 0)),
        out_shape=jax.ShapeDtypeStruct((m, MEM_W), BF),
        compiler_params=_params(("arbitrary", "arbitrary")),
        name="memattn_prompt",
    )(q, memk, memv, g_mq_l.reshape(1, MEM_HEAD_DIM))


MEM_SB = 4


def _memattn_sample_kernel(q_ref, k_ref, v_ref, g_ref, o_ref):
    for s in range(MEM_SB):
        q = q_ref[s]
        ms = jnp.mean(q * q, axis=-1, keepdims=True)
        qn = q * lax.rsqrt(ms + EPS) * g_ref[...]
        sc = jnp.sum(k_ref[s] * qn[None], axis=-1, keepdims=True) * MEM_SCALE
        mx = jnp.max(sc, axis=0, keepdims=True)
        p = jnp.exp(sc - mx)
        l = jnp.sum(p, axis=0)
        o = jnp.sum(p * v_ref[s], axis=0) / l
        o_ref[s] = o.astype(o_ref.dtype)


def _memattn_sample(q3, cmk, cmv, layer, g_mq_l):
    nb = q3.shape[0]
    kv_spec = pl.BlockSpec((None, MEM_SB, N_MEM, MEM_HEADS, MEM_HEAD_DIM), lambda i: (layer, i, 0, 0, 0))
    out = pl.pallas_call(
        _memattn_sample_kernel,
        grid=(nb // MEM_SB,),
        in_specs=[
            pl.BlockSpec((MEM_SB, MEM_HEADS, MEM_HEAD_DIM), lambda i: (i, 0, 0)),
            kv_spec, kv_spec,
            pl.BlockSpec((1, MEM_HEAD_DIM), lambda i: (0, 0)),
        ],
        out_specs=pl.BlockSpec((MEM_SB, MEM_HEADS, MEM_HEAD_DIM), lambda i: (i, 0, 0)),
        out_shape=jax.ShapeDtypeStruct((nb, MEM_HEADS, MEM_HEAD_DIM), BF),
        compiler_params=_params(("arbitrary",)),
        name="memattn_sample",
    )(q3, cmk, cmv, g_mq_l.reshape(1, MEM_HEAD_DIM))
    return out.reshape(nb, MEM_W)


FLASH_T = 512
FLASH_TK = 512
FLASH_HP = 2
MLA_SCALE = QK_DIM ** -0.5


def _flash_kernel(q_ref, k_ref, v_ref, o_ref):
    qi = pl.program_id(2)
    t, tk = FLASH_T, FLASH_TK
    ratio = t // tk
    row = lax.broadcasted_iota(jnp.int32, (t, tk), 0)
    col = lax.broadcasted_iota(jnp.int32, (t, tk), 1)

    def tile(ki, carry, hh, diag):
        m, l, acc = carry
        r = pl.multiple_of(ki * tk, tk)
        q = q_ref[:, hh * HEAD_PAD:(hh + 1) * HEAD_PAD]
        k = k_ref[pl.ds(r, tk), hh * HEAD_PAD:(hh + 1) * HEAD_PAD]
        s = lax.dot_general(q, k, NT_DIMS, preferred_element_type=F32) * MLA_SCALE
        if diag is not None:
            s = jnp.where(col + diag * tk <= row, s, NEG_BIG)
        m_new = jnp.maximum(m, jnp.max(s, axis=-1, keepdims=True))
        alpha = jnp.exp(m - m_new)
        p = jnp.exp(s - m_new)
        l = alpha * l + jnp.sum(p, axis=-1, keepdims=True)
        v = v_ref[pl.ds(r, tk), hh * V_DIM:(hh + 1) * V_DIM]
        acc = alpha * acc + jnp.dot(p.astype(BF), v, preferred_element_type=F32)
        return m_new, l, acc

    def tiles(ki, carries, diag):
        return tuple(tile(ki, carries[hh], hh, diag) for hh in range(FLASH_HP))

    init = tuple((jnp.full((t, 1), NEG_BIG, F32), jnp.zeros((t, 1), F32), jnp.zeros((t, V_DIM), F32))
                 for _ in range(FLASH_HP))
    carries = lax.fori_loop(0, qi * ratio, lambda ki, c: tiles(ki, c, None), init)
    for d in range(ratio):
        carries = tiles(qi * ratio + d, carries, d)
    for hh in range(FLASH_HP):
        m, l, acc = carries[hh]
        o_ref[:, hh * V_DIM:(hh + 1) * V_DIM] = (acc / l).astype(o_ref.dtype)


def _mla_prompt_attention(q, kfull, vfull, batch):
    m = q.shape[0]
    nq = SEQ // FLASH_T
    hp = FLASH_HP
    return pl.pallas_call(
        _flash_kernel,
        grid=(batch, MLA_HEADS // hp, nq),
        in_specs=[
            pl.BlockSpec((FLASH_T, hp * HEAD_PAD), lambda b, h, i: (b * nq + i, h)),
            pl.BlockSpec((SEQ, hp * HEAD_PAD), lambda b, h, i: (b, h)),
            pl.BlockSpec((SEQ, hp * V_DIM), lambda b, h, i: (b, h)),
        ],
        out_specs=pl.BlockSpec((FLASH_T, hp * V_DIM), lambda b, h, i: (b * nq + i, h)),
        out_shape=jax.ShapeDtypeStruct((m, MLA_HEADS * V_DIM), BF),
        compiler_params=_params(("arbitrary", "arbitrary", "arbitrary")),
        name="mla_prompt_attention",
    )(q, kfull, vfull)


def _headwise_kernel(x_ref, w_ref, g_ref, o_ref):
    x = (x_ref[...].astype(F32) * g_ref[...]).astype(BF)
    o_ref[...] = jnp.dot(x, w_ref[...].astype(BF), preferred_element_type=F32).astype(o_ref.dtype)


def _headwise_mm(name, x, x_spec, w, w_spec, g, out_shape, out_spec):
    kx = g.shape[-1]
    return pl.pallas_call(
        _headwise_kernel,
        grid=(MLA_HEADS,),
        in_specs=[x_spec, w_spec, pl.BlockSpec((1, kx), lambda h: (0, 0))],
        out_specs=out_spec,
        out_shape=out_shape,
        compiler_params=_params(("arbitrary",)),
        name=name,
    )(x, w, g)


PAGES_PER_STEP = 32


def _paged_kernel(pt_ref, ql_ref, q_ref, cn_ref, krn_ref, ksn_ref, lat_hbm, kr_hbm, ks_hbm, o_ref,
                  m_s, l_s, acc_s, cbuf, krbuf, ksbuf, lat_buf, kr_buf, ks_buf, sem):
    pp = PAGES_PER_STEP
    b = pl.program_id(0)
    c = pl.program_id(1)
    nb = pl.num_programs(0)
    nc = pl.num_programs(1)
    step = b * nc + c
    slot = lax.rem(step, 2)

    def fetch(bb, cc, s):
        for i in range(pp):
            pid = pt_ref[bb, cc * pp + i]
            pltpu.make_async_copy(lat_hbm.at[pid], lat_buf.at[s, i], sem.at[s]).start(priority=i % 2)
            pltpu.make_async_copy(kr_hbm.at[pid], kr_buf.at[s, i], sem.at[s]).start(priority=i % 2)
            pltpu.make_async_copy(ks_hbm.at[pid], ks_buf.at[s, i], sem.at[s]).start(priority=i % 2)

    def wait(s):
        pltpu.make_async_copy(lat_hbm.at[pl.ds(0, pp)], lat_buf.at[s], sem.at[s]).wait()
        pltpu.make_async_copy(kr_hbm.at[pl.ds(0, pp)], kr_buf.at[s], sem.at[s]).wait()
        pltpu.make_async_copy(ks_hbm.at[pl.ds(0, pp)], ks_buf.at[s], sem.at[s]).wait()

    @pl.when(step == 0)
    def _():
        fetch(b, c, 0)

    last = step == nb * nc - 1
    wrap = c == nc - 1
    nb_ = jnp.where(last, b, jnp.where(wrap, b + 1, b))
    nc_ = jnp.where(last, c, jnp.where(wrap, 0, c + 1))
    fetch(nb_, nc_, 1 - slot)
    wait(slot)

    @pl.when(c == 0)
    def _():
        m_s[...] = jnp.full(m_s.shape, NEG_BIG, F32)
        l_s[...] = jnp.zeros(l_s.shape, F32)
        acc_s[...] = jnp.zeros(acc_s.shape, F32)

    ql = ql_ref[0]
    qr32 = q_ref[0][:, LANES:LANES + ROPE_DIM]
    qr = qr32.astype(BF)
    for i in range(pp):
        rows = slice(i * PAGE_SIZE, (i + 1) * PAGE_SIZE)
        cbuf[rows, :] = lat_buf[slot, i].astype(BF)
        krbuf[rows, :] = kr_buf[slot, i].astype(BF)
        ksbuf[:, rows] = ks_buf[slot, i]
    keys = cbuf[...]
    s = (lax.dot_general(ql, keys, NT_DIMS, preferred_element_type=F32)
         + lax.dot_general(qr, krbuf[...], NT_DIMS, preferred_element_type=F32))
    s = s * ksbuf[...] * MLA_SCALE
    m_old = m_s[...]
    m_new = jnp.maximum(m_old, jnp.max(s, axis=-1, keepdims=True))
    alpha = jnp.exp(m_old - m_new)
    p = jnp.exp(s - m_new)
    l_new = alpha * l_s[...] + jnp.sum(p, axis=-1, keepdims=True)
    pv = jnp.dot(p.astype(BF), keys, preferred_element_type=F32)
    acc_new = alpha * acc_s[...] + pv
    m_s[...] = m_new
    l_s[...] = l_new
    acc_s[...] = acc_new

    @pl.when(c == nc - 1)
    def _():
        cn = cn_ref[0]
        krn = krn_ref[0][:, 0:ROPE_DIM]
        s_new = (jnp.sum(ql.astype(F32) * cn, axis=-1, keepdims=True)
                 + jnp.sum(qr32 * krn, axis=-1, keepdims=True)) * ksn_ref[0] * MLA_SCALE
        m2 = jnp.maximum(m_new, s_new)
        a2 = jnp.exp(m_new - m2)
        p2 = jnp.exp(s_new - m2)
        l2 = a2 * l_new + p2
        o_ref[0] = ((a2 * acc_new + p2 * cn) / l2).astype(o_ref.dtype)

    @pl.when(last)
    def _():
        wait(1 - slot)


def _mla_sample_attention(page_table, qlat3, q3, cache_latent, cache_krope, cache_kscale_t,
                          cn3, krn3, ksn3):
    nb, n_pages = page_table.shape
    pp = PAGES_PER_STEP
    nc = n_pages // pp
    any_spec = pl.BlockSpec(memory_space=pl.ANY)
    in_specs = [
        pl.BlockSpec((1, MLA_HEADS, KV_LORA), lambda b, c, pt: (b, 0, 0)),
        pl.BlockSpec((1, MLA_HEADS, HEAD_PAD), lambda b, c, pt: (b, 0, 0)),
        pl.BlockSpec((1, 1, KV_LORA), lambda b, c, pt: (b, 0, 0)),
        pl.BlockSpec((1, 1, LANES), lambda b, c, pt: (b, 0, 0)),
        pl.BlockSpec((1, MLA_HEADS, 1), lambda b, c, pt: (b, 0, 0)),
        any_spec, any_spec, any_spec,
    ]
    grid_spec = pltpu.PrefetchScalarGridSpec(
        num_scalar_prefetch=1,
        grid=(nb, nc),
        in_specs=in_specs,
        out_specs=pl.BlockSpec((1, MLA_HEADS, KV_LORA), lambda b, c, pt: (b, 0, 0)),
        scratch_shapes=[pltpu.VMEM((MLA_HEADS, 1), F32), pltpu.VMEM((MLA_HEADS, 1), F32),
                        pltpu.VMEM((MLA_HEADS, KV_LORA), F32),
                        pltpu.VMEM((pp * PAGE_SIZE, KV_LORA), BF), pltpu.VMEM((pp * PAGE_SIZE, ROPE_DIM), BF),
                        pltpu.VMEM((MLA_HEADS, pp * PAGE_SIZE), F32),
                        pltpu.VMEM((2, pp, PAGE_SIZE, KV_LORA), F32),
                        pltpu.VMEM((2, pp, PAGE_SIZE, ROPE_DIM), F32),
                        pltpu.VMEM((2, pp, MLA_HEADS, PAGE_SIZE), F32),
                        pltpu.SemaphoreType.DMA((2,))],
    )
    return pl.pallas_call(
        _paged_kernel,
        grid_spec=grid_spec,
        out_shape=jax.ShapeDtypeStruct((nb, MLA_HEADS, KV_LORA), F32),
        compiler_params=_params(("arbitrary", "arbitrary")),
        name="mla_sample_attention",
    )(page_table, qlat3, q3, cn3, krn3, ksn3, cache_latent, cache_krope, cache_kscale_t)


def _router_kernel(xp_ref, xs_ref, g_ref, w_ref, h_ref, gate_ref, info_ref, cnt_ref, cnt_s, *, nbp):
    i = pl.program_id(0)

    @pl.when(i == 0)
    def _():
        cnt_s[...] = jnp.zeros(cnt_s.shape, F32)

    xf = jnp.where(i < nbp, xp_ref[...], xs_ref[...])
    ms = jnp.mean(xf * xf, axis=-1, keepdims=True)
    h = xf * lax.rsqrt(ms + EPS) * g_ref[...]
    h_ref[...] = h
    logits = jnp.dot(h.astype(BF), w_ref[...].astype(BF),
                     preferred_element_type=F32)
    lane = lax.broadcasted_iota(jnp.int32, logits.shape, 1)
    big = jnp.int32(LANES)

    is_g = lane < N_GROUPS
    lg = jnp.where(is_g, logits, NEG_BIG)
    eg = jnp.where(is_g, jnp.exp(lg - jnp.max(lg, axis=-1, keepdims=True)), 0.0)
    pg = eg / jnp.sum(eg, axis=-1, keepdims=True)
    p_grp = jnp.max(pg, axis=-1, keepdims=True)
    grp = jnp.min(jnp.where(is_g & (pg == p_grp), lane, big), axis=-1, keepdims=True)

    e_idx = lane - N_GROUPS
    sel = (e_idx >= 0) & (e_idx < N_EXPERTS) & ((e_idx >> 3) == grp)
    le = jnp.where(sel, logits, NEG_BIG)
    ee = jnp.where(sel, jnp.exp(le - jnp.max(le, axis=-1, keepdims=True)), 0.0)
    pe = ee / jnp.sum(ee, axis=-1, keepdims=True)
    top1 = jnp.max(jnp.where(sel, pe, -1.0), axis=-1, keepdims=True)
    i1 = jnp.min(jnp.where(sel & (pe == top1), lane, big), axis=-1, keepdims=True)
    rest = sel & (lane != i1)
    top2 = jnp.max(jnp.where(rest, pe, -1.0), axis=-1, keepdims=True)
    i2 = jnp.min(jnp.where(rest & (pe == top2), lane, big), axis=-1, keepdims=True)
    denom = top1 + top2
    g1 = p_grp * top1 / denom
    g2 = p_grp * top2 / denom
    gate_ref[...] = jnp.where(lane == 0, g1, jnp.where(lane == 1, g2, 0.0))

    onehot = ((lane == i1) | (lane == i2)).astype(F32)
    r_i = lax.broadcasted_iota(jnp.int32, (MOE_BLOCK, MOE_BLOCK), 0)
    c_i = lax.broadcasted_iota(jnp.int32, (MOE_BLOCK, MOE_BLOCK), 1)
    tri = (c_i < r_i).astype(BF)
    before = jnp.dot(tri, onehot.astype(BF), preferred_element_type=F32) + cnt_s[...]
    rank1 = jnp.sum(jnp.where(lane == i1, before, 0.0), axis=-1, keepdims=True).astype(jnp.int32)
    rank2 = jnp.sum(jnp.where(lane == i2, before, 0.0), axis=-1, keepdims=True).astype(jnp.int32)
    info_ref[...] = jnp.where(lane == 0, i1 - N_GROUPS, jnp.where(lane == 1, i2 - N_GROUPS,
                              jnp.where(lane == 2, rank1, jnp.where(lane == 3, rank2, 0))))
    cnt_new = cnt_s[...] + jnp.sum(onehot, axis=0, keepdims=True)
    cnt_s[...] = cnt_new
    cnt_ref[...] = jnp.broadcast_to(cnt_new, cnt_ref.shape)


def _router(xp, xs, g, w_router):
    nbp = xp.shape[0] // MOE_BLOCK
    assert xs.shape[0] == MOE_BLOCK
    mt = xp.shape[0] + xs.shape[0]
    tok_spec = lambda width: pl.BlockSpec((MOE_BLOCK, width), lambda i: (i, 0))
    return pl.pallas_call(
        functools.partial(_router_kernel, nbp=nbp),
        grid=(nbp + 1,),
        in_specs=[
            pl.BlockSpec((MOE_BLOCK, D_MODEL), lambda i: (jnp.minimum(i, nbp - 1), 0)),
            pl.BlockSpec((MOE_BLOCK, D_MODEL), lambda i: (0, 0)),
            pl.BlockSpec((1, D_MODEL), lambda i: (0, 0)),
            pl.BlockSpec((D_MODEL, LANES), lambda i: (0, 0)),
        ],
        out_specs=[tok_spec(D_MODEL), tok_spec(LANES), tok_spec(LANES),
                   pl.BlockSpec((SUBLANES, LANES), lambda i: (0, 0))],
        out_shape=[jax.ShapeDtypeStruct((mt, D_MODEL), F32),
                   jax.ShapeDtypeStruct((mt, LANES), F32),
                   jax.ShapeDtypeStruct((mt, LANES), jnp.int32),
                   jax.ShapeDtypeStruct((SUBLANES, LANES), F32)],
        scratch_shapes=[pltpu.VMEM((1, LANES), F32)],
        compiler_params=_params(("arbitrary",)),
        name="moe_router",
    )(xp, xs, g.reshape(1, D_MODEL), w_router)


TRASH_ROWS = 2 * MOE_BLOCK


def _expert_kernel(be_ref, nu_ref, par_ref, nxt_ref, src_ref, srcn_ref, dst_ref,
                   h_ref, wg_hbm, wu_hbm, wd_hbm, ys_ref,
                   xbuf, ybuf, wg_f, wu_f, wd_f, wg_s, wu_s, wd_s, gsem, ssem, wsem, *, layer, n_slots):
    b = pl.program_id(0)
    nu = nu_ref[0]
    rows = MOE_BLOCK

    def gather_start(tab_ref, slot):
        def body(t, carry):
            pltpu.make_async_copy(h_ref.at[pl.ds(tab_ref[0, 0, t], 1)], xbuf.at[slot, pl.ds(t, 1)],
                                  gsem.at[slot]).start()
            return carry
        lax.fori_loop(0, rows, body, 0, unroll=8)

    def weight_copies(e, slot):
        return [pltpu.make_async_copy(w.at[layer, e], f.at[slot], wsem.at[slot])
                for w, f in ((wg_hbm, wg_f), (wu_hbm, wu_f), (wd_hbm, wd_f))]

    def scatter_wait():
        pltpu.make_async_copy(ybuf, ys_ref.at[pl.ds(0, rows)], ssem).wait()

    @pl.when(b < nu)
    def _():
        slot = lax.rem(b, 2)

        @pl.when(b == 0)
        def _():
            gather_start(src_ref, 0)

        @pl.when(b + 1 < nu)
        def _():
            gather_start(srcn_ref, 1 - slot)

        e = be_ref[b]
        p = par_ref[b]
        fresh = (b == 0) | (e != be_ref[jnp.maximum(b - 1, 0)])

        @pl.when(fresh)
        def _():
            @pl.when(b == 0)
            def _():
                for c in weight_copies(e, p):
                    c.start(priority=1)

            for c in weight_copies(e, p):
                c.wait()
            nxt = nxt_ref[b]

            @pl.when(nxt >= 0)
            def _():
                for c in weight_copies(nxt, 1 - p):
                    c.start(priority=1)

            wg_s[...] = wg_f[p].astype(BF)
            wu_s[...] = wu_f[p].astype(BF)
            wd_s[...] = wd_f[p].astype(BF)

        pltpu.make_async_copy(h_ref.at[pl.ds(0, rows)], xbuf.at[slot], gsem.at[slot]).wait()
        x = xbuf[slot].astype(BF)
        a = jnp.dot(x, wg_s[...], preferred_element_type=F32)
        u = jnp.dot(x, wu_s[...], preferred_element_type=F32)
        hmid = (a * jax.nn.sigmoid(a) * u).astype(BF)

        @pl.when(b >= 1)
        def _():
            scatter_wait()

        ybuf[...] = jnp.dot(hmid, wd_s[...], preferred_element_type=F32)

        def scatter_body(t, carry):
            pltpu.make_async_copy(ybuf.at[pl.ds(t, 1)], ys_ref.at[pl.ds(dst_ref[0, 0, t], 1)], ssem).start()
            return carry
        lax.fori_loop(0, rows, scatter_body, 0, unroll=8)

        @pl.when(b == nu - 1)
        def _():
            scatter_wait()
            ybuf[...] = jnp.zeros(ybuf.shape, F32)
            for c in range(TRASH_ROWS // rows):
                cp = pltpu.make_async_copy(ybuf, ys_ref.at[pl.ds(n_slots + c * rows, rows)], ssem)
                cp.start()
                cp.wait()


def _experts(blk_e, n_used, par, nxt, row_src3, row_dst3, h, w_gate, w_up, w_down, layer, n_slots):
    nblk = row_src3.shape[0]
    tab_spec = lambda f: pl.BlockSpec((1, 1, MOE_BLOCK), f, memory_space=pltpu.SMEM)
    any_spec = pl.BlockSpec(memory_space=pl.ANY)
    grid_spec = pltpu.PrefetchScalarGridSpec(
        num_scalar_prefetch=4,
        grid=(nblk,),
        in_specs=[
            tab_spec(lambda b, *_: (b, 0, 0)),
            tab_spec(lambda b, *_: (jnp.minimum(b + 1, nblk - 1), 0, 0)),
            tab_spec(lambda b, *_: (b, 0, 0)),
            any_spec, any_spec, any_spec, any_spec,
        ],
        out_specs=any_spec,
        scratch_shapes=[
            pltpu.VMEM((2, MOE_BLOCK, D_MODEL), F32), pltpu.VMEM((MOE_BLOCK, D_MODEL), F32),
            pltpu.VMEM((2, D_MODEL, D_EXPERT), F32), pltpu.VMEM((2, D_MODEL, D_EXPERT), F32),
            pltpu.VMEM((2, D_EXPERT, D_MODEL), F32),
            pltpu.VMEM((D_MODEL, D_EXPERT), BF), pltpu.VMEM((D_MODEL, D_EXPERT), BF),
            pltpu.VMEM((D_EXPERT, D_MODEL), BF),
            pltpu.SemaphoreType.DMA((2,)), pltpu.SemaphoreType.DMA(()), pltpu.SemaphoreType.DMA((2,)),
        ],
    )
    return pl.pallas_call(
        functools.partial(_expert_kernel, layer=layer, n_slots=n_slots),
        grid_spec=grid_spec,
        out_shape=jax.ShapeDtypeStruct((n_slots + TRASH_ROWS, D_MODEL), F32),
        compiler_params=_params(("arbitrary",)),
        name="moe_experts",
    )(blk_e, n_used, par, nxt, row_src3, row_src3, row_dst3, h, w_gate, w_up, w_down)


def _combine_kernel(gate_ref, xp_ref, xs_ref, y0_ref, y1_ref, op_ref, os_ref, *, nbp):
    i = pl.program_id(0)
    y = y0_ref[...] * gate_ref[:, 0:1] + y1_ref[...] * gate_ref[:, 1:2]

    @pl.when(i < nbp)
    def _():
        op_ref[...] = xp_ref[...] + y

    @pl.when(i == nbp)
    def _():
        os_ref[...] = xs_ref[...] + y


def _combine(gate, xp, xs, ys):
    nbp = xp.shape[0] // MOE_BLOCK
    nb_tok = nbp + 1
    p_spec = pl.BlockSpec((MOE_BLOCK, D_MODEL), lambda i: (jnp.minimum(i, nbp - 1), 0))
    s_spec = pl.BlockSpec((MOE_BLOCK, D_MODEL), lambda i: (0, 0))
    return pl.pallas_call(
        functools.partial(_combine_kernel, nbp=nbp),
        grid=(nb_tok,),
        in_specs=[
            pl.BlockSpec((MOE_BLOCK, LANES), lambda i: (i, 0)),
            p_spec, s_spec,
            pl.BlockSpec((MOE_BLOCK, D_MODEL), lambda i: (i, 0)),
            pl.BlockSpec((MOE_BLOCK, D_MODEL), lambda i: (nb_tok + i, 0)),
        ],
        out_specs=[p_spec, s_spec],
        out_shape=[jax.ShapeDtypeStruct(xp.shape, F32), jax.ShapeDtypeStruct(xs.shape, F32)],
        compiler_params=_params(("arbitrary",)),
        name="moe_combine",
    )(gate, xp, xs, ys, ys)


def _moe(xp, xs, g_ffn_l, w_router, w_gate, w_up, w_down, layer):
    mt = xp.shape[0] + xs.shape[0]
    n_slots = 2 * mt
    nblk = (n_slots + N_EXPERTS * (MOE_BLOCK - 1) + MOE_BLOCK - 1) // MOE_BLOCK
    h, gate, info, cnt = _router(xp, xs, g_ffn_l, w_router)

    ids = jnp.arange(N_EXPERTS, dtype=jnp.int32)
    counts = cnt[0, N_GROUPS:N_GROUPS + N_EXPERTS].astype(jnp.int32)
    used = counts > 0
    padded = (counts + MOE_BLOCK - 1) // MOE_BLOCK * MOE_BLOCK
    pad_ends = jnp.cumsum(padded)
    pad_starts = pad_ends - padded
    n_used = (pad_ends[-1] // MOE_BLOCK).astype(jnp.int32)
    blk_ids = jnp.arange(nblk, dtype=jnp.int32)
    blk_e = jnp.sum((pad_ends[None, :] <= blk_ids[:, None] * MOE_BLOCK).astype(jnp.int32), axis=1)
    last_e = jnp.max(jnp.where(used, ids, 0))
    blk_e = jnp.where(blk_ids < n_used, jnp.minimum(blk_e, N_EXPERTS - 1), last_e).astype(jnp.int32)
    ordinal = jnp.cumsum(used.astype(jnp.int32)) - 1
    next_e = jnp.min(jnp.where((ids[None, :] > ids[:, None]) & used[None, :], ids[None, :], N_EXPERTS), axis=1)
    next_e = jnp.where(next_e == N_EXPERTS, -1, next_e)
    blk_onehot = blk_e[:, None] == ids[None, :]
    par = jnp.sum(jnp.where(blk_onehot, (ordinal % 2)[None, :], 0), axis=1).astype(jnp.int32)
    nxt = jnp.sum(jnp.where(blk_onehot, next_e[None, :], 0), axis=1).astype(jnp.int32)

    e2, rank2 = info[:, 0:2], info[:, 2:4]
    start2 = jnp.sum(jnp.where(e2[:, :, None] == ids[None, None, :], pad_starts[None, None, :], 0), axis=-1)
    dest = start2 + rank2
    slot_code = jnp.arange(mt, dtype=jnp.int32)[:, None] + jnp.array([[0, mt]], jnp.int32)
    rows = jnp.arange(nblk * MOE_BLOCK, dtype=jnp.int32)
    row_dst = (n_slots + rows % TRASH_ROWS).at[dest.reshape(-1)].set(slot_code.reshape(-1))
    row_src = jnp.where(row_dst < n_slots, row_dst % mt, 0)
    shape3 = (nblk, 1, MOE_BLOCK)

    ys = _experts(blk_e, n_used.reshape(1), par, nxt, row_src.reshape(shape3), row_dst.reshape(shape3),
                  h, w_gate, w_up, w_down, layer, n_slots)
    return _combine(gate, xp, xs, ys)


def _rope_tables(pos):
    half = ROPE_DIM // 2
    inv = ROPE_THETA ** (-jnp.arange(half, dtype=F32) / half)
    ang = pos.astype(F32)[:, None] * inv[None, :]
    cos, sin = jnp.cos(ang), jnp.sin(ang)
    z = jnp.zeros_like(cos)
    a = jnp.concatenate([cos, cos, z, z], axis=1)
    b = jnp.concatenate([z, sin, z, z], axis=1)
    c = jnp.concatenate([-sin, z, z, z], axis=1)
    return a, b, c


def _out_spec(tm, tn):
    return pl.BlockSpec((tm, tn), lambda i, j: (i, j))


def _table_specs(tabs, tm, nper):
    return [(t, pl.BlockSpec((tm, LANES), lambda i, j, nper=nper: (i % nper, 0))) for t in tabs]


def _vec_extra(v):
    n = v.shape[-1]
    return (v.reshape(1, n), pl.BlockSpec((1, n), lambda i, j: (0, 0)))


def kernel(x_prompt, x_sample, cache_latent, cache_krope, cache_kscale, cache_mem_k, cache_mem_v, state_conv, page_table, mem_prompt, g_mix, g_ffn, g_mem, w_mem_kv, g_mq, g_mk, w_in_a, conv_w, conv_b, ln_g, ln_b, w_out_a, g_kv_in, w_kv_down, g_kv_lat, w_uk, w_uv, g_k, w_in_b, g_qlat, w_q_up, g_q, w_out_b, w_rg, w_re, w_gate, w_up, w_down):
    batch, seq, _ = x_prompt.shape
    nb = x_sample.shape[0]
    depth = g_mix.shape[0]
    n_a = w_in_a.shape[0]
    mp = batch * seq
    tmp = 1024
    xp = x_prompt.reshape(mp, D_MODEL)
    xs = x_sample.reshape(nb, D_MODEL)

    tabs_p = _rope_tables(jnp.arange(seq))
    tabs_s = _rope_tables(jnp.full((nb,), PAST_LEN))
    w_kvd = jnp.pad(w_kv_down, ((0, 0), (0, LANES - ROPE_DIM)))
    gk_n = g_k[:NOPE_DIM]
    gk_r = jnp.pad(g_k[NOPE_DIM:], (0, LANES - ROPE_DIM))
    w_uk2 = w_uk.reshape(KV_LORA, MLA_HEADS * NOPE_DIM)
    w_uk_t = jnp.transpose(w_uk, (1, 2, 0))
    w_uv2 = w_uv.reshape(KV_LORA, MLA_HEADS * V_DIM)
    w_qu = jnp.pad(w_q_up, ((0, 0), (0, 0), (0, 0), (0, HEAD_PAD - QK_DIM)))
    w_qu = w_qu.reshape(w_q_up.shape[0], Q_LORA, MLA_HEADS * HEAD_PAD)
    gq_pad = jnp.pad(g_q, ((0, 0), (0, HEAD_PAD - QK_DIM)))
    w_router = jnp.pad(jnp.concatenate([w_rg, w_re], axis=-1),
                       ((0, 0), (0, 0), (0, LANES - N_GROUPS - N_EXPERTS)))
    conv_w32 = jnp.pad(conv_w, ((0, 0), (0, 32 - CONV_W), (0, 0)))
    cache_kscale_t = jnp.transpose(cache_kscale, (0, 2, 1))

    memk, memv = _memory_kv_all(mem_prompt.reshape(batch * N_MEM, D_MODEL), g_mem, w_mem_kv, g_mk)

    conv_p_list, conv_s_list = [], []
    shared = None
    for l in range(depth):
        if l < n_a:
            n_glu = D_MODEL
            outs_p, outs_s = [], []
            for x, m, tm, dst in ((xp, mp, tmp, outs_p), (xs, nb, nb, outs_s)):
                tn = 256
                ng = n_glu // tn
                glu_col = lambda j, ng=ng: jnp.minimum(j, ng - 1)
                a, qm = _fused_mm(
                    "in_a", [(x, D_MODEL, 0)],
                    [(0, w_in_a, l, D_MODEL, tn, 0, glu_col),
                     (0, w_in_a, l, D_MODEL, tn, 0, lambda j, ng=ng: ng + jnp.minimum(j, ng - 1)),
                     (0, w_in_a, l, D_MODEL, tn, 0, lambda j, ng=ng: 2 * ng + jnp.maximum(j - ng, 0))],
                    m=m, tm=tm, nj=ng + MEM_W // tn, gain=g_mix[l],
                    outs=[(jax.ShapeDtypeStruct((m, n_glu), F32),
                           pl.BlockSpec((tm, tn), lambda i, j, ng=ng: (i, jnp.minimum(j, ng - 1)))),
                          (jax.ShapeDtypeStruct((m, MEM_W), F32),
                           pl.BlockSpec((tm, tn), lambda i, j, ng=ng: (i, jnp.maximum(j - ng, 0))))],
                    epi=functools.partial(_epi_glu_memq, ng=ng))
                dst.extend([a, qm])
            a_p, qm_p = outs_p
            a_s, qm_s = outs_s
            yc_p = _conv_prompt(a_p, conv_w32[l], conv_b[l], ln_g[l], ln_b[l], batch)
            conv_p_list.append(a_p.reshape(batch, seq, D_MODEL)[:, seq - (CONV_W - 1):])
            yc_s, ns = _conv_sample(state_conv[l], a_s, conv_w32[l], conv_b[l], ln_g[l], ln_b[l])
            conv_s_list.append(ns)
            ym_p = _memattn_prompt(qm_p, 0, memk, memv, l, g_mq[l], batch)
            ym_s = _memattn_sample(qm_s.reshape(nb, MEM_HEADS, MEM_HEAD_DIM), cache_mem_k, cache_mem_v,
                                   l, g_mq[l])
            new = []
            for x, m, tm, y1, y2 in ((xp, mp, tmp, yc_p, ym_p), (xs, nb, nb, yc_s, ym_s)):
                tn = 512
                new.append(_fused_mm(
                    "out_a", [(y1, D_MODEL, 0), (y2, MEM_W, 0)],
                    [(0, w_out_a, l, D_MODEL, tn, 0, 0), (1, w_out_a, l, MEM_W, tn, D_MODEL // MEM_W, 0)],
                    m=m, tm=tm, nj=D_MODEL // tn,
                    extras=[(x, _out_spec(tm, tn))],
                    outs=[(jax.ShapeDtypeStruct((m, D_MODEL), F32), _out_spec(tm, tn))],
                    epi=_epi_residual)[0])
            xp, xs = new
        else:
            jb = l - n_a
            kfull, vfull, c_s, kr_s, ks_s = shared
            att = []
            for x, m, tm, tabs, nper, qdt in ((xp, mp, tmp, tabs_p, seq // tmp, BF),
                                              (xs, nb, nb, tabs_s, 1, F32)):
                tn = Q_LORA
                n_in = Q_LORA + MEM_W
                u, qm = _fused_mm(
                    "in_b", [(x, D_MODEL, 0)], [(0, w_in_b, jb, D_MODEL, tn, 0, 0)],
                    m=m, tm=tm, nj=n_in // tn, gain=g_mix[l],
                    outs=[(jax.ShapeDtypeStruct((m, Q_LORA), F32),
                           pl.BlockSpec((tm, tn), lambda i, j: (i, 0))),
                          (jax.ShapeDtypeStruct((m, MEM_W), F32),
                           pl.BlockSpec((tm, tn), lambda i, j: (i, jnp.maximum(j - 1, 0))))],
                    epi=_epi_split)
                qh = _fused_mm(
                    "q_up", [(u, Q_LORA, 0)], [(0, w_qu, jb, Q_LORA, HEAD_PAD, 0, 0)],
                    m=m, tm=tm, nj=MLA_HEADS, gain=g_qlat[jb],
                    extras=[_vec_extra(gq_pad[jb])] + _table_specs(tabs, tm, nper),
                    outs=[(jax.ShapeDtypeStruct((m, MLA_HEADS * HEAD_PAD), qdt), _out_spec(tm, HEAD_PAD))],
                    epi=_epi_qhead)[0]
                att.append((qm, qh))
            (qm_p, qh_p), (qm_s, qh_s) = att
            att_p = _mla_prompt_attention(qh_p, kfull, vfull, batch)
            ym_p = _memattn_prompt(qm_p, 0, memk, memv, l, g_mq[l], batch)
            ym_s = _memattn_sample(qm_s.reshape(nb, MEM_HEADS, MEM_HEAD_DIM), cache_mem_k, cache_mem_v,
                                   l, g_mq[l])

            qlat = _headwise_mm(
                "q_latent", qh_s, pl.BlockSpec((nb, NOPE_DIM), lambda h: (0, 2 * h)),
                w_uk_t, pl.BlockSpec((None, NOPE_DIM, KV_LORA), lambda h: (h, 0, 0)),
                gk_n.reshape(1, NOPE_DIM),
                jax.ShapeDtypeStruct((nb, MLA_HEADS * KV_LORA), BF),
                pl.BlockSpec((nb, KV_LORA), lambda h: (0, h)))
            o_lat = _mla_sample_attention(
                page_table, qlat.reshape(nb, MLA_HEADS, KV_LORA),
                qh_s.reshape(nb, MLA_HEADS, HEAD_PAD), cache_latent, cache_krope, cache_kscale_t,
                c_s.reshape(nb, 1, KV_LORA), kr_s.reshape(nb, 1, LANES),
                ks_s[:, :MLA_HEADS].reshape(nb, MLA_HEADS, 1))
            att_s = _headwise_mm(
                "v_expand", o_lat.reshape(nb, MLA_HEADS * KV_LORA),
                pl.BlockSpec((nb, KV_LORA), lambda h: (0, h)),
                w_uv2, pl.BlockSpec((KV_LORA, V_DIM), lambda h: (0, h)),
                jnp.ones((1, KV_LORA), F32),
                jax.ShapeDtypeStruct((nb, MLA_HEADS * V_DIM), BF),
                pl.BlockSpec((nb, V_DIM), lambda h: (0, h)))
            new = []
            n_att = MLA_HEADS * V_DIM
            for x, m, tm, y1, y2 in ((xp, mp, tmp, att_p, ym_p), (xs, nb, nb, att_s, ym_s)):
                tn = 512
                new.append(_fused_mm(
                    "out_b", [(y1, n_att, 0), (y2, MEM_W, 0)],
                    [(0, w_out_b, jb, n_att, tn, 0, 0), (1, w_out_b, jb, MEM_W, tn, n_att // MEM_W, 0)],
                    m=m, tm=tm, nj=D_MODEL // tn,
                    extras=[(x, _out_spec(tm, tn))],
                    outs=[(jax.ShapeDtypeStruct((m, D_MODEL), F32), _out_spec(tm, tn))],
                    epi=_epi_residual)[0])
            xp, xs = new

        xp, xs = _moe(xp, xs, g_ffn[l], w_router[l], w_gate, w_up, w_down, l)

        if l == n_a - 1:
            lat = []
            for x, m, tm, tabs, nper in ((xp, mp, tmp, tabs_p, seq // tmp), (xs, nb, nb, tabs_s, 1)):
                nck = KV_LORA + LANES
                c, kr128, ssq128 = _fused_mm(
                    "kv_latent", [(x, D_MODEL, 0)], [(0, w_kvd, None, D_MODEL, nck, 0, 0)],
                    m=m, tm=tm, nj=1, gain=g_kv_in,
                    extras=[_vec_extra(g_kv_lat), _vec_extra(gk_r)] + _table_specs(tabs, tm, nper),
                    outs=[(jax.ShapeDtypeStruct((m, KV_LORA), F32), _out_spec(tm, KV_LORA)),
                          (jax.ShapeDtypeStruct((m, LANES), F32), pl.BlockSpec((tm, LANES), lambda i, j: (i, 0))),
                          (jax.ShapeDtypeStruct((m, LANES), F32), pl.BlockSpec((tm, LANES), lambda i, j: (i, 0)))],
                    epi=_epi_latent)
                kf, ks128 = _fused_mm(
                    "k_nope", [(c, KV_LORA, 0)], [(0, w_uk2, None, KV_LORA, 2 * NOPE_DIM, 0, 0)],
                    m=m, tm=tm, nj=MLA_HEADS // 2,
                    extras=[(kr128, pl.BlockSpec((tm, LANES), lambda i, j: (i, 0))),
                            (ssq128, pl.BlockSpec((tm, LANES), lambda i, j: (i, 0))),
                            _vec_extra(gk_n)],
                    outs=[(jax.ShapeDtypeStruct((m, MLA_HEADS * HEAD_PAD), BF), _out_spec(tm, 2 * HEAD_PAD)),
                          (jax.ShapeDtypeStruct((m, LANES), F32), pl.BlockSpec((tm, LANES), lambda i, j: (i, 0)))],
                    epi=_epi_knope)
                lat.append((c, kr128, ks128, kf))
            (c_p, kr_p, ks_p, kfull), (c_s, kr_s, ks_s, _) = lat
            vfull = _fused_mm(
                "v_full", [(c_p, KV_LORA, 0)], [(0, w_uv2, None, KV_LORA, 512, 0, 0)],
                m=mp, tm=tmp, nj=MLA_HEADS * V_DIM // 512,
                outs=[(jax.ShapeDtypeStruct((mp, MLA_HEADS * V_DIM), BF), _out_spec(tmp, 512))],
                epi=_epi_plain)[0]
            shared = (kfull, vfull, c_s, kr_s, ks_s)

    shape5 = (depth, batch, N_MEM, MEM_HEADS, MEM_HEAD_DIM)
    return (xp.reshape(batch, seq, D_MODEL), xs.reshape(nb, 1, D_MODEL),
            c_p.reshape(batch, seq, KV_LORA), kr_p[:, :ROPE_DIM].reshape(batch, seq, ROPE_DIM),
            ks_p[:, :MLA_HEADS].reshape(batch, seq, MLA_HEADS),
            memk.reshape(shape5), memv.reshape(shape5),
            jnp.stack(conv_p_list, axis=0),
            c_s.reshape(nb, 1, KV_LORA), kr_s[:, :ROPE_DIM].reshape(nb, 1, ROPE_DIM),
            ks_s[:, :MLA_HEADS].reshape(nb, 1, MLA_HEADS),
            jnp.stack(conv_s_list, axis=0))
```

```python
import functools
import math

import jax
import jax.numpy as jnp
from jax import lax
from jax.experimental import pallas as pl
from jax.experimental.pallas import tpu as pltpu

F32 = jnp.float32
BF = jnp.bfloat16

D_MODEL = 2048
SEQ = 2048
PAST_LEN = 8192
PAGE_SIZE = 128
CONV_W = 31
N_MEM = 256
MEM_HEADS = 4
MEM_HEAD_DIM = 256
MEM_W = MEM_HEADS * MEM_HEAD_DIM
MLA_HEADS = 16
Q_LORA = 512
KV_LORA = 512
NOPE_DIM = 128
ROPE_DIM = 64
QK_DIM = NOPE_DIM + ROPE_DIM
V_DIM = 128
ROPE_THETA = 10000.0
N_GROUPS = 8
EXPERTS_PER_GROUP = 8
N_EXPERTS = 64
D_EXPERT = 512
MOE_BLOCK = 128
EPS = 1e-6

HEAD_PAD = 256
LANES = 128
VMEM_LIMIT_BYTES = 56 * 1024 * 1024
NEG_BIG = -1e30
NT_DIMS = (((1,), (1,)), ((), ()))


def _params(sem):
    return pltpu.CompilerParams(dimension_semantics=sem, vmem_limit_bytes=VMEM_LIMIT_BYTES)


def _mm_kernel(*refs, nx, w_x, has_gain, n_extra, n_out, epi, tm):
    x_refs = refs[:nx]
    w_refs = refs[nx:nx + len(w_x)]
    pos = nx + len(w_x)
    g_ref = refs[pos] if has_gain else None
    pos += int(has_gain)
    extra_refs = refs[pos:pos + n_extra]
    pos += n_extra
    out_refs = refs[pos:pos + n_out]
    pos += n_out
    h_s = refs[pos] if has_gain else None
    j = pl.program_id(1)

    if has_gain:
        chunk = min(tm, 128)

        @pl.when(j == 0)
        def _():
            def body(c, carry):
                r = pl.multiple_of(c * chunk, chunk)
                xf = x_refs[0][pl.ds(r, chunk), :].astype(F32)
                ms = jnp.mean(xf * xf, axis=-1, keepdims=True)
                h_s[pl.ds(r, chunk), :] = (xf * lax.rsqrt(ms + EPS) * g_ref[...]).astype(BF)
                return carry
            lax.fori_loop(0, tm // chunk, body, 0)

    def product(wi):
        xi = w_x[wi]
        lhs = h_s[...] if has_gain and xi == 0 else x_refs[xi][...].astype(BF)
        return jnp.dot(lhs, w_refs[wi][...].astype(BF), preferred_element_type=F32)

    epi(_Products(product, len(w_x)), extra_refs, out_refs, j)


class _Products:
    def __init__(self, fn, n):
        self._fn, self._n = fn, n

    def __getitem__(self, i):
        if isinstance(i, slice):
            return [self._fn(k) for k in range(*i.indices(self._n))]
        return self._fn(i)


def _fused_mm(name, xs, ws, *, m, tm, nj, gain=None, extras=(), outs, epi):
    in_arrays, in_specs = [], []
    for arr, k, cb in xs:
        in_arrays.append(arr)
        in_specs.append(pl.BlockSpec((tm, k), lambda i, j, cb=cb: (i, cb)))
    for xi, arr, layer, k, tn, rb, cb0 in ws:
        in_arrays.append(arr)
        col = cb0 if callable(cb0) else (lambda j, cb0=cb0: cb0 + j)
        if layer is None:
            in_specs.append(pl.BlockSpec((k, tn), lambda i, j, rb=rb, col=col: (rb, col(j))))
        else:
            in_specs.append(pl.BlockSpec((None, k, tn),
                                         lambda i, j, l=layer, rb=rb, col=col: (l, rb, col(j))))
    scratch = []
    if gain is not None:
        k0 = xs[0][1]
        in_arrays.append(gain.reshape(1, k0).astype(F32))
        in_specs.append(pl.BlockSpec((1, k0), lambda i, j: (0, 0)))
        scratch.append(pltpu.VMEM((tm, k0), BF))
    for arr, spec in extras:
        in_arrays.append(arr)
        in_specs.append(spec)
    kern = functools.partial(_mm_kernel, nx=len(xs), w_x=tuple(w[0] for w in ws),
                             has_gain=gain is not None, n_extra=len(extras),
                             n_out=len(outs), epi=epi, tm=tm)
    res = pl.pallas_call(
        kern,
        grid=(m // tm, nj),
        in_specs=in_specs,
        out_specs=[o[1] for o in outs],
        out_shape=[o[0] for o in outs],
        scratch_shapes=scratch,
        compiler_params=_params(("arbitrary", "arbitrary")),
        name=name,
    )(*in_arrays)
    return res


def _epi_plain(ds, ex, outs, j):
    acc = ds[0]
    for d in ds[1:]:
        acc = acc + d
    outs[0][...] = acc.astype(outs[0].dtype)


def _epi_residual(ds, ex, outs, j):
    acc = ds[0]
    for d in ds[1:]:
        acc = acc + d
    outs[0][...] = (ex[0][...] + acc).astype(outs[0].dtype)


def _epi_split(ds, ex, outs, j):
    @pl.when(j == 0)
    def _():
        outs[0][...] = ds[0]

    @pl.when(j > 0)
    def _():
        outs[1][...] = ds[0]


def _epi_glu_memq(ds, ex, outs, j, *, ng):
    @pl.when(j < ng)
    def _():
        outs[0][...] = (ds[0] * jax.nn.sigmoid(ds[1])).astype(outs[0].dtype)

    @pl.when(j >= ng)
    def _():
        outs[1][...] = ds[2].astype(outs[1].dtype)


def _rope_tile(t, a_ref, b_ref):
    return t * a_ref[...] + pltpu.roll(t, 32, 1) * b_ref[...]


def _epi_qhead(ds, ex, outs, j):
    acc = ds[0]
    gq_ref, real_ref, a_ref, b_ref = ex
    lo, hi = acc[:, :LANES], acc[:, LANES:]
    sq = lo * lo + hi * hi * real_ref[...]
    sq_hi = sq.astype(BF)
    sq_lo = (sq - sq_hi.astype(F32)).astype(BF)
    ones = jnp.ones((LANES, LANES), BF)
    ssq = (jnp.dot(sq_hi, ones, preferred_element_type=F32)
           + jnp.dot(sq_lo, ones, preferred_element_type=F32))
    inv = lax.rsqrt(ssq * (1.0 / QK_DIM) + EPS)
    outs[0][:, :LANES] = (lo * inv * gq_ref[:, :LANES]).astype(outs[0].dtype)
    outs[0][:, LANES:] = _rope_tile(hi * inv * gq_ref[:, LANES:], a_ref, b_ref).astype(outs[0].dtype)


def _epi_latent(ds, ex, outs, j):
    ck = ds[0]
    glat_ref, gkr_ref, real_ref, a_ref, b_ref = ex
    c_raw = ck[:, :KV_LORA]
    ms = jnp.mean(c_raw * c_raw, axis=-1, keepdims=True)
    outs[0][...] = c_raw * lax.rsqrt(ms + EPS) * glat_ref[...]
    krt = ck[:, KV_LORA:]
    ssq = jnp.sum(krt * krt * real_ref[...], axis=-1, keepdims=True)
    outs[1][...] = _rope_tile(krt * gkr_ref[...], a_ref, b_ref)
    outs[2][...] = jnp.broadcast_to(ssq, outs[2].shape)


def _epi_knope(ds, ex, outs, j):
    acc = ds[0]
    kr_ref, ssq_ref, gkn_ref = ex
    ssq_r = ssq_ref[:, 0:1]
    kr = kr_ref[...]
    lane = lax.broadcasted_iota(jnp.int32, outs[1].shape, 1)

    @pl.when(j == 0)
    def _():
        outs[1][...] = jnp.zeros(outs[1].shape, F32)

    ks_all = outs[1][...]
    for t in range(2):
        kn = acc[:, t * NOPE_DIM:(t + 1) * NOPE_DIM]
        ssq = jnp.sum(kn * kn, axis=-1, keepdims=True) + ssq_r
        ks = lax.rsqrt(ssq * (1.0 / QK_DIM) + EPS)
        outs[0][:, t * HEAD_PAD:t * HEAD_PAD + LANES] = (kn * gkn_ref[...] * ks).astype(outs[0].dtype)
        outs[0][:, t * HEAD_PAD + LANES:(t + 1) * HEAD_PAD] = (kr * ks).astype(outs[0].dtype)
        ks_all = jnp.where(lane == 2 * j + t, ks, ks_all)
    outs[1][...] = ks_all


def _memkv_kernel(x_ref, g_ref, wk_ref, wv_ref, gk_ref, k_out, v_out, h_s):
    j = pl.program_id(1)
    rows = x_ref.shape[0]
    chunk = 128

    @pl.when(j == 0)
    def _():
        def body(c, carry):
            r = pl.multiple_of(c * chunk, chunk)
            xf = x_ref[pl.ds(r, chunk), :]
            ms = jnp.mean(xf * xf, axis=-1, keepdims=True)
            h_s[pl.ds(r, chunk), :] = (xf * lax.rsqrt(ms + EPS) * g_ref[...]).astype(BF)
            return carry
        lax.fori_loop(0, rows // chunk, body, 0)

    h = h_s[...]
    k = jnp.dot(h, wk_ref[...].astype(BF), preferred_element_type=F32)
    ms = jnp.mean(k * k, axis=-1, keepdims=True)
    k_out[...] = k * lax.rsqrt(ms + EPS) * gk_ref[...]
    v_out[...] = jnp.dot(h, wv_ref[...].astype(BF), preferred_element_type=F32)


def _memory_kv_all(mem2d, g_mem, w_mem_kv, g_mk):
    depth = w_mem_kv.shape[0]
    rows = mem2d.shape[0]
    hd = MEM_HEAD_DIM
    out = pl.pallas_call(
        _memkv_kernel,
        grid=(depth, MEM_HEADS),
        in_specs=[
            pl.BlockSpec((rows, D_MODEL), lambda l, j: (0, 0)),
            pl.BlockSpec((None, 1, D_MODEL), lambda l, j: (l, 0, 0)),
            pl.BlockSpec((None, D_MODEL, hd), lambda l, j: (l, 0, j)),
            pl.BlockSpec((None, D_MODEL, hd), lambda l, j: (l, 0, MEM_HEADS + j)),
            pl.BlockSpec((None, 1, hd), lambda l, j: (l, 0, 0)),
        ],
        out_specs=[
            pl.BlockSpec((None, rows, hd), lambda l, j: (l, 0, j)),
            pl.BlockSpec((None, rows, hd), lambda l, j: (l, 0, j)),
        ],
        out_shape=[jax.ShapeDtypeStruct((depth, rows, MEM_W), F32)] * 2,
        scratch_shapes=[pltpu.VMEM((rows, D_MODEL), BF)],
        compiler_params=_params(("arbitrary", "arbitrary")),
        name="memory_kv",
    )(mem2d, g_mem.reshape(depth, 1, D_MODEL), w_mem_kv, w_mem_kv, g_mk.reshape(depth, 1, hd))
    return out


CONV_TT = 256
CONV_RC = 32
CONV_LW = 256
CONV_NORM_RC = 64
CONV_HALO = 32
SUBLANES = 8


def _conv_prompt_kernel(a_ref, w_ref, cb_ref, lg_ref, lb_ref, o_ref, sh, ybuf):
    t = pl.program_id(1)
    tt = CONV_TT
    span = tt + CONV_HALO

    @pl.when(t == 0)
    def _():
        sh[0, 0:CONV_HALO, :] = jnp.zeros((CONV_HALO, D_MODEL), F32)
        sh[0, span:span + SUBLANES, :] = jnp.zeros((SUBLANES, D_MODEL), F32)

    sh[0, CONV_HALO:span, :] = a_ref[...]
    for p in range(1, SUBLANES):
        sh[p, 0:span, :] = sh[0, p:p + span, :]

    groups = CONV_RC // SUBLANES

    def conv_body(c, carry):
        r = c * CONV_RC
        for lq in range(D_MODEL // CONV_LW):
            ls = slice(lq * CONV_LW, (lq + 1) * CONV_LW)
            accs = [jnp.zeros((SUBLANES, CONV_LW), F32) for _ in range(groups)]
            for p in range(SUBLANES):
                slabs = {}
                for m in range(CONV_W // SUBLANES + 2):
                    k = p - 2 + SUBLANES * m
                    if not 0 <= k < CONV_W:
                        continue
                    w = w_ref[k * SUBLANES:(k + 1) * SUBLANES, ls]
                    for g in range(groups):
                        j = m + g
                        if j not in slabs:
                            start = pl.multiple_of(r + SUBLANES * j, SUBLANES)
                            slabs[j] = sh[p, pl.ds(start, SUBLANES), ls]
                        accs[g] = accs[g] + slabs[j] * w
            ybuf[pl.ds(pl.multiple_of(r, CONV_RC), CONV_RC), ls] = jnp.concatenate(accs, axis=0)
        return carry

    lax.fori_loop(0, tt // CONV_RC, conv_body, 0)

    def norm_body(c, carry):
        rows = pl.ds(pl.multiple_of(c * CONV_NORM_RC, CONV_NORM_RC), CONV_NORM_RC)
        y = ybuf[rows, :] + cb_ref[...]
        mu = jnp.mean(y, axis=-1, keepdims=True)
        yc = y - mu
        var = jnp.mean(yc * yc, axis=-1, keepdims=True)
        z = yc * lax.rsqrt(var + EPS) * lg_ref[...] + lb_ref[...]
        o_ref[rows, :] = (z * jax.nn.sigmoid(z)).astype(o_ref.dtype)
        return carry

    lax.fori_loop(0, tt // CONV_NORM_RC, norm_body, 0)
    sh[0, 0:CONV_HALO, :] = sh[0, tt:span, :]


def _conv_prompt(a, w32, cb, lg, lb, batch):
    m = a.shape[0]
    nt = SEQ // CONV_TT
    vec = lambda v: v.reshape(1, D_MODEL)
    w_rep = jnp.repeat(w32, SUBLANES, axis=0)
    return pl.pallas_call(
        _conv_prompt_kernel,
        grid=(batch, nt),
        in_specs=[
            pl.BlockSpec((CONV_TT, D_MODEL), lambda b, t: (b * nt + t, 0)),
            pl.BlockSpec((32 * SUBLANES, D_MODEL), lambda b, t: (0, 0)),
            pl.BlockSpec((1, D_MODEL), lambda b, t: (0, 0)),
            pl.BlockSpec((1, D_MODEL), lambda b, t: (0, 0)),
            pl.BlockSpec((1, D_MODEL), lambda b, t: (0, 0)),
        ],
        out_specs=pl.BlockSpec((CONV_TT, D_MODEL), lambda b, t: (b * nt + t, 0)),
        out_shape=jax.ShapeDtypeStruct((m, D_MODEL), BF),
        scratch_shapes=[pltpu.VMEM((SUBLANES, CONV_TT + CONV_HALO + SUBLANES, D_MODEL), F32),
                        pltpu.VMEM((CONV_TT, D_MODEL), F32)],
        compiler_params=_params(("arbitrary", "arbitrary")),
        name="conv_prompt",
    )(a, w_rep, vec(cb), vec(lg), vec(lb))


CONV_SB = 8


def _conv_sample_kernel(st_ref, a_ref, w_ref, cb_ref, lg_ref, lb_ref, y_ref, ns_ref):
    nstate = CONV_W - 1
    for s in range(CONV_SB):
        st = st_ref[s]
        a = a_ref[s]
        y = (jnp.sum(st * w_ref[0:nstate, :], axis=0, keepdims=True)
             + a * w_ref[nstate:nstate + 1, :] + cb_ref[...])
        mu = jnp.mean(y, axis=-1, keepdims=True)
        yc = y - mu
        var = jnp.mean(yc * yc, axis=-1, keepdims=True)
        z = yc * lax.rsqrt(var + EPS) * lg_ref[...] + lb_ref[...]
        y_ref[s] = (z * jax.nn.sigmoid(z)).astype(y_ref.dtype)
        ns_ref[s, 0:nstate - 1, :] = st_ref[s, 1:nstate, :]
        ns_ref[s, nstate - 1:nstate, :] = a


def _conv_sample(state, a, w32, cb, lg, lb):
    nb = state.shape[0]
    nstate = CONV_W - 1
    vec = lambda v: v.reshape(1, D_MODEL)
    y, ns = pl.pallas_call(
        _conv_sample_kernel,
        grid=(nb // CONV_SB,),
        in_specs=[
            pl.BlockSpec((CONV_SB, nstate, D_MODEL), lambda i: (i, 0, 0)),
            pl.BlockSpec((CONV_SB, 1, D_MODEL), lambda i: (i, 0, 0)),
            pl.BlockSpec((32, D_MODEL), lambda i: (0, 0)),
            pl.BlockSpec((1, D_MODEL), lambda i: (0, 0)),
            pl.BlockSpec((1, D_MODEL), lambda i: (0, 0)),
            pl.BlockSpec((1, D_MODEL), lambda i: (0, 0)),
        ],
        out_specs=[
            pl.BlockSpec((CONV_SB, 1, D_MODEL), lambda i: (i, 0, 0)),
            pl.BlockSpec((CONV_SB, nstate, D_MODEL), lambda i: (i, 0, 0)),
        ],
        out_shape=[jax.ShapeDtypeStruct((nb, 1, D_MODEL), BF),
                   jax.ShapeDtypeStruct((nb, nstate, D_MODEL), F32)],
        compiler_params=_params(("arbitrary",)),
        name="conv_sample",
    )(state, a.reshape(nb, 1, D_MODEL), w32, vec(cb), vec(lg), vec(lb))
    return y.reshape(nb, D_MODEL), ns


MEM_TQ = 512
MEM_SCALE = MEM_HEAD_DIM ** -0.5


def _memattn_prompt_kernel(q_ref, k_ref, v_ref, g_ref, o_ref):
    for h in range(MEM_HEADS):
        sl = slice(h * MEM_HEAD_DIM, (h + 1) * MEM_HEAD_DIM)
        qh = q_ref[:, sl]
        ms = jnp.mean(qh * qh, axis=-1, keepdims=True)
        qn = (qh * lax.rsqrt(ms + EPS) * g_ref[...]).astype(BF)
        s = lax.dot_general(qn, k_ref[:, sl].astype(BF), NT_DIMS,
                            preferred_element_type=F32) * MEM_SCALE
        mx = jnp.max(s, axis=-1, keepdims=True)
        p = jnp.exp(s - mx)
        l = jnp.sum(p, axis=-1, keepdims=True)
        o = jnp.dot(p.astype(BF), v_ref[:, sl].astype(BF), preferred_element_type=F32)
        o_ref[:, sl] = (o / l).astype(o_ref.dtype)


def _memattn_prompt(q, q_colblk, memk, memv, layer, g_mq_l, batch):
    m = q.shape[0]
    nq = SEQ // MEM_TQ
    return pl.pallas_call(
        _memattn_prompt_kernel,
        grid=(batch, nq),
        in_specs=[
            pl.BlockSpec((MEM_TQ, MEM_W), lambda b, i: (b * nq + i, q_colblk)),
            pl.BlockSpec((None, N_MEM, MEM_W), lambda b, i: (layer, b, 0)),
            pl.BlockSpec((None, N_MEM, MEM_W), lambda b, i: (layer, b, 0)),
            pl.BlockSpec((1, MEM_HEAD_DIM), lambda b, i: (0, 0)),
        ],
        out_specs=pl.BlockSpec((MEM_TQ, MEM_W), lambda b, i: (b * nq + i, 0)),
        out_shape=jax.ShapeDtypeStruct((m, MEM_W), BF),
        compiler_params=_params(("arbitrary", "arbitrary")),
        name="memattn_prompt",
    )(q, memk, memv, g_mq_l.reshape(1, MEM_HEAD_DIM))


MEM_SB = 4


def _memattn_sample_kernel(q_ref, k_ref, v_ref, g_ref, o_ref):
    for s in range(MEM_SB):
        q = q_ref[s]
        ms = jnp.mean(q * q, axis=-1, keepdims=True)
        qn = q * lax.rsqrt(ms + EPS) * g_ref[...]
        sc = jnp.sum(k_ref[s] * qn[None], axis=-1, keepdims=True) * MEM_SCALE
        mx = jnp.max(sc, axis=0, keepdims=True)
        p = jnp.exp(sc - mx)
        l = jnp.sum(p, axis=0)
        o = jnp.sum(p * v_ref[s], axis=0) / l
        o_ref[s] = o.astype(o_ref.dtype)


def _memattn_sample(q3, cmk, cmv, layer, g_mq_l):
    nb = q3.shape[0]
    kv_spec = pl.BlockSpec((None, MEM_SB, N_MEM, MEM_HEADS, MEM_HEAD_DIM), lambda i: (layer, i, 0, 0, 0))
    out = pl.pallas_call(
        _memattn_sample_kernel,
        grid=(nb // MEM_SB,),
        in_specs=[
            pl.BlockSpec((MEM_SB, MEM_HEADS, MEM_HEAD_DIM), lambda i: (i, 0, 0)),
            kv_spec, kv_spec,
            pl.BlockSpec((1, MEM_HEAD_DIM), lambda i: (0, 0)),
        ],
        out_specs=pl.BlockSpec((MEM_SB, MEM_HEADS, MEM_HEAD_DIM), lambda i: (i, 0, 0)),
        out_shape=jax.ShapeDtypeStruct((nb, MEM_HEADS, MEM_HEAD_DIM), BF),
        compiler_params=_params(("arbitrary",)),
        name="memattn_sample",
    )(q3, cmk, cmv, g_mq_l.reshape(1, MEM_HEAD_DIM))
    return out.reshape(nb, MEM_W)


FLASH_T = 512
FLASH_TK = 512
FLASH_HP = 2
MLA_SCALE = QK_DIM ** -0.5


def _flash_kernel(q_ref, k_ref, v_ref, o_ref):
    qi = pl.program_id(2)
    t, tk = FLASH_T, FLASH_TK
    ratio = t // tk
    row = lax.broadcasted_iota(jnp.int32, (t, tk), 0)
    col = lax.broadcasted_iota(jnp.int32, (t, tk), 1)

    def tile(ki, carry, hh, diag):
        m, l, acc = carry
        r = pl.multiple_of(ki * tk, tk)
        q = q_ref[:, hh * HEAD_PAD:(hh + 1) * HEAD_PAD]
        k = k_ref[pl.ds(r, tk), hh * HEAD_PAD:(hh + 1) * HEAD_PAD]
        s = lax.dot_general(q, k, NT_DIMS, preferred_element_type=F32) * MLA_SCALE
        if diag is not None:
            s = jnp.where(col + diag * tk <= row, s, NEG_BIG)
        m_new = jnp.maximum(m, jnp.max(s, axis=-1, keepdims=True))
        alpha = jnp.exp(m - m_new)
        p = jnp.exp(s - m_new)
        l = alpha * l + jnp.sum(p, axis=-1, keepdims=True)
        v = v_ref[pl.ds(r, tk), hh * V_DIM:(hh + 1) * V_DIM]
        acc = alpha * acc + jnp.dot(p.astype(BF), v, preferred_element_type=F32)
        return m_new, l, acc

    def tiles(ki, carries, diag):
        return tuple(tile(ki, carries[hh], hh, diag) for hh in range(FLASH_HP))

    init = tuple((jnp.full((t, 1), NEG_BIG, F32), jnp.zeros((t, 1), F32), jnp.zeros((t, V_DIM), F32))
                 for _ in range(FLASH_HP))
    carries = lax.fori_loop(0, qi * ratio, lambda ki, c: tiles(ki, c, None), init)
    for d in range(ratio):
        carries = tiles(qi * ratio + d, carries, d)
    for hh in range(FLASH_HP):
        m, l, acc = carries[hh]
        o_ref[:, hh * V_DIM:(hh + 1) * V_DIM] = (acc / l).astype(o_ref.dtype)


def _mla_prompt_attention(q, kfull, vfull, batch):
    m = q.shape[0]
    nq = SEQ // FLASH_T
    hp = FLASH_HP
    return pl.pallas_call(
        _flash_kernel,
        grid=(batch, MLA_HEADS // hp, nq),
        in_specs=[
            pl.BlockSpec((FLASH_T, hp * HEAD_PAD), lambda b, h, i: (b * nq + i, h)),
            pl.BlockSpec((SEQ, hp * HEAD_PAD), lambda b, h, i: (b, h)),
            pl.BlockSpec((SEQ, hp * V_DIM), lambda b, h, i: (b, h)),
        ],
        out_specs=pl.BlockSpec((FLASH_T, hp * V_DIM), lambda b, h, i: (b * nq + i, h)),
        out_shape=jax.ShapeDtypeStruct((m, MLA_HEADS * V_DIM), BF),
        compiler_params=_params(("arbitrary", "arbitrary", "arbitrary")),
        name="mla_prompt_attention",
    )(q, kfull, vfull)


def _headwise_kernel(x_ref, w_ref, g_ref, o_ref):
    x = (x_ref[...].astype(F32) * g_ref[...]).astype(BF)
    o_ref[...] = jnp.dot(x, w_ref[...].astype(BF), preferred_element_type=F32).astype(o_ref.dtype)


def _headwise_mm(name, x, x_spec, w, w_spec, g, out_shape, out_spec):
    kx = g.shape[-1]
    return pl.pallas_call(
        _headwise_kernel,
        grid=(MLA_HEADS,),
        in_specs=[x_spec, w_spec, pl.BlockSpec((1, kx), lambda h: (0, 0))],
        out_specs=out_spec,
        out_shape=out_shape,
        compiler_params=_params(("arbitrary",)),
        name=name,
    )(x, w, g)


KSCALE_PB = 64


def _kscale_t_kernel(x_ref, o_ref):
    o_ref[...] = jnp.swapaxes(x_ref[...], 1, 2)


def _kscale_transpose(cache_kscale):
    n_pool = cache_kscale.shape[0]
    pb = math.gcd(n_pool, KSCALE_PB)
    return pl.pallas_call(
        _kscale_t_kernel,
        grid=(n_pool // pb,),
        in_specs=[pl.BlockSpec((pb, PAGE_SIZE, MLA_HEADS), lambda i: (i, 0, 0))],
        out_specs=pl.BlockSpec((pb, MLA_HEADS, PAGE_SIZE), lambda i: (i, 0, 0)),
        out_shape=jax.ShapeDtypeStruct((n_pool, MLA_HEADS, PAGE_SIZE), F32),
        compiler_params=_params(("arbitrary",)),
        name="kscale_relayout",
    )(cache_kscale)


PAGES_PER_STEP = 32


def _paged_kernel(pt_ref, ql_ref, q_ref, cn_ref, krn_ref, ksn_ref, lat_hbm, kr_hbm, ks_hbm, o_ref,
                  m_s, l_s, acc_s, cbuf, krbuf, ksbuf, lat_buf, kr_buf, ks_buf, sem):
    pp = PAGES_PER_STEP
    b = pl.program_id(0)
    c = pl.program_id(1)
    nb = pl.num_programs(0)
    nc = pl.num_programs(1)
    step = b * nc + c
    slot = lax.rem(step, 2)

    def fetch(bb, cc, s):
        for i in range(pp):
            pid = pt_ref[bb, cc * pp + i]
            pltpu.make_async_copy(lat_hbm.at[pid], lat_buf.at[s, i], sem.at[s]).start(priority=i % 2)
            pltpu.make_async_copy(kr_hbm.at[pid], kr_buf.at[s, i], sem.at[s]).start(priority=i % 2)
            pltpu.make_async_copy(ks_hbm.at[pid], ks_buf.at[s, i], sem.at[s]).start(priority=i % 2)

    def wait(s):
        pltpu.make_async_copy(lat_hbm.at[pl.ds(0, pp)], lat_buf.at[s], sem.at[s]).wait()
        pltpu.make_async_copy(kr_hbm.at[pl.ds(0, pp)], kr_buf.at[s], sem.at[s]).wait()
        pltpu.make_async_copy(ks_hbm.at[pl.ds(0, pp)], ks_buf.at[s], sem.at[s]).wait()

    @pl.when(step == 0)
    def _():
        fetch(b, c, 0)

    last = step == nb * nc - 1
    wrap = c == nc - 1
    nb_ = jnp.where(last, b, jnp.where(wrap, b + 1, b))
    nc_ = jnp.where(last, c, jnp.where(wrap, 0, c + 1))
    fetch(nb_, nc_, 1 - slot)
    wait(slot)

    @pl.when(c == 0)
    def _():
        m_s[...] = jnp.full(m_s.shape, NEG_BIG, F32)
        l_s[...] = jnp.zeros(l_s.shape, F32)
        acc_s[...] = jnp.zeros(acc_s.shape, F32)

    ql = ql_ref[0]
    qr32 = q_ref[0][:, LANES:LANES + ROPE_DIM]
    qr = qr32.astype(BF)
    for i in range(pp):
        rows = slice(i * PAGE_SIZE, (i + 1) * PAGE_SIZE)
        cbuf[rows, :] = lat_buf[slot, i].astype(BF)
        krbuf[rows, :] = kr_buf[slot, i].astype(BF)
        ksbuf[:, rows] = ks_buf[slot, i]
    keys = cbuf[...]
    s = (lax.dot_general(ql, keys, NT_DIMS, preferred_element_type=F32)
         + lax.dot_general(qr, krbuf[...], NT_DIMS, preferred_element_type=F32))
    s = s * ksbuf[...] * MLA_SCALE
    m_old = m_s[...]
    m_new = jnp.maximum(m_old, jnp.max(s, axis=-1, keepdims=True))
    alpha = jnp.exp(m_old - m_new)
    p = jnp.exp(s - m_new)
    l_new = alpha * l_s[...] + jnp.sum(p, axis=-1, keepdims=True)
    pv = jnp.dot(p.astype(BF), keys, preferred_element_type=F32)
    acc_new = alpha * acc_s[...] + pv
    m_s[...] = m_new
    l_s[...] = l_new
    acc_s[...] = acc_new

    @pl.when(c == nc - 1)
    def _():
        cn = cn_ref[0]
        krn = krn_ref[0][:, 0:ROPE_DIM]
        s_new = (jnp.sum(ql.astype(F32) * cn, axis=-1, keepdims=True)
                 + jnp.sum(qr32 * krn, axis=-1, keepdims=True)) * ksn_ref[0] * MLA_SCALE
        m2 = jnp.maximum(m_new, s_new)
        a2 = jnp.exp(m_new - m2)
        p2 = jnp.exp(s_new - m2)
        l2 = a2 * l_new + p2
        o_ref[0] = ((a2 * acc_new + p2 * cn) / l2).astype(o_ref.dtype)

    @pl.when(last)
    def _():
        wait(1 - slot)


def _mla_sample_attention(page_table, qlat3, q3, cache_latent, cache_krope, cache_kscale_t,
                          cn3, krn3, ksn3):
    nb, n_pages = page_table.shape
    pp = PAGES_PER_STEP
    nc = n_pages // pp
    any_spec = pl.BlockSpec(memory_space=pl.ANY)
    in_specs = [
        pl.BlockSpec((1, MLA_HEADS, KV_LORA), lambda b, c, pt: (b, 0, 0)),
        pl.BlockSpec((1, MLA_HEADS, HEAD_PAD), lambda b, c, pt: (b, 0, 0)),
        pl.BlockSpec((1, 1, KV_LORA), lambda b, c, pt: (b, 0, 0)),
        pl.BlockSpec((1, 1, LANES), lambda b, c, pt: (b, 0, 0)),
        pl.BlockSpec((1, MLA_HEADS, 1), lambda b, c, pt: (b, 0, 0)),
        any_spec, any_spec, any_spec,
    ]
    grid_spec = pltpu.PrefetchScalarGridSpec(
        num_scalar_prefetch=1,
        grid=(nb, nc),
        in_specs=in_specs,
        out_specs=pl.BlockSpec((1, MLA_HEADS, KV_LORA), lambda b, c, pt: (b, 0, 0)),
        scratch_shapes=[pltpu.VMEM((MLA_HEADS, 1), F32), pltpu.VMEM((MLA_HEADS, 1), F32),
                        pltpu.VMEM((MLA_HEADS, KV_LORA), F32),
                        pltpu.VMEM((pp * PAGE_SIZE, KV_LORA), BF), pltpu.VMEM((pp * PAGE_SIZE, ROPE_DIM), BF),
                        pltpu.VMEM((MLA_HEADS, pp * PAGE_SIZE), F32),
                        pltpu.VMEM((2, pp, PAGE_SIZE, KV_LORA), F32),
                        pltpu.VMEM((2, pp, PAGE_SIZE, ROPE_DIM), F32),
                        pltpu.VMEM((2, pp, MLA_HEADS, PAGE_SIZE), F32),
                        pltpu.SemaphoreType.DMA((2,))],
    )
    return pl.pallas_call(
        _paged_kernel,
        grid_spec=grid_spec,
        out_shape=jax.ShapeDtypeStruct((nb, MLA_HEADS, KV_LORA), F32),
        compiler_params=_params(("arbitrary", "arbitrary")),
        name="mla_sample_attention",
    )(page_table, qlat3, q3, cn3, krn3, ksn3, cache_latent, cache_krope, cache_kscale_t)


def _router_kernel(xp_ref, xs_ref, g_ref, w_ref, h_ref, gate_ref, info_ref, cnt_ref, cnt_s, *, nbp):
    i = pl.program_id(0)

    @pl.when(i == 0)
    def _():
        cnt_s[...] = jnp.zeros(cnt_s.shape, F32)

    xf = jnp.where(i < nbp, xp_ref[...], xs_ref[...])
    ms = jnp.mean(xf * xf, axis=-1, keepdims=True)
    h = xf * lax.rsqrt(ms + EPS) * g_ref[...]
    hb = lax.bitcast_convert_type(h.astype(BF).astype(F32), jnp.uint32)
    h_ref[...] = (hb[:, :D_MODEL // 2] >> 16) | hb[:, D_MODEL // 2:]
    logits = jnp.dot(h.astype(BF), w_ref[...].astype(BF),
                     preferred_element_type=F32)
    lane = lax.broadcasted_iota(jnp.int32, logits.shape, 1)
    big = jnp.int32(LANES)

    is_g = lane < N_GROUPS
    lg = jnp.where(is_g, logits, NEG_BIG)
    eg = jnp.where(is_g, jnp.exp(lg - jnp.max(lg, axis=-1, keepdims=True)), 0.0)
    pg = eg / jnp.sum(eg, axis=-1, keepdims=True)
    p_grp = jnp.max(pg, axis=-1, keepdims=True)
    grp = jnp.min(jnp.where(is_g & (pg == p_grp), lane, big), axis=-1, keepdims=True)

    e_idx = lane - N_GROUPS
    sel = (e_idx >= 0) & (e_idx < N_EXPERTS) & ((e_idx >> 3) == grp)
    le = jnp.where(sel, logits, NEG_BIG)
    ee = jnp.where(sel, jnp.exp(le - jnp.max(le, axis=-1, keepdims=True)), 0.0)
    pe = ee / jnp.sum(ee, axis=-1, keepdims=True)
    top1 = jnp.max(jnp.where(sel, pe, -1.0), axis=-1, keepdims=True)
    i1 = jnp.min(jnp.where(sel & (pe == top1), lane, big), axis=-1, keepdims=True)
    rest = sel & (lane != i1)
    top2 = jnp.max(jnp.where(rest, pe, -1.0), axis=-1, keepdims=True)
    i2 = jnp.min(jnp.where(rest & (pe == top2), lane, big), axis=-1, keepdims=True)
    denom = top1 + top2
    g1 = p_grp * top1 / denom
    g2 = p_grp * top2 / denom
    gate_ref[...] = jnp.where(lane == 0, g1, jnp.where(lane == 1, g2, 0.0))

    onehot = ((lane == i1) | (lane == i2)).astype(F32)
    r_i = lax.broadcasted_iota(jnp.int32, (MOE_BLOCK, MOE_BLOCK), 0)
    c_i = lax.broadcasted_iota(jnp.int32, (MOE_BLOCK, MOE_BLOCK), 1)
    tri = (c_i < r_i).astype(BF)
    before = jnp.dot(tri, onehot.astype(BF), preferred_element_type=F32) + cnt_s[...]
    rank1 = jnp.sum(jnp.where(lane == i1, before, 0.0), axis=-1, keepdims=True).astype(jnp.int32)
    rank2 = jnp.sum(jnp.where(lane == i2, before, 0.0), axis=-1, keepdims=True).astype(jnp.int32)
    info_ref[...] = jnp.where(lane == 0, i1 - N_GROUPS, jnp.where(lane == 1, i2 - N_GROUPS,
                              jnp.where(lane == 2, rank1, jnp.where(lane == 3, rank2, 0))))
    cnt_new = cnt_s[...] + jnp.sum(onehot, axis=0, keepdims=True)
    cnt_s[...] = cnt_new
    cnt_ref[...] = jnp.broadcast_to(cnt_new, cnt_ref.shape)


def _router(xp, xs, g, w_router):
    nbp = xp.shape[0] // MOE_BLOCK
    assert xs.shape[0] == MOE_BLOCK
    mt = xp.shape[0] + xs.shape[0]
    tok_spec = lambda width: pl.BlockSpec((MOE_BLOCK, width), lambda i: (i, 0))
    return pl.pallas_call(
        functools.partial(_router_kernel, nbp=nbp),
        grid=(nbp + 1,),
        in_specs=[
            pl.BlockSpec((MOE_BLOCK, D_MODEL), lambda i: (jnp.minimum(i, nbp - 1), 0)),
            pl.BlockSpec((MOE_BLOCK, D_MODEL), lambda i: (0, 0)),
            pl.BlockSpec((1, D_MODEL), lambda i: (0, 0)),
            pl.BlockSpec((D_MODEL, LANES), lambda i: (0, 0)),
        ],
        out_specs=[tok_spec(D_MODEL // 2), tok_spec(LANES), tok_spec(LANES),
                   pl.BlockSpec((SUBLANES, LANES), lambda i: (0, 0))],
        out_shape=[jax.ShapeDtypeStruct((mt, D_MODEL // 2), jnp.uint32),
                   jax.ShapeDtypeStruct((mt, LANES), F32),
                   jax.ShapeDtypeStruct((mt, LANES), jnp.int32),
                   jax.ShapeDtypeStruct((SUBLANES, LANES), F32)],
        scratch_shapes=[pltpu.VMEM((1, LANES), F32)],
        compiler_params=_params(("arbitrary",)),
        name="moe_router",
    )(xp, xs, g.reshape(1, D_MODEL), w_router)


TRASH_ROWS = 2 * MOE_BLOCK


def _expert_kernel(be_ref, nu_ref, par_ref, nxt_ref, src_ref, srcn_ref, dst_ref,
                   h_ref, wg_hbm, wu_hbm, wd_hbm, ys_ref,
                   xbuf, ybuf, wg_f, wu_f, wd_f, wg_s, wu_s, wd_s, gsem, ssem, wsem, *, layer, n_slots):
    b = pl.program_id(0)
    nu = nu_ref[0]
    rows = MOE_BLOCK

    unroll = 8

    def gather_start(tab_ref, slot):
        def body(c, carry):
            for u in range(unroll):
                t = c * unroll + u
                pltpu.make_async_copy(h_ref.at[pl.ds(tab_ref[0, 0, t], 1)], xbuf.at[slot, pl.ds(t, 1)],
                                      gsem.at[slot]).start(priority=u % 2)
            return carry
        lax.fori_loop(0, rows // unroll, body, 0)

    def weight_copies(e, slot):
        return [pltpu.make_async_copy(w.at[layer, e], f.at[slot], wsem.at[slot])
                for w, f in ((wg_hbm, wg_f), (wu_hbm, wu_f), (wd_hbm, wd_f))]

    def scatter_wait():
        pltpu.make_async_copy(ybuf, ys_ref.at[pl.ds(0, rows)], ssem).wait()

    @pl.when(b < nu)
    def _():
        slot = lax.rem(b, 2)

        @pl.when(b == 0)
        def _():
            gather_start(src_ref, 0)

        @pl.when(b + 1 < nu)
        def _():
            gather_start(srcn_ref, 1 - slot)

        e = be_ref[b]
        p = par_ref[b]
        fresh = (b == 0) | (e != be_ref[jnp.maximum(b - 1, 0)])

        @pl.when(fresh)
        def _():
            @pl.when(b == 0)
            def _():
                for c in weight_copies(e, p):
                    c.start(priority=1)

            for c in weight_copies(e, p):
                c.wait()
            nxt = nxt_ref[b]

            @pl.when(nxt >= 0)
            def _():
                for c in weight_copies(nxt, 1 - p):
                    c.start(priority=1)

            wg_s[...] = wg_f[p].astype(BF)
            wu_s[...] = wu_f[p].astype(BF)
            wd_s[...] = wd_f[p].astype(BF)

        pltpu.make_async_copy(h_ref.at[pl.ds(0, rows)], xbuf.at[slot], gsem.at[slot]).wait()
        packed = xbuf[slot]
        x = jnp.concatenate(
            [lax.bitcast_convert_type(packed << 16, F32).astype(BF),
             lax.bitcast_convert_type(packed & jnp.uint32(0xFFFF0000), F32).astype(BF)], axis=1)
        a = jnp.dot(x, wg_s[...], preferred_element_type=F32)
        u = jnp.dot(x, wu_s[...], preferred_element_type=F32)
        hmid = (a * jax.nn.sigmoid(a) * u).astype(BF)

        @pl.when(b >= 1)
        def _():
            scatter_wait()

        ybuf[...] = jnp.dot(hmid, wd_s[...], preferred_element_type=F32)

        def scatter_body(c, carry):
            for u in range(unroll):
                t = c * unroll + u
                pltpu.make_async_copy(ybuf.at[pl.ds(t, 1)], ys_ref.at[pl.ds(dst_ref[0, 0, t], 1)],
                                      ssem).start(priority=u % 2)
            return carry
        lax.fori_loop(0, rows // unroll, scatter_body, 0)

        @pl.when(b == nu - 1)
        def _():
            scatter_wait()
            ybuf[...] = jnp.zeros(ybuf.shape, F32)
            for c in range(TRASH_ROWS // rows):
                cp = pltpu.make_async_copy(ybuf, ys_ref.at[pl.ds(n_slots + c * rows, rows)], ssem)
                cp.start()
                cp.wait()


def _experts(blk_e, n_used, par, nxt, row_src3, row_dst3, h, w_gate, w_up, w_down, layer, n_slots):
    nblk = row_src3.shape[0]
    tab_spec = lambda f: pl.BlockSpec((1, 1, MOE_BLOCK), f, memory_space=pltpu.SMEM)
    any_spec = pl.BlockSpec(memory_space=pl.ANY)
    grid_spec = pltpu.PrefetchScalarGridSpec(
        num_scalar_prefetch=4,
        grid=(nblk,),
        in_specs=[
            tab_spec(lambda b, *_: (b, 0, 0)),
            tab_spec(lambda b, *_: (jnp.minimum(b + 1, nblk - 1), 0, 0)),
            tab_spec(lambda b, *_: (b, 0, 0)),
            any_spec, any_spec, any_spec, any_spec,
        ],
        out_specs=any_spec,
        scratch_shapes=[
            pltpu.VMEM((2, MOE_BLOCK, D_MODEL // 2), jnp.uint32), pltpu.VMEM((MOE_BLOCK, D_MODEL), F32),
            pltpu.VMEM((2, D_MODEL, D_EXPERT), F32), pltpu.VMEM((2, D_MODEL, D_EXPERT), F32),
            pltpu.VMEM((2, D_EXPERT, D_MODEL), F32),
            pltpu.VMEM((D_MODEL, D_EXPERT), BF), pltpu.VMEM((D_MODEL, D_EXPERT), BF),
            pltpu.VMEM((D_EXPERT, D_MODEL), BF),
            pltpu.SemaphoreType.DMA((2,)), pltpu.SemaphoreType.DMA(()), pltpu.SemaphoreType.DMA((2,)),
        ],
    )
    return pl.pallas_call(
        functools.partial(_expert_kernel, layer=layer, n_slots=n_slots),
        grid_spec=grid_spec,
        out_shape=jax.ShapeDtypeStruct((n_slots + TRASH_ROWS, D_MODEL), F32),
        compiler_params=_params(("arbitrary",)),
        name="moe_experts",
    )(blk_e, n_used, par, nxt, row_src3, row_src3, row_dst3, h, w_gate, w_up, w_down)


def _combine_kernel(gate_ref, xp_ref, xs_ref, y0_ref, y1_ref, op_ref, os_ref, *, nbp):
    i = pl.program_id(0)
    y = y0_ref[...] * gate_ref[:, 0:1] + y1_ref[...] * gate_ref[:, 1:2]

    @pl.when(i < nbp)
    def _():
        op_ref[...] = xp_ref[...] + y

    @pl.when(i == nbp)
    def _():
        os_ref[...] = xs_ref[...] + y


def _combine(gate, xp, xs, ys):
    nbp = xp.shape[0] // MOE_BLOCK
    nb_tok = nbp + 1
    p_spec = pl.BlockSpec((MOE_BLOCK, D_MODEL), lambda i: (jnp.minimum(i, nbp - 1), 0))
    s_spec = pl.BlockSpec((MOE_BLOCK, D_MODEL), lambda i: (0, 0))
    return pl.pallas_call(
        functools.partial(_combine_kernel, nbp=nbp),
        grid=(nb_tok,),
        in_specs=[
            pl.BlockSpec((MOE_BLOCK, LANES), lambda i: (i, 0)),
            p_spec, s_spec,
            pl.BlockSpec((MOE_BLOCK, D_MODEL), lambda i: (i, 0)),
            pl.BlockSpec((MOE_BLOCK, D_MODEL), lambda i: (nb_tok + i, 0)),
        ],
        out_specs=[p_spec, s_spec],
        out_shape=[jax.ShapeDtypeStruct(xp.shape, F32), jax.ShapeDtypeStruct(xs.shape, F32)],
        compiler_params=_params(("arbitrary",)),
        name="moe_combine",
    )(gate, xp, xs, ys, ys)


def _moe(xp, xs, g_ffn_l, w_router, w_gate, w_up, w_down, layer):
    mt = xp.shape[0] + xs.shape[0]
    n_slots = 2 * mt
    nblk = (n_slots + N_EXPERTS * (MOE_BLOCK - 1) + MOE_BLOCK - 1) // MOE_BLOCK
    h, gate, info, cnt = _router(xp, xs, g_ffn_l, w_router)

    ids = jnp.arange(N_EXPERTS, dtype=jnp.int32)
    counts = cnt[0, N_GROUPS:N_GROUPS + N_EXPERTS].astype(jnp.int32)
    used = counts > 0
    padded = (counts + MOE_BLOCK - 1) // MOE_BLOCK * MOE_BLOCK
    pad_ends = jnp.cumsum(padded)
    pad_starts = pad_ends - padded
    n_used = (pad_ends[-1] // MOE_BLOCK).astype(jnp.int32)
    blk_ids = jnp.arange(nblk, dtype=jnp.int32)
    blk_e = jnp.sum((pad_ends[None, :] <= blk_ids[:, None] * MOE_BLOCK).astype(jnp.int32), axis=1)
    last_e = jnp.max(jnp.where(used, ids, 0))
    blk_e = jnp.where(blk_ids < n_used, jnp.minimum(blk_e, N_EXPERTS - 1), last_e).astype(jnp.int32)
    ordinal = jnp.cumsum(used.astype(jnp.int32)) - 1
    next_e = jnp.min(jnp.where((ids[None, :] > ids[:, None]) & used[None, :], ids[None, :], N_EXPERTS), axis=1)
    next_e = jnp.where(next_e == N_EXPERTS, -1, next_e)
    blk_onehot = blk_e[:, None] == ids[None, :]
    par = jnp.sum(jnp.where(blk_onehot, (ordinal % 2)[None, :], 0), axis=1).astype(jnp.int32)
    nxt = jnp.sum(jnp.where(blk_onehot, next_e[None, :], 0), axis=1).astype(jnp.int32)

    e2, rank2 = info[:, 0:2], info[:, 2:4]
    start2 = jnp.sum(jnp.where(e2[:, :, None] == ids[None, None, :], pad_starts[None, None, :], 0), axis=-1)
    dest = start2 + rank2
    slot_code = jnp.arange(mt, dtype=jnp.int32)[:, None] + jnp.array([[0, mt]], jnp.int32)
    rows = jnp.arange(nblk * MOE_BLOCK, dtype=jnp.int32)
    row_dst = (n_slots + rows % TRASH_ROWS).at[dest.reshape(-1)].set(slot_code.reshape(-1))
    row_src = jnp.where(row_dst < n_slots, row_dst % mt, 0)
    shape3 = (nblk, 1, MOE_BLOCK)

    ys = _experts(blk_e, n_used.reshape(1), par, nxt, row_src.reshape(shape3), row_dst.reshape(shape3),
                  h, w_gate, w_up, w_down, layer, n_slots)
    return _combine(gate, xp, xs, ys)


def _rope_tables(pos):
    half = ROPE_DIM // 2
    inv = ROPE_THETA ** (-jnp.arange(half, dtype=F32) / half)
    ang = pos.astype(F32)[:, None] * inv[None, :]
    cos, sin = jnp.cos(ang), jnp.sin(ang)
    z = jnp.zeros_like(cos)
    a = jnp.concatenate([cos, cos, z, z], axis=1)
    b = jnp.concatenate([-sin, sin, z, z], axis=1)
    return a, b


def _out_spec(tm, tn):
    return pl.BlockSpec((tm, tn), lambda i, j: (i, j))


def _table_specs(tabs, tm, nper):
    return [(t, pl.BlockSpec((tm, LANES), lambda i, j, nper=nper: (i % nper, 0))) for t in tabs]


def _vec_extra(v):
    n = v.shape[-1]
    return (v.reshape(1, n), pl.BlockSpec((1, n), lambda i, j: (0, 0)))


def kernel(x_prompt, x_sample, cache_latent, cache_krope, cache_kscale, cache_mem_k, cache_mem_v, state_conv, page_table, mem_prompt, g_mix, g_ffn, g_mem, w_mem_kv, g_mq, g_mk, w_in_a, conv_w, conv_b, ln_g, ln_b, w_out_a, g_kv_in, w_kv_down, g_kv_lat, w_uk, w_uv, g_k, w_in_b, g_qlat, w_q_up, g_q, w_out_b, w_rg, w_re, w_gate, w_up, w_down):
    batch, seq, _ = x_prompt.shape
    nb = x_sample.shape[0]
    depth = g_mix.shape[0]
    n_a = w_in_a.shape[0]
    mp = batch * seq
    tmp = 1024
    xp = x_prompt.reshape(mp, D_MODEL)
    xs = x_sample.reshape(nb, D_MODEL)

    tabs_p = _rope_tables(jnp.arange(seq))
    tabs_s = _rope_tables(jnp.full((nb,), PAST_LEN))
    w_kvd = jnp.concatenate([w_kv_down, w_kv_down[:, KV_LORA:]], axis=1)
    gk_n = g_k[:NOPE_DIM]
    gk_r = jnp.concatenate([g_k[NOPE_DIM:], g_k[NOPE_DIM:]])
    real_lanes = jnp.concatenate([jnp.ones((ROPE_DIM,), F32), jnp.zeros((LANES - ROPE_DIM,), F32)])
    w_uk2 = w_uk.reshape(KV_LORA, MLA_HEADS * NOPE_DIM)
    w_uk_t = jnp.transpose(w_uk, (1, 2, 0))
    w_uv2 = w_uv.reshape(KV_LORA, MLA_HEADS * V_DIM)
    w_qu = jnp.concatenate([w_q_up, w_q_up[..., NOPE_DIM:]], axis=-1)
    w_qu = w_qu.reshape(w_q_up.shape[0], Q_LORA, MLA_HEADS * HEAD_PAD)
    gq_pad = jnp.concatenate([g_q, g_q[:, NOPE_DIM:]], axis=-1)
    w_router = jnp.pad(jnp.concatenate([w_rg, w_re], axis=-1),
                       ((0, 0), (0, 0), (0, LANES - N_GROUPS - N_EXPERTS)))
    conv_w32 = jnp.pad(conv_w, ((0, 0), (0, 32 - CONV_W), (0, 0)))
    cache_kscale_t = _kscale_transpose(cache_kscale)

    memk, memv = _memory_kv_all(mem_prompt.reshape(batch * N_MEM, D_MODEL), g_mem, w_mem_kv, g_mk)

    conv_p_list, conv_s_list = [], []
    shared = None
    for l in range(depth):
        if l < n_a:
            n_glu = D_MODEL
            outs_p, outs_s = [], []
            for x, m, tm, dst in ((xp, mp, tmp, outs_p), (xs, nb, nb, outs_s)):
                tn = 256
                ng = n_glu // tn
                glu_col = lambda j, ng=ng: jnp.minimum(j, ng - 1)
                a, qm = _fused_mm(
                    "in_a", [(x, D_MODEL, 0)],
                    [(0, w_in_a, l, D_MODEL, tn, 0, glu_col),
                     (0, w_in_a, l, D_MODEL, tn, 0, lambda j, ng=ng: ng + jnp.minimum(j, ng - 1)),
                     (0, w_in_a, l, D_MODEL, tn, 0, lambda j, ng=ng: 2 * ng + jnp.maximum(j - ng, 0))],
                    m=m, tm=tm, nj=ng + MEM_W // tn, gain=g_mix[l],
                    outs=[(jax.ShapeDtypeStruct((m, n_glu), F32),
                           pl.BlockSpec((tm, tn), lambda i, j, ng=ng: (i, jnp.minimum(j, ng - 1)))),
                          (jax.ShapeDtypeStruct((m, MEM_W), F32),
                           pl.BlockSpec((tm, tn), lambda i, j, ng=ng: (i, jnp.maximum(j - ng, 0))))],
                    epi=functools.partial(_epi_glu_memq, ng=ng))
                dst.extend([a, qm])
            a_p, qm_p = outs_p
            a_s, qm_s = outs_s
            yc_p = _conv_prompt(a_p, conv_w32[l], conv_b[l], ln_g[l], ln_b[l], batch)
            conv_p_list.append(a_p.reshape(batch, seq, D_MODEL)[:, seq - (CONV_W - 1):])
            yc_s, ns = _conv_sample(state_conv[l], a_s, conv_w32[l], conv_b[l], ln_g[l], ln_b[l])
            conv_s_list.append(ns)
            ym_p = _memattn_prompt(qm_p, 0, memk, memv, l, g_mq[l], batch)
            ym_s = _memattn_sample(qm_s.reshape(nb, MEM_HEADS, MEM_HEAD_DIM), cache_mem_k, cache_mem_v,
                                   l, g_mq[l])
            new = []
            for x, m, tm, y1, y2 in ((xp, mp, tmp, yc_p, ym_p), (xs, nb, nb, yc_s, ym_s)):
                tn = 512
                new.append(_fused_mm(
                    "out_a", [(y1, D_MODEL, 0), (y2, MEM_W, 0)],
                    [(0, w_out_a, l, D_MODEL, tn, 0, 0), (1, w_out_a, l, MEM_W, tn, D_MODEL // MEM_W, 0)],
                    m=m, tm=tm, nj=D_MODEL // tn,
                    extras=[(x, _out_spec(tm, tn))],
                    outs=[(jax.ShapeDtypeStruct((m, D_MODEL), F32), _out_spec(tm, tn))],
                    epi=_epi_residual)[0])
            xp, xs = new
        else:
            jb = l - n_a
            kfull, vfull, c_s, kr_s, ks_s = shared
            att = []
            for x, m, tm, tabs, nper, qdt in ((xp, mp, tmp, tabs_p, seq // tmp, BF),
                                              (xs, nb, nb, tabs_s, 1, F32)):
                tn = Q_LORA
                n_in = Q_LORA + MEM_W
                u, qm = _fused_mm(
                    "in_b", [(x, D_MODEL, 0)], [(0, w_in_b, jb, D_MODEL, tn, 0, 0)],
                    m=m, tm=tm, nj=n_in // tn, gain=g_mix[l],
                    outs=[(jax.ShapeDtypeStruct((m, Q_LORA), F32),
                           pl.BlockSpec((tm, tn), lambda i, j: (i, 0))),
                          (jax.ShapeDtypeStruct((m, MEM_W), F32),
                           pl.BlockSpec((tm, tn), lambda i, j: (i, jnp.maximum(j - 1, 0))))],
                    epi=_epi_split)
                qh = _fused_mm(
                    "q_up", [(u, Q_LORA, 0)], [(0, w_qu, jb, Q_LORA, HEAD_PAD, 0, 0)],
                    m=m, tm=tm, nj=MLA_HEADS, gain=g_qlat[jb],
                    extras=[_vec_extra(gq_pad[jb]), _vec_extra(real_lanes)] + _table_specs(tabs, tm, nper),
                    outs=[(jax.ShapeDtypeStruct((m, MLA_HEADS * HEAD_PAD), qdt), _out_spec(tm, HEAD_PAD))],
                    epi=_epi_qhead)[0]
                att.append((qm, qh))
            (qm_p, qh_p), (qm_s, qh_s) = att
            att_p = _mla_prompt_attention(qh_p, kfull, vfull, batch)
            ym_p = _memattn_prompt(qm_p, 0, memk, memv, l, g_mq[l], batch)
            ym_s = _memattn_sample(qm_s.reshape(nb, MEM_HEADS, MEM_HEAD_DIM), cache_mem_k, cache_mem_v,
                                   l, g_mq[l])

            qlat = _headwise_mm(
                "q_latent", qh_s, pl.BlockSpec((nb, NOPE_DIM), lambda h: (0, 2 * h)),
                w_uk_t, pl.BlockSpec((None, NOPE_DIM, KV_LORA), lambda h: (h, 0, 0)),
                gk_n.reshape(1, NOPE_DIM),
                jax.ShapeDtypeStruct((nb, MLA_HEADS * KV_LORA), BF),
                pl.BlockSpec((nb, KV_LORA), lambda h: (0, h)))
            o_lat = _mla_sample_attention(
                page_table, qlat.reshape(nb, MLA_HEADS, KV_LORA),
                qh_s.reshape(nb, MLA_HEADS, HEAD_PAD), cache_latent, cache_krope, cache_kscale_t,
                c_s.reshape(nb, 1, KV_LORA), kr_s.reshape(nb, 1, LANES),
                ks_s[:, :MLA_HEADS].reshape(nb, MLA_HEADS, 1))
            att_s = _headwise_mm(
                "v_expand", o_lat.reshape(nb, MLA_HEADS * KV_LORA),
                pl.BlockSpec((nb, KV_LORA), lambda h: (0, h)),
                w_uv2, pl.BlockSpec((KV_LORA, V_DIM), lambda h: (0, h)),
                jnp.ones((1, KV_LORA), F32),
                jax.ShapeDtypeStruct((nb, MLA_HEADS * V_DIM), BF),
                pl.BlockSpec((nb, V_DIM), lambda h: (0, h)))
            new = []
            n_att = MLA_HEADS * V_DIM
            for x, m, tm, y1, y2 in ((xp, mp, tmp, att_p, ym_p), (xs, nb, nb, att_s, ym_s)):
                tn = 512
                new.append(_fused_mm(
                    "out_b", [(y1, n_att, 0), (y2, MEM_W, 0)],
                    [(0, w_out_b, jb, n_att, tn, 0, 0), (1, w_out_b, jb, MEM_W, tn, n_att // MEM_W, 0)],
                    m=m, tm=tm, nj=D_MODEL // tn,
                    extras=[(x, _out_spec(tm, tn))],
                    outs=[(jax.ShapeDtypeStruct((m, D_MODEL), F32), _out_spec(tm, tn))],
                    epi=_epi_residual)[0])
            xp, xs = new

        xp, xs = _moe(xp, xs, g_ffn[l], w_router[l], w_gate, w_up, w_down, l)

        if l == n_a - 1:
            lat = []
            for x, m, tm, tabs, nper in ((xp, mp, tmp, tabs_p, seq // tmp), (xs, nb, nb, tabs_s, 1)):
                nck = KV_LORA + LANES
                c, kr128, ssq128 = _fused_mm(
                    "kv_latent", [(x, D_MODEL, 0)], [(0, w_kvd, None, D_MODEL, nck, 0, 0)],
                    m=m, tm=tm, nj=1, gain=g_kv_in,
                    extras=[_vec_extra(g_kv_lat), _vec_extra(gk_r), _vec_extra(real_lanes)]
                    + _table_specs(tabs, tm, nper),
                    outs=[(jax.ShapeDtypeStruct((m, KV_LORA), F32), _out_spec(tm, KV_LORA)),
                          (jax.ShapeDtypeStruct((m, LANES), F32), pl.BlockSpec((tm, LANES), lambda i, j: (i, 0))),
                          (jax.ShapeDtypeStruct((m, LANES), F32), pl.BlockSpec((tm, LANES), lambda i, j: (i, 0)))],
                    epi=_epi_latent)
                kf, ks128 = _fused_mm(
                    "k_nope", [(c, KV_LORA, 0)], [(0, w_uk2, None, KV_LORA, 2 * NOPE_DIM, 0, 0)],
                    m=m, tm=tm, nj=MLA_HEADS // 2,
                    extras=[(kr128, pl.BlockSpec((tm, LANES), lambda i, j: (i, 0))),
                            (ssq128, pl.BlockSpec((tm, LANES), lambda i, j: (i, 0))),
                            _vec_extra(gk_n)],
                    outs=[(jax.ShapeDtypeStruct((m, MLA_HEADS * HEAD_PAD), BF), _out_spec(tm, 2 * HEAD_PAD)),
                          (jax.ShapeDtypeStruct((m, LANES), F32), pl.BlockSpec((tm, LANES), lambda i, j: (i, 0)))],
                    epi=_epi_knope)
                lat.append((c, kr128, ks128, kf))
            (c_p, kr_p, ks_p, kfull), (c_s, kr_s, ks_s, _) = lat
            vfull = _fused_mm(
                "v_full", [(c_p, KV_LORA, 0)], [(0, w_uv2, None, KV_LORA, 512, 0, 0)],
                m=mp, tm=tmp, nj=MLA_HEADS * V_DIM // 512,
                outs=[(jax.ShapeDtypeStruct((mp, MLA_HEADS * V_DIM), BF), _out_spec(tmp, 512))],
                epi=_epi_plain)[0]
            shared = (kfull, vfull, c_s, kr_s, ks_s)

    shape5 = (depth, batch, N_MEM, MEM_HEADS, MEM_HEAD_DIM)
    return (xp.reshape(batch, seq, D_MODEL), xs.reshape(nb, 1, D_MODEL),
            c_p.reshape(batch, seq, KV_LORA), kr_p[:, :ROPE_DIM].reshape(batch, seq, ROPE_DIM),
            ks_p[:, :MLA_HEADS].reshape(batch, seq, MLA_HEADS),
            memk.reshape(shape5), memv.reshape(shape5),
            jnp.stack(conv_p_list, axis=0),
            c_s.reshape(nb, 1, KV_LORA), kr_s[:, :ROPE_DIM].reshape(nb, 1, ROPE_DIM),
            ks_s[:, :MLA_HEADS].reshape(nb, 1, MLA_HEADS),
            jnp.stack(conv_s_list, axis=0))
```

```python
import functools
import math

import jax
import jax.numpy as jnp
from jax import lax
from jax.experimental import pallas as pl
from jax.experimental.pallas import tpu as pltpu

F32 = jnp.float32
BF = jnp.bfloat16

D_MODEL = 2048
SEQ = 2048
PAST_LEN = 8192
PAGE_SIZE = 128
CONV_W = 31
N_MEM = 256
MEM_HEADS = 4
MEM_HEAD_DIM = 256
MEM_W = MEM_HEADS * MEM_HEAD_DIM
MLA_HEADS = 16
Q_LORA = 512
KV_LORA = 512
NOPE_DIM = 128
ROPE_DIM = 64
QK_DIM = NOPE_DIM + ROPE_DIM
V_DIM = 128
ROPE_THETA = 10000.0
N_GROUPS = 8
EXPERTS_PER_GROUP = 8
N_EXPERTS = 64
D_EXPERT = 512
MOE_BLOCK = 128
EPS = 1e-6

HEAD_PAD = 256
LANES = 128
VMEM_LIMIT_BYTES = 56 * 1024 * 1024
NEG_BIG = -1e30
NT_DIMS = (((1,), (1,)), ((), ()))


def _params(sem):
    return pltpu.CompilerParams(dimension_semantics=sem, vmem_limit_bytes=VMEM_LIMIT_BYTES)


def _mm_kernel(*refs, nx, w_x, has_gain, n_extra, n_out, epi, tm):
    x_refs = refs[:nx]
    w_refs = refs[nx:nx + len(w_x)]
    pos = nx + len(w_x)
    g_ref = refs[pos] if has_gain else None
    pos += int(has_gain)
    extra_refs = refs[pos:pos + n_extra]
    pos += n_extra
    out_refs = refs[pos:pos + n_out]
    pos += n_out
    h_s = refs[pos] if has_gain else None
    j = pl.program_id(1)

    if has_gain:
        chunk = min(tm, 128)

        @pl.when(j == 0)
        def _():
            def body(c, carry):
                r = pl.multiple_of(c * chunk, chunk)
                xf = x_refs[0][pl.ds(r, chunk), :].astype(F32)
                ms = jnp.mean(xf * xf, axis=-1, keepdims=True)
                h_s[pl.ds(r, chunk), :] = (xf * lax.rsqrt(ms + EPS) * g_ref[...]).astype(BF)
                return carry
            lax.fori_loop(0, tm // chunk, body, 0)

    def product(wi):
        xi = w_x[wi]
        lhs = h_s[...] if has_gain and xi == 0 else x_refs[xi][...].astype(BF)
        return jnp.dot(lhs, w_refs[wi][...].astype(BF), preferred_element_type=F32)

    epi(_Products(product, len(w_x)), extra_refs, out_refs, j)


class _Products:
    def __init__(self, fn, n):
        self._fn, self._n = fn, n

    def __getitem__(self, i):
        if isinstance(i, slice):
            return [self._fn(k) for k in range(*i.indices(self._n))]
        return self._fn(i)


def _fused_mm(name, xs, ws, *, m, tm, nj, gain=None, extras=(), outs, epi):
    in_arrays, in_specs = [], []
    for arr, k, cb in xs:
        in_arrays.append(arr)
        in_specs.append(pl.BlockSpec((tm, k), lambda i, j, cb=cb: (i, cb)))
    for xi, arr, layer, k, tn, rb, cb0 in ws:
        in_arrays.append(arr)
        col = cb0 if callable(cb0) else (lambda j, cb0=cb0: cb0 + j)
        if layer is None:
            in_specs.append(pl.BlockSpec((k, tn), lambda i, j, rb=rb, col=col: (rb, col(j))))
        else:
            in_specs.append(pl.BlockSpec((None, k, tn),
                                         lambda i, j, l=layer, rb=rb, col=col: (l, rb, col(j))))
    scratch = []
    if gain is not None:
        k0 = xs[0][1]
        in_arrays.append(gain.reshape(1, k0).astype(F32))
        in_specs.append(pl.BlockSpec((1, k0), lambda i, j: (0, 0)))
        scratch.append(pltpu.VMEM((tm, k0), BF))
    for arr, spec in extras:
        in_arrays.append(arr)
        in_specs.append(spec)
    kern = functools.partial(_mm_kernel, nx=len(xs), w_x=tuple(w[0] for w in ws),
                             has_gain=gain is not None, n_extra=len(extras),
                             n_out=len(outs), epi=epi, tm=tm)
    res = pl.pallas_call(
        kern,
        grid=(m // tm, nj),
        in_specs=in_specs,
        out_specs=[o[1] for o in outs],
        out_shape=[o[0] for o in outs],
        scratch_shapes=scratch,
        compiler_params=_params(("arbitrary", "arbitrary")),
        name=name,
    )(*in_arrays)
    return res


def _epi_plain(ds, ex, outs, j):
    acc = ds[0]
    for d in ds[1:]:
        acc = acc + d
    outs[0][...] = acc.astype(outs[0].dtype)


def _epi_residual(ds, ex, outs, j):
    acc = ds[0]
    for d in ds[1:]:
        acc = acc + d
    outs[0][...] = (ex[0][...] + acc).astype(outs[0].dtype)


def _epi_split(ds, ex, outs, j):
    @pl.when(j == 0)
    def _():
        outs[0][...] = ds[0]

    @pl.when(j > 0)
    def _():
        outs[1][...] = ds[0]


def _epi_glu_memq(ds, ex, outs, j, *, ng):
    @pl.when(j < ng)
    def _():
        outs[0][...] = (ds[0] * jax.nn.sigmoid(ds[1])).astype(outs[0].dtype)

    @pl.when(j >= ng)
    def _():
        outs[1][...] = ds[2].astype(outs[1].dtype)


def _rope_tile(t, a_ref, b_ref):
    return t * a_ref[...] + pltpu.roll(t, 32, 1) * b_ref[...]


def _epi_qhead(ds, ex, outs, j):
    acc = ds[0]
    gq_ref, real_ref, a_ref, b_ref = ex
    lo, hi = acc[:, :LANES], acc[:, LANES:]
    sq = lo * lo + hi * hi * real_ref[...]
    sq_hi = sq.astype(BF)
    sq_lo = (sq - sq_hi.astype(F32)).astype(BF)
    ones = jnp.ones((LANES, LANES), BF)
    ssq = (jnp.dot(sq_hi, ones, preferred_element_type=F32)
           + jnp.dot(sq_lo, ones, preferred_element_type=F32))
    inv = lax.rsqrt(ssq * (1.0 / QK_DIM) + EPS)
    outs[0][:, :LANES] = (lo * inv * gq_ref[:, :LANES]).astype(outs[0].dtype)
    outs[0][:, LANES:] = _rope_tile(hi * inv * gq_ref[:, LANES:], a_ref, b_ref).astype(outs[0].dtype)


def _epi_latent(ds, ex, outs, j):
    ck = ds[0]
    glat_ref, gkr_ref, real_ref, a_ref, b_ref = ex
    c_raw = ck[:, :KV_LORA]
    ms = jnp.mean(c_raw * c_raw, axis=-1, keepdims=True)
    outs[0][...] = c_raw * lax.rsqrt(ms + EPS) * glat_ref[...]
    krt = ck[:, KV_LORA:]
    ssq = jnp.sum(krt * krt * real_ref[...], axis=-1, keepdims=True)
    outs[1][...] = _rope_tile(krt * gkr_ref[...], a_ref, b_ref)
    outs[2][...] = jnp.broadcast_to(ssq, outs[2].shape)


def _epi_knope(ds, ex, outs, j):
    acc = ds[0]
    kr_ref, ssq_ref, gkn_ref = ex
    ssq_r = ssq_ref[:, 0:1]
    kr = kr_ref[...]
    lane = lax.broadcasted_iota(jnp.int32, outs[1].shape, 1)

    @pl.when(j == 0)
    def _():
        outs[1][...] = jnp.zeros(outs[1].shape, F32)

    ks_all = outs[1][...]
    for t in range(2):
        kn = acc[:, t * NOPE_DIM:(t + 1) * NOPE_DIM]
        ssq = jnp.sum(kn * kn, axis=-1, keepdims=True) + ssq_r
        ks = lax.rsqrt(ssq * (1.0 / QK_DIM) + EPS)
        outs[0][:, t * HEAD_PAD:t * HEAD_PAD + LANES] = (kn * gkn_ref[...] * ks).astype(outs[0].dtype)
        outs[0][:, t * HEAD_PAD + LANES:(t + 1) * HEAD_PAD] = (kr * ks).astype(outs[0].dtype)
        ks_all = jnp.where(lane == 2 * j + t, ks, ks_all)
    outs[1][...] = ks_all


def _memkv_kernel(x_ref, g_ref, wk_ref, wv_ref, gk_ref, k_out, v_out, h_s):
    j = pl.program_id(1)
    rows = x_ref.shape[0]
    chunk = 128

    @pl.when(j == 0)
    def _():
        def body(c, carry):
            r = pl.multiple_of(c * chunk, chunk)
            xf = x_ref[pl.ds(r, chunk), :]
            ms = jnp.mean(xf * xf, axis=-1, keepdims=True)
            h_s[pl.ds(r, chunk), :] = (xf * lax.rsqrt(ms + EPS) * g_ref[...]).astype(BF)
            return carry
        lax.fori_loop(0, rows // chunk, body, 0)

    h = h_s[...]
    k = jnp.dot(h, wk_ref[...].astype(BF), preferred_element_type=F32)
    ms = jnp.mean(k * k, axis=-1, keepdims=True)
    k_out[...] = k * lax.rsqrt(ms + EPS) * gk_ref[...]
    v_out[...] = jnp.dot(h, wv_ref[...].astype(BF), preferred_element_type=F32)


def _memory_kv_all(mem2d, g_mem, w_mem_kv, g_mk):
    depth = w_mem_kv.shape[0]
    rows = mem2d.shape[0]
    hd = MEM_HEAD_DIM
    out = pl.pallas_call(
        _memkv_kernel,
        grid=(depth, MEM_HEADS),
        in_specs=[
            pl.BlockSpec((rows, D_MODEL), lambda l, j: (0, 0)),
            pl.BlockSpec((None, 1, D_MODEL), lambda l, j: (l, 0, 0)),
            pl.BlockSpec((None, D_MODEL, hd), lambda l, j: (l, 0, j)),
            pl.BlockSpec((None, D_MODEL, hd), lambda l, j: (l, 0, MEM_HEADS + j)),
            pl.BlockSpec((None, 1, hd), lambda l, j: (l, 0, 0)),
        ],
        out_specs=[
            pl.BlockSpec((None, rows, hd), lambda l, j: (l, 0, j)),
            pl.BlockSpec((None, rows, hd), lambda l, j: (l, 0, j)),
        ],
        out_shape=[jax.ShapeDtypeStruct((depth, rows, MEM_W), F32)] * 2,
        scratch_shapes=[pltpu.VMEM((rows, D_MODEL), BF)],
        compiler_params=_params(("arbitrary", "arbitrary")),
        name="memory_kv",
    )(mem2d, g_mem.reshape(depth, 1, D_MODEL), w_mem_kv, w_mem_kv, g_mk.reshape(depth, 1, hd))
    return out


CONV_TT = 256
CONV_RC = 32
CONV_LW = 256
CONV_NORM_RC = 64
CONV_HALO = 32
SUBLANES = 8


def _conv_prompt_kernel(a_ref, w_ref, cb_ref, lg_ref, lb_ref, o_ref, sh, ybuf):
    t = pl.program_id(1)
    tt = CONV_TT
    span = tt + CONV_HALO

    @pl.when(t == 0)
    def _():
        sh[0, 0:CONV_HALO, :] = jnp.zeros((CONV_HALO, D_MODEL), F32)
        sh[0, span:span + SUBLANES, :] = jnp.zeros((SUBLANES, D_MODEL), F32)

    sh[0, CONV_HALO:span, :] = a_ref[...]
    for p in range(1, SUBLANES):
        sh[p, 0:span, :] = sh[0, p:p + span, :]

    groups = CONV_RC // SUBLANES

    def conv_body(c, carry):
        r = c * CONV_RC
        for lq in range(D_MODEL // CONV_LW):
            ls = slice(lq * CONV_LW, (lq + 1) * CONV_LW)
            accs = [jnp.zeros((SUBLANES, CONV_LW), F32) for _ in range(groups)]
            for p in range(SUBLANES):
                slabs = {}
                for m in range(CONV_W // SUBLANES + 2):
                    k = p - 2 + SUBLANES * m
                    if not 0 <= k < CONV_W:
                        continue
                    w = w_ref[k * SUBLANES:(k + 1) * SUBLANES, ls]
                    for g in range(groups):
                        j = m + g
                        if j not in slabs:
                            start = pl.multiple_of(r + SUBLANES * j, SUBLANES)
                            slabs[j] = sh[p, pl.ds(start, SUBLANES), ls]
                        accs[g] = accs[g] + slabs[j] * w
            ybuf[pl.ds(pl.multiple_of(r, CONV_RC), CONV_RC), ls] = jnp.concatenate(accs, axis=0)
        return carry

    lax.fori_loop(0, tt // CONV_RC, conv_body, 0)

    def norm_body(c, carry):
        rows = pl.ds(pl.multiple_of(c * CONV_NORM_RC, CONV_NORM_RC), CONV_NORM_RC)
        y = ybuf[rows, :] + cb_ref[...]
        mu = jnp.mean(y, axis=-1, keepdims=True)
        yc = y - mu
        var = jnp.mean(yc * yc, axis=-1, keepdims=True)
        z = yc * lax.rsqrt(var + EPS) * lg_ref[...] + lb_ref[...]
        o_ref[rows, :] = (z * jax.nn.sigmoid(z)).astype(o_ref.dtype)
        return carry

    lax.fori_loop(0, tt // CONV_NORM_RC, norm_body, 0)
    sh[0, 0:CONV_HALO, :] = sh[0, tt:span, :]


def _conv_prompt(a, w32, cb, lg, lb, batch):
    m = a.shape[0]
    nt = SEQ // CONV_TT
    vec = lambda v: v.reshape(1, D_MODEL)
    w_rep = jnp.repeat(w32, SUBLANES, axis=0)
    return pl.pallas_call(
        _conv_prompt_kernel,
        grid=(batch, nt),
        in_specs=[
            pl.BlockSpec((CONV_TT, D_MODEL), lambda b, t: (b * nt + t, 0)),
            pl.BlockSpec((32 * SUBLANES, D_MODEL), lambda b, t: (0, 0)),
            pl.BlockSpec((1, D_MODEL), lambda b, t: (0, 0)),
            pl.BlockSpec((1, D_MODEL), lambda b, t: (0, 0)),
            pl.BlockSpec((1, D_MODEL), lambda b, t: (0, 0)),
        ],
        out_specs=pl.BlockSpec((CONV_TT, D_MODEL), lambda b, t: (b * nt + t, 0)),
        out_shape=jax.ShapeDtypeStruct((m, D_MODEL), BF),
        scratch_shapes=[pltpu.VMEM((SUBLANES, CONV_TT + CONV_HALO + SUBLANES, D_MODEL), F32),
                        pltpu.VMEM((CONV_TT, D_MODEL), F32)],
        compiler_params=_params(("arbitrary", "arbitrary")),
        name="conv_prompt",
    )(a, w_rep, vec(cb), vec(lg), vec(lb))


CONV_SB = 16


def _conv_sample_kernel(st_ref, a_ref, w_ref, cb_ref, lg_ref, lb_ref, y_ref, ns_ref):
    nstate = CONV_W - 1
    a = a_ref[...]
    acc = a * w_ref[nstate:nstate + 1, :]
    for k in range(nstate):
        acc = acc + st_ref[k] * w_ref[k:k + 1, :]
    y = acc + cb_ref[...]
    mu = jnp.mean(y, axis=-1, keepdims=True)
    yc = y - mu
    var = jnp.mean(yc * yc, axis=-1, keepdims=True)
    z = yc * lax.rsqrt(var + EPS) * lg_ref[...] + lb_ref[...]
    y_ref[...] = (z * jax.nn.sigmoid(z)).astype(y_ref.dtype)
    for k in range(nstate - 1):
        ns_ref[k] = st_ref[k + 1]
    ns_ref[nstate - 1] = a


def _conv_sample(state_t, layer, a, w32, cb, lg, lb):
    nstate, nb = state_t.shape[1], state_t.shape[2]
    vec = lambda v: v.reshape(1, D_MODEL)
    return pl.pallas_call(
        _conv_sample_kernel,
        grid=(nb // CONV_SB,),
        in_specs=[
            pl.BlockSpec((None, nstate, CONV_SB, D_MODEL), lambda i: (layer, 0, i, 0)),
            pl.BlockSpec((CONV_SB, D_MODEL), lambda i: (i, 0)),
            pl.BlockSpec((32, D_MODEL), lambda i: (0, 0)),
            pl.BlockSpec((1, D_MODEL), lambda i: (0, 0)),
            pl.BlockSpec((1, D_MODEL), lambda i: (0, 0)),
            pl.BlockSpec((1, D_MODEL), lambda i: (0, 0)),
        ],
        out_specs=[
            pl.BlockSpec((CONV_SB, D_MODEL), lambda i: (i, 0)),
            pl.BlockSpec((nstate, CONV_SB, D_MODEL), lambda i: (0, i, 0)),
        ],
        out_shape=[jax.ShapeDtypeStruct((nb, D_MODEL), BF),
                   jax.ShapeDtypeStruct((nstate, nb, D_MODEL), F32)],
        compiler_params=_params(("arbitrary",)),
        name="conv_sample",
    )(state_t, a, w32, vec(cb), vec(lg), vec(lb))


MEM_TQ = 512
MEM_SCALE = MEM_HEAD_DIM ** -0.5


def _memattn_prompt_kernel(q_ref, k_ref, v_ref, g_ref, o_ref):
    for h in range(MEM_HEADS):
        sl = slice(h * MEM_HEAD_DIM, (h + 1) * MEM_HEAD_DIM)
        qh = q_ref[:, sl]
        ms = jnp.mean(qh * qh, axis=-1, keepdims=True)
        qn = (qh * lax.rsqrt(ms + EPS) * g_ref[...]).astype(BF)
        s = lax.dot_general(qn, k_ref[:, sl].astype(BF), NT_DIMS,
                            preferred_element_type=F32) * MEM_SCALE
        mx = jnp.max(s, axis=-1, keepdims=True)
        p = jnp.exp(s - mx)
        l = jnp.sum(p, axis=-1, keepdims=True)
        o = jnp.dot(p.astype(BF), v_ref[:, sl].astype(BF), preferred_element_type=F32)
        o_ref[:, sl] = (o / l).astype(o_ref.dtype)


def _memattn_prompt(q, q_colblk, memk, memv, layer, g_mq_l, batch):
    m = q.shape[0]
    nq = SEQ // MEM_TQ
    return pl.pallas_call(
        _memattn_prompt_kernel,
        grid=(batch, nq),
        in_specs=[
            pl.BlockSpec((MEM_TQ, MEM_W), lambda b, i: (b * nq + i, q_colblk)),
            pl.BlockSpec((None, N_MEM, MEM_W), lambda b, i: (layer, b, 0)),
            pl.BlockSpec((None, N_MEM, MEM_W), lambda b, i: (layer, b, 0)),
            pl.BlockSpec((1, MEM_HEAD_DIM), lambda b, i: (0, 0)),
        ],
        out_specs=pl.BlockSpec((MEM_TQ, MEM_W), lambda b, i: (b * nq + i, 0)),
        out_shape=jax.ShapeDtypeStruct((m, MEM_W), BF),
        compiler_params=_params(("arbitrary", "arbitrary")),
        name="memattn_prompt",
    )(q, memk, memv, g_mq_l.reshape(1, MEM_HEAD_DIM))


MEM_SB = 4


def _memattn_sample_kernel(q_ref, k_ref, v_ref, g_ref, o_ref):
    for s in range(MEM_SB):
        q = q_ref[s]
        ms = jnp.mean(q * q, axis=-1, keepdims=True)
        qn = q * lax.rsqrt(ms + EPS) * g_ref[...]
        sc = jnp.sum(k_ref[s] * qn[None], axis=-1, keepdims=True) * MEM_SCALE
        mx = jnp.max(sc, axis=0, keepdims=True)
        p = jnp.exp(sc - mx)
        l = jnp.sum(p, axis=0)
        o = jnp.sum(p * v_ref[s], axis=0) / l
        o_ref[s] = o.astype(o_ref.dtype)


def _memattn_sample(q3, cmk, cmv, layer, g_mq_l):
    nb = q3.shape[0]
    kv_spec = pl.BlockSpec((None, MEM_SB, N_MEM, MEM_HEADS, MEM_HEAD_DIM), lambda i: (layer, i, 0, 0, 0))
    out = pl.pallas_call(
        _memattn_sample_kernel,
        grid=(nb // MEM_SB,),
        in_specs=[
            pl.BlockSpec((MEM_SB, MEM_HEADS, MEM_HEAD_DIM), lambda i: (i, 0, 0)),
            kv_spec, kv_spec,
            pl.BlockSpec((1, MEM_HEAD_DIM), lambda i: (0, 0)),
        ],
        out_specs=pl.BlockSpec((MEM_SB, MEM_HEADS, MEM_HEAD_DIM), lambda i: (i, 0, 0)),
        out_shape=jax.ShapeDtypeStruct((nb, MEM_HEADS, MEM_HEAD_DIM), BF),
        compiler_params=_params(("arbitrary",)),
        name="memattn_sample",
    )(q3, cmk, cmv, g_mq_l.reshape(1, MEM_HEAD_DIM))
    return out.reshape(nb, MEM_W)


FLASH_T = 512
FLASH_TK = 512
FLASH_HP = 2
MLA_SCALE = QK_DIM ** -0.5


def _flash_kernel(q_ref, k_ref, v_ref, o_ref):
    qi = pl.program_id(2)
    t, tk = FLASH_T, FLASH_TK
    ratio = t // tk
    row = lax.broadcasted_iota(jnp.int32, (t, tk), 0)
    col = lax.broadcasted_iota(jnp.int32, (t, tk), 1)

    def tile(ki, carry, hh, diag):
        m, l, acc = carry
        r = pl.multiple_of(ki * tk, tk)
        q = q_ref[:, hh * HEAD_PAD:(hh + 1) * HEAD_PAD]
        k = k_ref[pl.ds(r, tk), hh * HEAD_PAD:(hh + 1) * HEAD_PAD]
        s = lax.dot_general(q, k, NT_DIMS, preferred_element_type=F32) * MLA_SCALE
        if diag is not None:
            s = jnp.where(col + diag * tk <= row, s, NEG_BIG)
        m_new = jnp.maximum(m, jnp.max(s, axis=-1, keepdims=True))
        alpha = jnp.exp(m - m_new)
        p = jnp.exp(s - m_new)
        l = alpha * l + jnp.sum(p, axis=-1, keepdims=True)
        v = v_ref[pl.ds(r, tk), hh * V_DIM:(hh + 1) * V_DIM]
        acc = alpha * acc + jnp.dot(p.astype(BF), v, preferred_element_type=F32)
        return m_new, l, acc

    def tiles(ki, carries, diag):
        return tuple(tile(ki, carries[hh], hh, diag) for hh in range(FLASH_HP))

    init = tuple((jnp.full((t, 1), NEG_BIG, F32), jnp.zeros((t, 1), F32), jnp.zeros((t, V_DIM), F32))
                 for _ in range(FLASH_HP))
    carries = lax.fori_loop(0, qi * ratio, lambda ki, c: tiles(ki, c, None), init)
    for d in range(ratio):
        carries = tiles(qi * ratio + d, carries, d)
    for hh in range(FLASH_HP):
        m, l, acc = carries[hh]
        o_ref[:, hh * V_DIM:(hh + 1) * V_DIM] = (acc / l).astype(o_ref.dtype)


def _mla_prompt_attention(q, kfull, vfull, batch):
    m = q.shape[0]
    nq = SEQ // FLASH_T
    hp = FLASH_HP
    return pl.pallas_call(
        _flash_kernel,
        grid=(batch, MLA_HEADS // hp, nq),
        in_specs=[
            pl.BlockSpec((FLASH_T, hp * HEAD_PAD), lambda b, h, i: (b * nq + i, h)),
            pl.BlockSpec((SEQ, hp * HEAD_PAD), lambda b, h, i: (b, h)),
            pl.BlockSpec((SEQ, hp * V_DIM), lambda b, h, i: (b, h)),
        ],
        out_specs=pl.BlockSpec((FLASH_T, hp * V_DIM), lambda b, h, i: (b * nq + i, h)),
        out_shape=jax.ShapeDtypeStruct((m, MLA_HEADS * V_DIM), BF),
        compiler_params=_params(("arbitrary", "arbitrary", "arbitrary")),
        name="mla_prompt_attention",
    )(q, kfull, vfull)


def _headwise_kernel(x_ref, w_ref, g_ref, o_ref):
    x = (x_ref[...].astype(F32) * g_ref[...]).astype(BF)
    o_ref[...] = jnp.dot(x, w_ref[...].astype(BF), preferred_element_type=F32).astype(o_ref.dtype)


def _headwise_mm(name, x, x_spec, w, w_spec, g, out_shape, out_spec):
    kx = g.shape[-1]
    return pl.pallas_call(
        _headwise_kernel,
        grid=(MLA_HEADS,),
        in_specs=[x_spec, w_spec, pl.BlockSpec((1, kx), lambda h: (0, 0))],
        out_specs=out_spec,
        out_shape=out_shape,
        compiler_params=_params(("arbitrary",)),
        name=name,
    )(x, w, g)


PAGES_PER_STEP = 32


def _paged_kernel(pt_ref, ql_ref, q_ref, cn_ref, krn_ref, ksn_ref, lat_hbm, kr_hbm, ks_hbm, o_ref,
                  m_s, l_s, acc_s, cbuf, krbuf, ksbuf, lat_buf, kr_buf, ks_buf, sem):
    pp = PAGES_PER_STEP
    b = pl.program_id(0)
    c = pl.program_id(1)
    nb = pl.num_programs(0)
    nc = pl.num_programs(1)
    step = b * nc + c
    slot = lax.rem(step, 2)

    def fetch(bb, cc, s):
        for i in range(pp):
            pid = pt_ref[bb, cc * pp + i]
            pltpu.make_async_copy(lat_hbm.at[pid], lat_buf.at[s, i], sem.at[s]).start(priority=i % 2)
            pltpu.make_async_copy(kr_hbm.at[pid], kr_buf.at[s, i], sem.at[s]).start(priority=i % 2)
            pltpu.make_async_copy(ks_hbm.at[pid], ks_buf.at[s, i], sem.at[s]).start(priority=i % 2)

    def wait(s):
        pltpu.make_async_copy(lat_hbm.at[pl.ds(0, pp)], lat_buf.at[s], sem.at[s]).wait()
        pltpu.make_async_copy(kr_hbm.at[pl.ds(0, pp)], kr_buf.at[s], sem.at[s]).wait()
        pltpu.make_async_copy(ks_hbm.at[pl.ds(0, pp)], ks_buf.at[s], sem.at[s]).wait()

    @pl.when(step == 0)
    def _():
        fetch(b, c, 0)

    last = step == nb * nc - 1
    wrap = c == nc - 1
    nb_ = jnp.where(last, b, jnp.where(wrap, b + 1, b))
    nc_ = jnp.where(last, c, jnp.where(wrap, 0, c + 1))
    fetch(nb_, nc_, 1 - slot)
    wait(slot)

    @pl.when(c == 0)
    def _():
        m_s[...] = jnp.full(m_s.shape, NEG_BIG, F32)
        l_s[...] = jnp.zeros(l_s.shape, F32)
        acc_s[...] = jnp.zeros(acc_s.shape, F32)

    ql = ql_ref[0]
    qr32 = q_ref[0][:, LANES:LANES + ROPE_DIM]
    qr = qr32.astype(BF)
    for i in range(pp):
        rows = slice(i * PAGE_SIZE, (i + 1) * PAGE_SIZE)
        cbuf[rows, :] = lat_buf[slot, i].astype(BF)
        krbuf[:, rows] = kr_buf[slot, i].astype(BF)
        ksbuf[:, rows] = ks_buf[slot, i]
    keys = cbuf[...]
    s = (lax.dot_general(ql, keys, NT_DIMS, preferred_element_type=F32)
         + jnp.dot(qr, krbuf[...], preferred_element_type=F32))
    s = s * ksbuf[...] * MLA_SCALE
    m_old = m_s[...]
    m_new = jnp.maximum(m_old, jnp.max(s, axis=-1, keepdims=True))
    alpha = jnp.exp(m_old - m_new)
    p = jnp.exp(s - m_new)
    l_new = alpha * l_s[...] + jnp.sum(p, axis=-1, keepdims=True)
    pv = jnp.dot(p.astype(BF), keys, preferred_element_type=F32)
    acc_new = alpha * acc_s[...] + pv
    m_s[...] = m_new
    l_s[...] = l_new
    acc_s[...] = acc_new

    @pl.when(c == nc - 1)
    def _():
        cn = cn_ref[0]
        krn = krn_ref[0][:, 0:ROPE_DIM]
        s_new = (jnp.sum(ql.astype(F32) * cn, axis=-1, keepdims=True)
                 + jnp.sum(qr32 * krn, axis=-1, keepdims=True)) * ksn_ref[0] * MLA_SCALE
        m2 = jnp.maximum(m_new, s_new)
        a2 = jnp.exp(m_new - m2)
        p2 = jnp.exp(s_new - m2)
        l2 = a2 * l_new + p2
        o_ref[0] = ((a2 * acc_new + p2 * cn) / l2).astype(o_ref.dtype)

    @pl.when(last)
    def _():
        wait(1 - slot)


def _mla_sample_attention(page_table, qlat3, q3, cache_latent, cache_krope_t, cache_kscale_t,
                          cn3, krn3, ksn3):
    nb, n_pages = page_table.shape
    pp = PAGES_PER_STEP
    nc = n_pages // pp
    any_spec = pl.BlockSpec(memory_space=pl.ANY)
    in_specs = [
        pl.BlockSpec((1, MLA_HEADS, KV_LORA), lambda b, c, pt: (b, 0, 0)),
        pl.BlockSpec((1, MLA_HEADS, HEAD_PAD), lambda b, c, pt: (b, 0, 0)),
        pl.BlockSpec((1, 1, KV_LORA), lambda b, c, pt: (b, 0, 0)),
        pl.BlockSpec((1, 1, LANES), lambda b, c, pt: (b, 0, 0)),
        pl.BlockSpec((1, MLA_HEADS, 1), lambda b, c, pt: (b, 0, 0)),
        any_spec, any_spec, any_spec,
    ]
    grid_spec = pltpu.PrefetchScalarGridSpec(
        num_scalar_prefetch=1,
        grid=(nb, nc),
        in_specs=in_specs,
        out_specs=pl.BlockSpec((1, MLA_HEADS, KV_LORA), lambda b, c, pt: (b, 0, 0)),
        scratch_shapes=[pltpu.VMEM((MLA_HEADS, 1), F32), pltpu.VMEM((MLA_HEADS, 1), F32),
                        pltpu.VMEM((MLA_HEADS, KV_LORA), F32),
                        pltpu.VMEM((pp * PAGE_SIZE, KV_LORA), BF), pltpu.VMEM((ROPE_DIM, pp * PAGE_SIZE), BF),
                        pltpu.VMEM((MLA_HEADS, pp * PAGE_SIZE), F32),
                        pltpu.VMEM((2, pp, PAGE_SIZE, KV_LORA), F32),
                        pltpu.VMEM((2, pp, ROPE_DIM, PAGE_SIZE), F32),
                        pltpu.VMEM((2, pp, MLA_HEADS, PAGE_SIZE), F32),
                        pltpu.SemaphoreType.DMA((2,))],
    )
    return pl.pallas_call(
        _paged_kernel,
        grid_spec=grid_spec,
        out_shape=jax.ShapeDtypeStruct((nb, MLA_HEADS, KV_LORA), F32),
        compiler_params=_params(("arbitrary", "arbitrary")),
        name="mla_sample_attention",
    )(page_table, qlat3, q3, cn3, krn3, ksn3, cache_latent, cache_krope_t, cache_kscale_t)


def _router_kernel(xp_ref, xs_ref, g_ref, w_ref, h_ref, gate_ref, info_ref, cnt_ref, cnt_s, *, nbp):
    i = pl.program_id(0)

    @pl.when(i == 0)
    def _():
        cnt_s[...] = jnp.zeros(cnt_s.shape, F32)

    xf = jnp.where(i < nbp, xp_ref[...], xs_ref[...])
    ms = jnp.mean(xf * xf, axis=-1, keepdims=True)
    h = xf * lax.rsqrt(ms + EPS) * g_ref[...]
    hb = lax.bitcast_convert_type(h.astype(BF).astype(F32), jnp.uint32)
    h_ref[...] = (hb[:, :D_MODEL // 2] >> 16) | hb[:, D_MODEL // 2:]
    logits = jnp.dot(h.astype(BF), w_ref[...].astype(BF),
                     preferred_element_type=F32)
    lane = lax.broadcasted_iota(jnp.int32, logits.shape, 1)
    big = jnp.int32(LANES)

    is_g = lane < N_GROUPS
    lg = jnp.where(is_g, logits, NEG_BIG)
    eg = jnp.where(is_g, jnp.exp(lg - jnp.max(lg, axis=-1, keepdims=True)), 0.0)
    pg = eg / jnp.sum(eg, axis=-1, keepdims=True)
    p_grp = jnp.max(pg, axis=-1, keepdims=True)
    grp = jnp.min(jnp.where(is_g & (pg == p_grp), lane, big), axis=-1, keepdims=True)

    e_idx = lane - N_GROUPS
    sel = (e_idx >= 0) & (e_idx < N_EXPERTS) & ((e_idx >> 3) == grp)
    le = jnp.where(sel, logits, NEG_BIG)
    ee = jnp.where(sel, jnp.exp(le - jnp.max(le, axis=-1, keepdims=True)), 0.0)
    pe = ee / jnp.sum(ee, axis=-1, keepdims=True)
    top1 = jnp.max(jnp.where(sel, pe, -1.0), axis=-1, keepdims=True)
    i1 = jnp.min(jnp.where(sel & (pe == top1), lane, big), axis=-1, keepdims=True)
    rest = sel & (lane != i1)
    top2 = jnp.max(jnp.where(rest, pe, -1.0), axis=-1, keepdims=True)
    i2 = jnp.min(jnp.where(rest & (pe == top2), lane, big), axis=-1, keepdims=True)
    denom = top1 + top2
    g1 = p_grp * top1 / denom
    g2 = p_grp * top2 / denom
    gate_ref[...] = jnp.where(lane == 0, g1, jnp.where(lane == 1, g2, 0.0))

    onehot = ((lane == i1) | (lane == i2)).astype(F32)
    r_i = lax.broadcasted_iota(jnp.int32, (MOE_BLOCK, MOE_BLOCK), 0)
    c_i = lax.broadcasted_iota(jnp.int32, (MOE_BLOCK, MOE_BLOCK), 1)
    tri = (c_i < r_i).astype(BF)
    before = jnp.dot(tri, onehot.astype(BF), preferred_element_type=F32) + cnt_s[...]
    rank1 = jnp.sum(jnp.where(lane == i1, before, 0.0), axis=-1, keepdims=True).astype(jnp.int32)
    rank2 = jnp.sum(jnp.where(lane == i2, before, 0.0), axis=-1, keepdims=True).astype(jnp.int32)
    info_ref[...] = jnp.where(lane == 0, i1 - N_GROUPS, jnp.where(lane == 1, i2 - N_GROUPS,
                              jnp.where(lane == 2, rank1, jnp.where(lane == 3, rank2, 0))))
    cnt_new = cnt_s[...] + jnp.sum(onehot, axis=0, keepdims=True)
    cnt_s[...] = cnt_new
    cnt_ref[...] = jnp.broadcast_to(cnt_new, cnt_ref.shape)


def _router(xp, xs, g, w_router):
    nbp = xp.shape[0] // MOE_BLOCK
    assert xs.shape[0] == MOE_BLOCK
    mt = xp.shape[0] + xs.shape[0]
    tok_spec = lambda width: pl.BlockSpec((MOE_BLOCK, width), lambda i: (i, 0))
    return pl.pallas_call(
        functools.partial(_router_kernel, nbp=nbp),
        grid=(nbp + 1,),
        in_specs=[
            pl.BlockSpec((MOE_BLOCK, D_MODEL), lambda i: (jnp.minimum(i, nbp - 1), 0)),
            pl.BlockSpec((MOE_BLOCK, D_MODEL), lambda i: (0, 0)),
            pl.BlockSpec((1, D_MODEL), lambda i: (0, 0)),
            pl.BlockSpec((D_MODEL, LANES), lambda i: (0, 0)),
        ],
        out_specs=[tok_spec(D_MODEL // 2), tok_spec(LANES), tok_spec(LANES),
                   pl.BlockSpec((SUBLANES, LANES), lambda i: (0, 0))],
        out_shape=[jax.ShapeDtypeStruct((mt, D_MODEL // 2), jnp.uint32),
                   jax.ShapeDtypeStruct((mt, LANES), F32),
                   jax.ShapeDtypeStruct((mt, LANES), jnp.int32),
                   jax.ShapeDtypeStruct((SUBLANES, LANES), F32)],
        scratch_shapes=[pltpu.VMEM((1, LANES), F32)],
        compiler_params=_params(("arbitrary",)),
        name="moe_router",
    )(xp, xs, g.reshape(1, D_MODEL), w_router)


TRASH_ROWS = 2 * MOE_BLOCK


def _expert_kernel(be_ref, nu_ref, par_ref, nxt_ref, src_ref, srcn_ref, dst_ref,
                   h_ref, wg_hbm, wu_hbm, wd_hbm, ys_ref,
                   xbuf, ybuf, wg_f, wu_f, wd_f, wg_s, wu_s, wd_s, gsem, ssem, wsem, *, layer, n_slots):
    b = pl.program_id(0)
    nu = nu_ref[0]
    rows = MOE_BLOCK

    unroll = 8

    def gather_start(tab_ref, slot):
        def body(c, carry):
            for u in range(unroll):
                t = c * unroll + u
                pltpu.make_async_copy(h_ref.at[pl.ds(tab_ref[0, 0, t], 1)], xbuf.at[slot, pl.ds(t, 1)],
                                      gsem.at[slot]).start(priority=u % 2)
            return carry
        lax.fori_loop(0, rows // unroll, body, 0)

    def weight_copies(e, slot):
        return [pltpu.make_async_copy(w.at[layer, e], f.at[slot], wsem.at[slot])
                for w, f in ((wg_hbm, wg_f), (wu_hbm, wu_f), (wd_hbm, wd_f))]

    def scatter_wait():
        pltpu.make_async_copy(ybuf, ys_ref.at[pl.ds(0, rows)], ssem).wait()

    @pl.when(b < nu)
    def _():
        slot = lax.rem(b, 2)

        @pl.when(b == 0)
        def _():
            gather_start(src_ref, 0)

        @pl.when(b + 1 < nu)
        def _():
            gather_start(srcn_ref, 1 - slot)

        e = be_ref[b]
        p = par_ref[b]
        fresh = (b == 0) | (e != be_ref[jnp.maximum(b - 1, 0)])

        @pl.when(fresh)
        def _():
            @pl.when(b == 0)
            def _():
                for c in weight_copies(e, p):
                    c.start(priority=1)

            for c in weight_copies(e, p):
                c.wait()
            nxt = nxt_ref[b]

            @pl.when(nxt >= 0)
            def _():
                for c in weight_copies(nxt, 1 - p):
                    c.start(priority=1)

            wg_s[...] = wg_f[p].astype(BF)
            wu_s[...] = wu_f[p].astype(BF)
            wd_s[...] = wd_f[p].astype(BF)

        pltpu.make_async_copy(h_ref.at[pl.ds(0, rows)], xbuf.at[slot], gsem.at[slot]).wait()
        packed = xbuf[slot]
        x = jnp.concatenate(
            [lax.bitcast_convert_type(packed << 16, F32).astype(BF),
             lax.bitcast_convert_type(packed & jnp.uint32(0xFFFF0000), F32).astype(BF)], axis=1)
        a = jnp.dot(x, wg_s[...], preferred_element_type=F32)
        u = jnp.dot(x, wu_s[...], preferred_element_type=F32)
        hmid = (a * jax.nn.sigmoid(a) * u).astype(BF)

        @pl.when(b >= 1)
        def _():
            scatter_wait()

        ybuf[...] = jnp.dot(hmid, wd_s[...], preferred_element_type=F32)

        def scatter_body(c, carry):
            for u in range(unroll):
                t = c * unroll + u
                pltpu.make_async_copy(ybuf.at[pl.ds(t, 1)], ys_ref.at[pl.ds(dst_ref[0, 0, t], 1)],
                                      ssem).start(priority=u % 2)
            return carry
        lax.fori_loop(0, rows // unroll, scatter_body, 0)

        @pl.when(b == nu - 1)
        def _():
            scatter_wait()
            ybuf[...] = jnp.zeros(ybuf.shape, F32)
            for c in range(TRASH_ROWS // rows):
                cp = pltpu.make_async_copy(ybuf, ys_ref.at[pl.ds(n_slots + c * rows, rows)], ssem)
                cp.start()
                cp.wait()


def _experts(blk_e, n_used, par, nxt, row_src3, row_dst3, h, w_gate, w_up, w_down, layer, n_slots):
    nblk = row_src3.shape[0]
    tab_spec = lambda f: pl.BlockSpec((1, 1, MOE_BLOCK), f, memory_space=pltpu.SMEM)
    any_spec = pl.BlockSpec(memory_space=pl.ANY)
    grid_spec = pltpu.PrefetchScalarGridSpec(
        num_scalar_prefetch=4,
        grid=(nblk,),
        in_specs=[
            tab_spec(lambda b, *_: (b, 0, 0)),
            tab_spec(lambda b, *_: (jnp.minimum(b + 1, nblk - 1), 0, 0)),
            tab_spec(lambda b, *_: (b, 0, 0)),
            any_spec, any_spec, any_spec, any_spec,
        ],
        out_specs=any_spec,
        scratch_shapes=[
            pltpu.VMEM((2, MOE_BLOCK, D_MODEL // 2), jnp.uint32), pltpu.VMEM((MOE_BLOCK, D_MODEL), F32),
            pltpu.VMEM((2, D_MODEL, D_EXPERT), F32), pltpu.VMEM((2, D_MODEL, D_EXPERT), F32),
            pltpu.VMEM((2, D_EXPERT, D_MODEL), F32),
            pltpu.VMEM((D_MODEL, D_EXPERT), BF), pltpu.VMEM((D_MODEL, D_EXPERT), BF),
            pltpu.VMEM((D_EXPERT, D_MODEL), BF),
            pltpu.SemaphoreType.DMA((2,)), pltpu.SemaphoreType.DMA(()), pltpu.SemaphoreType.DMA((2,)),
        ],
    )
    return pl.pallas_call(
        functools.partial(_expert_kernel, layer=layer, n_slots=n_slots),
        grid_spec=grid_spec,
        out_shape=jax.ShapeDtypeStruct((n_slots + TRASH_ROWS, D_MODEL), F32),
        compiler_params=_params(("arbitrary",)),
        name="moe_experts",
    )(blk_e, n_used, par, nxt, row_src3, row_src3, row_dst3, h, w_gate, w_up, w_down)


def _combine_kernel(gate_ref, xp_ref, xs_ref, y0_ref, y1_ref, op_ref, os_ref, *, nbp):
    i = pl.program_id(0)
    y = y0_ref[...] * gate_ref[:, 0:1] + y1_ref[...] * gate_ref[:, 1:2]

    @pl.when(i < nbp)
    def _():
        op_ref[...] = xp_ref[...] + y

    @pl.when(i == nbp)
    def _():
        os_ref[...] = xs_ref[...] + y


def _combine(gate, xp, xs, ys):
    nbp = xp.shape[0] // MOE_BLOCK
    nb_tok = nbp + 1
    p_spec = pl.BlockSpec((MOE_BLOCK, D_MODEL), lambda i: (jnp.minimum(i, nbp - 1), 0))
    s_spec = pl.BlockSpec((MOE_BLOCK, D_MODEL), lambda i: (0, 0))
    return pl.pallas_call(
        functools.partial(_combine_kernel, nbp=nbp),
        grid=(nb_tok,),
        in_specs=[
            pl.BlockSpec((MOE_BLOCK, LANES), lambda i: (i, 0)),
            p_spec, s_spec,
            pl.BlockSpec((MOE_BLOCK, D_MODEL), lambda i: (i, 0)),
            pl.BlockSpec((MOE_BLOCK, D_MODEL), lambda i: (nb_tok + i, 0)),
        ],
        out_specs=[p_spec, s_spec],
        out_shape=[jax.ShapeDtypeStruct(xp.shape, F32), jax.ShapeDtypeStruct(xs.shape, F32)],
        compiler_params=_params(("arbitrary",)),
        name="moe_combine",
    )(gate, xp, xs, ys, ys)


def _moe(xp, xs, g_ffn_l, w_router, w_gate, w_up, w_down, layer):
    mt = xp.shape[0] + xs.shape[0]
    n_slots = 2 * mt
    nblk = (n_slots + N_EXPERTS * (MOE_BLOCK - 1) + MOE_BLOCK - 1) // MOE_BLOCK
    h, gate, info, cnt = _router(xp, xs, g_ffn_l, w_router)

    ids = jnp.arange(N_EXPERTS, dtype=jnp.int32)
    counts = cnt[0, N_GROUPS:N_GROUPS + N_EXPERTS].astype(jnp.int32)
    used = counts > 0
    padded = (counts + MOE_BLOCK - 1) // MOE_BLOCK * MOE_BLOCK
    pad_ends = jnp.cumsum(padded)
    pad_starts = pad_ends - padded
    n_used = (pad_ends[-1] // MOE_BLOCK).astype(jnp.int32)
    blk_ids = jnp.arange(nblk, dtype=jnp.int32)
    blk_e = jnp.sum((pad_ends[None, :] <= blk_ids[:, None] * MOE_BLOCK).astype(jnp.int32), axis=1)
    last_e = jnp.max(jnp.where(used, ids, 0))
    blk_e = jnp.where(blk_ids < n_used, jnp.minimum(blk_e, N_EXPERTS - 1), last_e).astype(jnp.int32)
    ordinal = jnp.cumsum(used.astype(jnp.int32)) - 1
    next_e = jnp.min(jnp.where((ids[None, :] > ids[:, None]) & used[None, :], ids[None, :], N_EXPERTS), axis=1)
    next_e = jnp.where(next_e == N_EXPERTS, -1, next_e)
    blk_onehot = blk_e[:, None] == ids[None, :]
    par = jnp.sum(jnp.where(blk_onehot, (ordinal % 2)[None, :], 0), axis=1).astype(jnp.int32)
    nxt = jnp.sum(jnp.where(blk_onehot, next_e[None, :], 0), axis=1).astype(jnp.int32)

    e2, rank2 = info[:, 0:2], info[:, 2:4]
    start2 = jnp.sum(jnp.where(e2[:, :, None] == ids[None, None, :], pad_starts[None, None, :], 0), axis=-1)
    dest = start2 + rank2
    slot_code = jnp.arange(mt, dtype=jnp.int32)[:, None] + jnp.array([[0, mt]], jnp.int32)
    rows = jnp.arange(nblk * MOE_BLOCK, dtype=jnp.int32)
    row_dst = (n_slots + rows % TRASH_ROWS).at[dest.reshape(-1)].set(slot_code.reshape(-1))
    row_src = jnp.where(row_dst < n_slots, row_dst % mt, 0)
    shape3 = (nblk, 1, MOE_BLOCK)

    ys = _experts(blk_e, n_used.reshape(1), par, nxt, row_src.reshape(shape3), row_dst.reshape(shape3),
                  h, w_gate, w_up, w_down, layer, n_slots)
    return _combine(gate, xp, xs, ys)


def _rope_tables(pos):
    half = ROPE_DIM // 2
    inv = ROPE_THETA ** (-jnp.arange(half, dtype=F32) / half)
    ang = pos.astype(F32)[:, None] * inv[None, :]
    cos, sin = jnp.cos(ang), jnp.sin(ang)
    z = jnp.zeros_like(cos)
    a = jnp.concatenate([cos, cos, z, z], axis=1)
    b = jnp.concatenate([-sin, sin, z, z], axis=1)
    return a, b


def _out_spec(tm, tn):
    return pl.BlockSpec((tm, tn), lambda i, j: (i, j))


def _table_specs(tabs, tm, nper):
    return [(t, pl.BlockSpec((tm, LANES), lambda i, j, nper=nper: (i % nper, 0))) for t in tabs]


def _vec_extra(v):
    n = v.shape[-1]
    return (v.reshape(1, n), pl.BlockSpec((1, n), lambda i, j: (0, 0)))


def kernel(x_prompt, x_sample, cache_latent, cache_krope, cache_kscale, cache_mem_k, cache_mem_v, state_conv, page_table, mem_prompt, g_mix, g_ffn, g_mem, w_mem_kv, g_mq, g_mk, w_in_a, conv_w, conv_b, ln_g, ln_b, w_out_a, g_kv_in, w_kv_down, g_kv_lat, w_uk, w_uv, g_k, w_in_b, g_qlat, w_q_up, g_q, w_out_b, w_rg, w_re, w_gate, w_up, w_down):
    batch, seq, _ = x_prompt.shape
    nb = x_sample.shape[0]
    depth = g_mix.shape[0]
    n_a = w_in_a.shape[0]
    mp = batch * seq
    tmp = 1024
    xp = x_prompt.reshape(mp, D_MODEL)
    xs = x_sample.reshape(nb, D_MODEL)

    tabs_p = _rope_tables(jnp.arange(seq))
    tabs_s = _rope_tables(jnp.full((nb,), PAST_LEN))
    w_kvd = jnp.concatenate([w_kv_down, w_kv_down[:, KV_LORA:]], axis=1)
    gk_n = g_k[:NOPE_DIM]
    gk_r = jnp.concatenate([g_k[NOPE_DIM:], g_k[NOPE_DIM:]])
    real_lanes = jnp.concatenate([jnp.ones((ROPE_DIM,), F32), jnp.zeros((LANES - ROPE_DIM,), F32)])
    w_uk2 = w_uk.reshape(KV_LORA, MLA_HEADS * NOPE_DIM)
    w_uk_t = jnp.transpose(w_uk, (1, 2, 0))
    w_uv2 = w_uv.reshape(KV_LORA, MLA_HEADS * V_DIM)
    w_qu = jnp.concatenate([w_q_up, w_q_up[..., NOPE_DIM:]], axis=-1)
    w_qu = w_qu.reshape(w_q_up.shape[0], Q_LORA, MLA_HEADS * HEAD_PAD)
    gq_pad = jnp.concatenate([g_q, g_q[:, NOPE_DIM:]], axis=-1)
    w_router = jnp.pad(jnp.concatenate([w_rg, w_re], axis=-1),
                       ((0, 0), (0, 0), (0, LANES - N_GROUPS - N_EXPERTS)))
    conv_w32 = jnp.pad(conv_w, ((0, 0), (0, 32 - CONV_W), (0, 0)))
    cache_kscale_t = jnp.transpose(cache_kscale, (0, 2, 1))
    cache_krope_t = jnp.transpose(cache_krope, (0, 2, 1))
    state_conv_t = jnp.transpose(state_conv, (0, 2, 1, 3))

    memk, memv = _memory_kv_all(mem_prompt.reshape(batch * N_MEM, D_MODEL), g_mem, w_mem_kv, g_mk)

    conv_p_list, conv_s_list = [], []
    shared = None
    for l in range(depth):
        if l < n_a:
            n_glu = D_MODEL
            outs_p, outs_s = [], []
            for x, m, tm, dst in ((xp, mp, tmp, outs_p), (xs, nb, nb, outs_s)):
                tn = 256
                ng = n_glu // tn
                glu_col = lambda j, ng=ng: jnp.minimum(j, ng - 1)
                a, qm = _fused_mm(
                    "in_a", [(x, D_MODEL, 0)],
                    [(0, w_in_a, l, D_MODEL, tn, 0, glu_col),
                     (0, w_in_a, l, D_MODEL, tn, 0, lambda j, ng=ng: ng + jnp.minimum(j, ng - 1)),
                     (0, w_in_a, l, D_MODEL, tn, 0, lambda j, ng=ng: 2 * ng + jnp.maximum(j - ng, 0))],
                    m=m, tm=tm, nj=ng + MEM_W // tn, gain=g_mix[l],
                    outs=[(jax.ShapeDtypeStruct((m, n_glu), F32),
                           pl.BlockSpec((tm, tn), lambda i, j, ng=ng: (i, jnp.minimum(j, ng - 1)))),
                          (jax.ShapeDtypeStruct((m, MEM_W), F32),
                           pl.BlockSpec((tm, tn), lambda i, j, ng=ng: (i, jnp.maximum(j - ng, 0))))],
                    epi=functools.partial(_epi_glu_memq, ng=ng))
                dst.extend([a, qm])
            a_p, qm_p = outs_p
            a_s, qm_s = outs_s
            yc_p = _conv_prompt(a_p, conv_w32[l], conv_b[l], ln_g[l], ln_b[l], batch)
            conv_p_list.append(a_p.reshape(batch, seq, D_MODEL)[:, seq - (CONV_W - 1):])
            yc_s, ns = _conv_sample(state_conv_t, l, a_s, conv_w32[l], conv_b[l], ln_g[l], ln_b[l])
            conv_s_list.append(ns)
            ym_p = _memattn_prompt(qm_p, 0, memk, memv, l, g_mq[l], batch)
            ym_s = _memattn_sample(qm_s.reshape(nb, MEM_HEADS, MEM_HEAD_DIM), cache_mem_k, cache_mem_v,
                                   l, g_mq[l])
            new = []
            for x, m, tm, y1, y2 in ((xp, mp, tmp, yc_p, ym_p), (xs, nb, nb, yc_s, ym_s)):
                tn = 512
                new.append(_fused_mm(
                    "out_a", [(y1, D_MODEL, 0), (y2, MEM_W, 0)],
                    [(0, w_out_a, l, D_MODEL, tn, 0, 0), (1, w_out_a, l, MEM_W, tn, D_MODEL // MEM_W, 0)],
                    m=m, tm=tm, nj=D_MODEL // tn,
                    extras=[(x, _out_spec(tm, tn))],
                    outs=[(jax.ShapeDtypeStruct((m, D_MODEL), F32), _out_spec(tm, tn))],
                    epi=_epi_residual)[0])
            xp, xs = new
        else:
            jb = l - n_a
            kfull, vfull, c_s, kr_s, ks_s = shared
            att = []
            for x, m, tm, tabs, nper, qdt in ((xp, mp, tmp, tabs_p, seq // tmp, BF),
                                              (xs, nb, nb, tabs_s, 1, F32)):
                tn = Q_LORA
                n_in = Q_LORA + MEM_W
                u, qm = _fused_mm(
                    "in_b", [(x, D_MODEL, 0)], [(0, w_in_b, jb, D_MODEL, tn, 0, 0)],
                    m=m, tm=tm, nj=n_in // tn, gain=g_mix[l],
                    outs=[(jax.ShapeDtypeStruct((m, Q_LORA), F32),
                           pl.BlockSpec((tm, tn), lambda i, j: (i, 0))),
                          (jax.ShapeDtypeStruct((m, MEM_W), F32),
                           pl.BlockSpec((tm, tn), lambda i, j: (i, jnp.maximum(j - 1, 0))))],
                    epi=_epi_split)
                qh = _fused_mm(
                    "q_up", [(u, Q_LORA, 0)], [(0, w_qu, jb, Q_LORA, HEAD_PAD, 0, 0)],
                    m=m, tm=tm, nj=MLA_HEADS, gain=g_qlat[jb],
                    extras=[_vec_extra(gq_pad[jb]), _vec_extra(real_lanes)] + _table_specs(tabs, tm, nper),
                    outs=[(jax.ShapeDtypeStruct((m, MLA_HEADS * HEAD_PAD), qdt), _out_spec(tm, HEAD_PAD))],
                    epi=_epi_qhead)[0]
                att.append((qm, qh))
            (qm_p, qh_p), (qm_s, qh_s) = att
            att_p = _mla_prompt_attention(qh_p, kfull, vfull, batch)
            ym_p = _memattn_prompt(qm_p, 0, memk, memv, l, g_mq[l], batch)
            ym_s = _memattn_sample(qm_s.reshape(nb, MEM_HEADS, MEM_HEAD_DIM), cache_mem_k, cache_mem_v,
                                   l, g_mq[l])

            qlat = _headwise_mm(
                "q_latent", qh_s, pl.BlockSpec((nb, NOPE_DIM), lambda h: (0, 2 * h)),
                w_uk_t, pl.BlockSpec((None, NOPE_DIM, KV_LORA), lambda h: (h, 0, 0)),
                gk_n.reshape(1, NOPE_DIM),
                jax.ShapeDtypeStruct((nb, MLA_HEADS * KV_LORA), BF),
                pl.BlockSpec((nb, KV_LORA), lambda h: (0, h)))
            o_lat = _mla_sample_attention(
                page_table, qlat.reshape(nb, MLA_HEADS, KV_LORA),
                qh_s.reshape(nb, MLA_HEADS, HEAD_PAD), cache_latent, cache_krope_t, cache_kscale_t,
                c_s.reshape(nb, 1, KV_LORA), kr_s.reshape(nb, 1, LANES),
                ks_s[:, :MLA_HEADS].reshape(nb, MLA_HEADS, 1))
            att_s = _headwise_mm(
                "v_expand", o_lat.reshape(nb, MLA_HEADS * KV_LORA),
                pl.BlockSpec((nb, KV_LORA), lambda h: (0, h)),
                w_uv2, pl.BlockSpec((KV_LORA, V_DIM), lambda h: (0, h)),
                jnp.ones((1, KV_LORA), F32),
                jax.ShapeDtypeStruct((nb, MLA_HEADS * V_DIM), BF),
                pl.BlockSpec((nb, V_DIM), lambda h: (0, h)))
            new = []
            n_att = MLA_HEADS * V_DIM
            for x, m, tm, y1, y2 in ((xp, mp, tmp, att_p, ym_p), (xs, nb, nb, att_s, ym_s)):
                tn = 512
                new.append(_fused_mm(
                    "out_b", [(y1, n_att, 0), (y2, MEM_W, 0)],
                    [(0, w_out_b, jb, n_att, tn, 0, 0), (1, w_out_b, jb, MEM_W, tn, n_att // MEM_W, 0)],
                    m=m, tm=tm, nj=D_MODEL // tn,
                    extras=[(x, _out_spec(tm, tn))],
                    outs=[(jax.ShapeDtypeStruct((m, D_MODEL), F32), _out_spec(tm, tn))],
                    epi=_epi_residual)[0])
            xp, xs = new

        xp, xs = _moe(xp, xs, g_ffn[l], w_router[l], w_gate, w_up, w_down, l)

        if l == n_a - 1:
            lat = []
            for x, m, tm, tabs, nper in ((xp, mp, tmp, tabs_p, seq // tmp), (xs, nb, nb, tabs_s, 1)):
                nck = KV_LORA + LANES
                c, kr128, ssq128 = _fused_mm(
                    "kv_latent", [(x, D_MODEL, 0)], [(0, w_kvd, None, D_MODEL, nck, 0, 0)],
                    m=m, tm=tm, nj=1, gain=g_kv_in,
                    extras=[_vec_extra(g_kv_lat), _vec_extra(gk_r), _vec_extra(real_lanes)]
                    + _table_specs(tabs, tm, nper),
                    outs=[(jax.ShapeDtypeStruct((m, KV_LORA), F32), _out_spec(tm, KV_LORA)),
                          (jax.ShapeDtypeStruct((m, LANES), F32), pl.BlockSpec((tm, LANES), lambda i, j: (i, 0))),
                          (jax.ShapeDtypeStruct((m, LANES), F32), pl.BlockSpec((tm, LANES), lambda i, j: (i, 0)))],
                    epi=_epi_latent)
                kf, ks128 = _fused_mm(
                    "k_nope", [(c, KV_LORA, 0)], [(0, w_uk2, None, KV_LORA, 2 * NOPE_DIM, 0, 0)],
                    m=m, tm=tm, nj=MLA_HEADS // 2,
                    extras=[(kr128, pl.BlockSpec((tm, LANES), lambda i, j: (i, 0))),
                            (ssq128, pl.BlockSpec((tm, LANES), lambda i, j: (i, 0))),
                            _vec_extra(gk_n)],
                    outs=[(jax.ShapeDtypeStruct((m, MLA_HEADS * HEAD_PAD), BF), _out_spec(tm, 2 * HEAD_PAD)),
                          (jax.ShapeDtypeStruct((m, LANES), F32), pl.BlockSpec((tm, LANES), lambda i, j: (i, 0)))],
                    epi=_epi_knope)
                lat.append((c, kr128, ks128, kf))
            (c_p, kr_p, ks_p, kfull), (c_s, kr_s, ks_s, _) = lat
            vfull = _fused_mm(
                "v_full", [(c_p, KV_LORA, 0)], [(0, w_uv2, None, KV_LORA, 512, 0, 0)],
                m=mp, tm=tmp, nj=MLA_HEADS * V_DIM // 512,
                outs=[(jax.ShapeDtypeStruct((mp, MLA_HEADS * V_DIM), BF), _out_spec(tmp, 512))],
                epi=_epi_plain)[0]
            shared = (kfull, vfull, c_s, kr_s, ks_s)

    shape5 = (depth, batch, N_MEM, MEM_HEADS, MEM_HEAD_DIM)
    return (xp.reshape(batch, seq, D_MODEL), xs.reshape(nb, 1, D_MODEL),
            c_p.reshape(batch, seq, KV_LORA), kr_p[:, :ROPE_DIM].reshape(batch, seq, ROPE_DIM),
            ks_p[:, :MLA_HEADS].reshape(batch, seq, MLA_HEADS),
            memk.reshape(shape5), memv.reshape(shape5),
            jnp.stack(conv_p_list, axis=0),
            c_s.reshape(nb, 1, KV_LORA), kr_s[:, :ROPE_DIM].reshape(nb, 1, ROPE_DIM),
            ks_s[:, :MLA_HEADS].reshape(nb, 1, MLA_HEADS),
            jnp.transpose(jnp.stack(conv_s_list, axis=0), (0, 2, 1, 3)))
```

```python
import functools
import math

import jax
import jax.numpy as jnp
from jax import lax
from jax.experimental import pallas as pl
from jax.experimental.pallas import tpu as pltpu

F32 = jnp.float32
BF = jnp.bfloat16

D_MODEL = 2048
SEQ = 2048
PAST_LEN = 8192
PAGE_SIZE = 128
CONV_W = 31
N_MEM = 256
MEM_HEADS = 4
MEM_HEAD_DIM = 256
MEM_W = MEM_HEADS * MEM_HEAD_DIM
MLA_HEADS = 16
Q_LORA = 512
KV_LORA = 512
NOPE_DIM = 128
ROPE_DIM = 64
QK_DIM = NOPE_DIM + ROPE_DIM
V_DIM = 128
ROPE_THETA = 10000.0
N_GROUPS = 8
EXPERTS_PER_GROUP = 8
N_EXPERTS = 64
D_EXPERT = 512
MOE_BLOCK = 128
EPS = 1e-6

HEAD_PAD = 256
LANES = 128
VMEM_LIMIT_BYTES = 56 * 1024 * 1024
NEG_BIG = -1e30
NT_DIMS = (((1,), (1,)), ((), ()))


def _params(sem):
    return pltpu.CompilerParams(dimension_semantics=sem, vmem_limit_bytes=VMEM_LIMIT_BYTES)


def _mm_kernel(*refs, nx, w_x, has_gain, n_extra, n_out, epi, tm):
    x_refs = refs[:nx]
    w_refs = refs[nx:nx + len(w_x)]
    pos = nx + len(w_x)
    g_ref = refs[pos] if has_gain else None
    pos += int(has_gain)
    extra_refs = refs[pos:pos + n_extra]
    pos += n_extra
    out_refs = refs[pos:pos + n_out]
    pos += n_out
    h_s = refs[pos] if has_gain else None
    j = pl.program_id(1)

    if has_gain:
        chunk = min(tm, 128)

        @pl.when(j == 0)
        def _():
            def body(c, carry):
                r = pl.multiple_of(c * chunk, chunk)
                xf = x_refs[0][pl.ds(r, chunk), :].astype(F32)
                ms = jnp.mean(xf * xf, axis=-1, keepdims=True)
                h_s[pl.ds(r, chunk), :] = (xf * lax.rsqrt(ms + EPS) * g_ref[...]).astype(BF)
                return carry
            lax.fori_loop(0, tm // chunk, body, 0)

    def product(wi):
        xi = w_x[wi]
        lhs = h_s[...] if has_gain and xi == 0 else x_refs[xi][...].astype(BF)
        return jnp.dot(lhs, w_refs[wi][...].astype(BF), preferred_element_type=F32)

    epi(_Products(product, len(w_x)), extra_refs, out_refs, j)


class _Products:
    def __init__(self, fn, n):
        self._fn, self._n = fn, n

    def __getitem__(self, i):
        if isinstance(i, slice):
            return [self._fn(k) for k in range(*i.indices(self._n))]
        return self._fn(i)


def _fused_mm(name, xs, ws, *, m, tm, nj, gain=None, extras=(), outs, epi):
    in_arrays, in_specs = [], []
    for arr, k, cb in xs:
        in_arrays.append(arr)
        in_specs.append(pl.BlockSpec((tm, k), lambda i, j, cb=cb: (i, cb)))
    for xi, arr, layer, k, tn, rb, cb0 in ws:
        in_arrays.append(arr)
        col = cb0 if callable(cb0) else (lambda j, cb0=cb0: cb0 + j)
        if layer is None:
            in_specs.append(pl.BlockSpec((k, tn), lambda i, j, rb=rb, col=col: (rb, col(j))))
        else:
            in_specs.append(pl.BlockSpec((None, k, tn),
                                         lambda i, j, l=layer, rb=rb, col=col: (l, rb, col(j))))
    scratch = []
    if gain is not None:
        k0 = xs[0][1]
        in_arrays.append(gain.reshape(1, k0).astype(F32))
        in_specs.append(pl.BlockSpec((1, k0), lambda i, j: (0, 0)))
        scratch.append(pltpu.VMEM((tm, k0), BF))
    for arr, spec in extras:
        in_arrays.append(arr)
        in_specs.append(spec)
    kern = functools.partial(_mm_kernel, nx=len(xs), w_x=tuple(w[0] for w in ws),
                             has_gain=gain is not None, n_extra=len(extras),
                             n_out=len(outs), epi=epi, tm=tm)
    res = pl.pallas_call(
        kern,
        grid=(m // tm, nj),
        in_specs=in_specs,
        out_specs=[o[1] for o in outs],
        out_shape=[o[0] for o in outs],
        scratch_shapes=scratch,
        compiler_params=_params(("arbitrary", "arbitrary")),
        name=name,
    )(*in_arrays)
    return res


def _epi_plain(ds, ex, outs, j):
    acc = ds[0]
    for d in ds[1:]:
        acc = acc + d
    outs[0][...] = acc.astype(outs[0].dtype)


def _epi_residual(ds, ex, outs, j):
    acc = ds[0]
    for d in ds[1:]:
        acc = acc + d
    outs[0][...] = (ex[0][...] + acc).astype(outs[0].dtype)


def _epi_split(ds, ex, outs, j):
    @pl.when(j == 0)
    def _():
        outs[0][...] = ds[0]

    @pl.when(j > 0)
    def _():
        outs[1][...] = ds[0]


def _epi_glu_memq(ds, ex, outs, j, *, ng):
    @pl.when(j < ng)
    def _():
        outs[0][...] = (ds[0] * jax.nn.sigmoid(ds[1])).astype(outs[0].dtype)

    @pl.when(j >= ng)
    def _():
        outs[1][...] = ds[2].astype(outs[1].dtype)


def _rope_tile(t, a_ref, b_ref):
    return t * a_ref[...] + pltpu.roll(t, 32, 1) * b_ref[...]


def _epi_qhead(ds, ex, outs, j):
    acc = ds[0]
    gq_ref, real_ref, a_ref, b_ref = ex
    lo, hi = acc[:, :LANES], acc[:, LANES:]
    sq = lo * lo + hi * hi * real_ref[...]
    sq_hi = sq.astype(BF)
    sq_lo = (sq - sq_hi.astype(F32)).astype(BF)
    ones = jnp.ones((LANES, LANES), BF)
    ssq = (jnp.dot(sq_hi, ones, preferred_element_type=F32)
           + jnp.dot(sq_lo, ones, preferred_element_type=F32))
    inv = lax.rsqrt(ssq * (1.0 / QK_DIM) + EPS)
    outs[0][:, :LANES] = (lo * inv * gq_ref[:, :LANES]).astype(outs[0].dtype)
    outs[0][:, LANES:] = _rope_tile(hi * inv * gq_ref[:, LANES:], a_ref, b_ref).astype(outs[0].dtype)


def _epi_latent(ds, ex, outs, j):
    ck = ds[0]
    glat_ref, gkr_ref, real_ref, a_ref, b_ref = ex
    c_raw = ck[:, :KV_LORA]
    ms = jnp.mean(c_raw * c_raw, axis=-1, keepdims=True)
    outs[0][...] = c_raw * lax.rsqrt(ms + EPS) * glat_ref[...]
    krt = ck[:, KV_LORA:]
    ssq = jnp.sum(krt * krt * real_ref[...], axis=-1, keepdims=True)
    outs[1][...] = _rope_tile(krt * gkr_ref[...], a_ref, b_ref)
    outs[2][...] = jnp.broadcast_to(ssq, outs[2].shape)


def _epi_knope(ds, ex, outs, j):
    acc = ds[0]
    kr_ref, ssq_ref, gkn_ref = ex
    ssq_r = ssq_ref[:, 0:1]
    kr = kr_ref[...]
    lane = lax.broadcasted_iota(jnp.int32, outs[1].shape, 1)

    @pl.when(j == 0)
    def _():
        outs[1][...] = jnp.zeros(outs[1].shape, F32)

    ks_all = outs[1][...]
    for t in range(2):
        kn = acc[:, t * NOPE_DIM:(t + 1) * NOPE_DIM]
        ssq = jnp.sum(kn * kn, axis=-1, keepdims=True) + ssq_r
        ks = lax.rsqrt(ssq * (1.0 / QK_DIM) + EPS)
        outs[0][:, t * HEAD_PAD:t * HEAD_PAD + LANES] = (kn * gkn_ref[...] * ks).astype(outs[0].dtype)
        outs[0][:, t * HEAD_PAD + LANES:(t + 1) * HEAD_PAD] = (kr * ks).astype(outs[0].dtype)
        ks_all = jnp.where(lane == 2 * j + t, ks, ks_all)
    outs[1][...] = ks_all


def _memkv_kernel(x_ref, g_ref, wk_ref, wv_ref, gk_ref, k_out, v_out, h_s):
    j = pl.program_id(1)
    rows = x_ref.shape[0]
    chunk = 128

    @pl.when(j == 0)
    def _():
        def body(c, carry):
            r = pl.multiple_of(c * chunk, chunk)
            xf = x_ref[pl.ds(r, chunk), :]
            ms = jnp.mean(xf * xf, axis=-1, keepdims=True)
            h_s[pl.ds(r, chunk), :] = (xf * lax.rsqrt(ms + EPS) * g_ref[...]).astype(BF)
            return carry
        lax.fori_loop(0, rows // chunk, body, 0)

    h = h_s[...]
    k = jnp.dot(h, wk_ref[...].astype(BF), preferred_element_type=F32)
    ms = jnp.mean(k * k, axis=-1, keepdims=True)
    k_out[...] = k * lax.rsqrt(ms + EPS) * gk_ref[...]
    v_out[...] = jnp.dot(h, wv_ref[...].astype(BF), preferred_element_type=F32)


def _memory_kv_all(mem2d, g_mem, w_mem_kv, g_mk):
    depth = w_mem_kv.shape[0]
    rows = mem2d.shape[0]
    hd = MEM_HEAD_DIM
    out = pl.pallas_call(
        _memkv_kernel,
        grid=(depth, MEM_HEADS),
        in_specs=[
            pl.BlockSpec((rows, D_MODEL), lambda l, j: (0, 0)),
            pl.BlockSpec((None, 1, D_MODEL), lambda l, j: (l, 0, 0)),
            pl.BlockSpec((None, D_MODEL, hd), lambda l, j: (l, 0, j)),
            pl.BlockSpec((None, D_MODEL, hd), lambda l, j: (l, 0, MEM_HEADS + j)),
            pl.BlockSpec((None, 1, hd), lambda l, j: (l, 0, 0)),
        ],
        out_specs=[
            pl.BlockSpec((None, rows, hd), lambda l, j: (l, 0, j)),
            pl.BlockSpec((None, rows, hd), lambda l, j: (l, 0, j)),
        ],
        out_shape=[jax.ShapeDtypeStruct((depth, rows, MEM_W), F32)] * 2,
        scratch_shapes=[pltpu.VMEM((rows, D_MODEL), BF)],
        compiler_params=_params(("arbitrary", "arbitrary")),
        name="memory_kv",
    )(mem2d, g_mem.reshape(depth, 1, D_MODEL), w_mem_kv, w_mem_kv, g_mk.reshape(depth, 1, hd))
    return out


CONV_TT = 256
CONV_RC = 32
CONV_LW = 256
CONV_NORM_RC = 64
CONV_HALO = 32
SUBLANES = 8


def _conv_prompt_kernel(a_ref, w_ref, cb_ref, lg_ref, lb_ref, o_ref, sh, ybuf):
    t = pl.program_id(1)
    tt = CONV_TT
    span = tt + CONV_HALO

    @pl.when(t == 0)
    def _():
        sh[0, 0:CONV_HALO, :] = jnp.zeros((CONV_HALO, D_MODEL), F32)
        sh[0, span:span + SUBLANES, :] = jnp.zeros((SUBLANES, D_MODEL), F32)

    sh[0, CONV_HALO:span, :] = a_ref[...]
    for p in range(1, SUBLANES):
        sh[p, 0:span, :] = sh[0, p:p + span, :]

    groups = CONV_RC // SUBLANES

    def conv_body(c, carry):
        r = c * CONV_RC
        for lq in range(D_MODEL // CONV_LW):
            ls = slice(lq * CONV_LW, (lq + 1) * CONV_LW)
            accs = [jnp.zeros((SUBLANES, CONV_LW), F32) for _ in range(groups)]
            for p in range(SUBLANES):
                slabs = {}
                for m in range(CONV_W // SUBLANES + 2):
                    k = p - 2 + SUBLANES * m
                    if not 0 <= k < CONV_W:
                        continue
                    w = w_ref[k * SUBLANES:(k + 1) * SUBLANES, ls]
                    for g in range(groups):
                        j = m + g
                        if j not in slabs:
                            start = pl.multiple_of(r + SUBLANES * j, SUBLANES)
                            slabs[j] = sh[p, pl.ds(start, SUBLANES), ls]
                        accs[g] = accs[g] + slabs[j] * w
            ybuf[pl.ds(pl.multiple_of(r, CONV_RC), CONV_RC), ls] = jnp.concatenate(accs, axis=0)
        return carry

    lax.fori_loop(0, tt // CONV_RC, conv_body, 0)

    def norm_body(c, carry):
        rows = pl.ds(pl.multiple_of(c * CONV_NORM_RC, CONV_NORM_RC), CONV_NORM_RC)
        y = ybuf[rows, :] + cb_ref[...]
        mu = jnp.mean(y, axis=-1, keepdims=True)
        yc = y - mu
        var = jnp.mean(yc * yc, axis=-1, keepdims=True)
        z = yc * lax.rsqrt(var + EPS) * lg_ref[...] + lb_ref[...]
        o_ref[rows, :] = (z * jax.nn.sigmoid(z)).astype(o_ref.dtype)
        return carry

    lax.fori_loop(0, tt // CONV_NORM_RC, norm_body, 0)
    sh[0, 0:CONV_HALO, :] = sh[0, tt:span, :]


def _conv_prompt(a, w32, cb, lg, lb, batch):
    m = a.shape[0]
    nt = SEQ // CONV_TT
    vec = lambda v: v.reshape(1, D_MODEL)
    w_rep = jnp.repeat(w32, SUBLANES, axis=0)
    return pl.pallas_call(
        _conv_prompt_kernel,
        grid=(batch, nt),
        in_specs=[
            pl.BlockSpec((CONV_TT, D_MODEL), lambda b, t: (b * nt + t, 0)),
            pl.BlockSpec((32 * SUBLANES, D_MODEL), lambda b, t: (0, 0)),
            pl.BlockSpec((1, D_MODEL), lambda b, t: (0, 0)),
            pl.BlockSpec((1, D_MODEL), lambda b, t: (0, 0)),
            pl.BlockSpec((1, D_MODEL), lambda b, t: (0, 0)),
        ],
        out_specs=pl.BlockSpec((CONV_TT, D_MODEL), lambda b, t: (b * nt + t, 0)),
        out_shape=jax.ShapeDtypeStruct((m, D_MODEL), BF),
        scratch_shapes=[pltpu.VMEM((SUBLANES, CONV_TT + CONV_HALO + SUBLANES, D_MODEL), F32),
                        pltpu.VMEM((CONV_TT, D_MODEL), F32)],
        compiler_params=_params(("arbitrary", "arbitrary")),
        name="conv_prompt",
    )(a, w_rep, vec(cb), vec(lg), vec(lb))


CONV_SB = 16


def _conv_sample_kernel(st_ref, a_ref, w_ref, cb_ref, lg_ref, lb_ref, y_ref, ns_ref):
    nstate = CONV_W - 1
    a = a_ref[...]
    acc = a * w_ref[nstate:nstate + 1, :]
    for k in range(nstate):
        acc = acc + st_ref[k] * w_ref[k:k + 1, :]
    y = acc + cb_ref[...]
    mu = jnp.mean(y, axis=-1, keepdims=True)
    yc = y - mu
    var = jnp.mean(yc * yc, axis=-1, keepdims=True)
    z = yc * lax.rsqrt(var + EPS) * lg_ref[...] + lb_ref[...]
    y_ref[...] = (z * jax.nn.sigmoid(z)).astype(y_ref.dtype)
    for k in range(nstate - 1):
        ns_ref[k] = st_ref[k + 1]
    ns_ref[nstate - 1] = a


def _conv_sample(state_t, layer, a, w32, cb, lg, lb):
    nstate, nb = state_t.shape[1], state_t.shape[2]
    vec = lambda v: v.reshape(1, D_MODEL)
    return pl.pallas_call(
        _conv_sample_kernel,
        grid=(nb // CONV_SB,),
        in_specs=[
            pl.BlockSpec((None, nstate, CONV_SB, D_MODEL), lambda i: (layer, 0, i, 0)),
            pl.BlockSpec((CONV_SB, D_MODEL), lambda i: (i, 0)),
            pl.BlockSpec((32, D_MODEL), lambda i: (0, 0)),
            pl.BlockSpec((1, D_MODEL), lambda i: (0, 0)),
            pl.BlockSpec((1, D_MODEL), lambda i: (0, 0)),
            pl.BlockSpec((1, D_MODEL), lambda i: (0, 0)),
        ],
        out_specs=[
            pl.BlockSpec((CONV_SB, D_MODEL), lambda i: (i, 0)),
            pl.BlockSpec((nstate, CONV_SB, D_MODEL), lambda i: (0, i, 0)),
        ],
        out_shape=[jax.ShapeDtypeStruct((nb, D_MODEL), BF),
                   jax.ShapeDtypeStruct((nstate, nb, D_MODEL), F32)],
        compiler_params=_params(("arbitrary",)),
        name="conv_sample",
    )(state_t, a, w32, vec(cb), vec(lg), vec(lb))


MEM_TQ = 512
MEM_SCALE = MEM_HEAD_DIM ** -0.5


def _memattn_prompt_kernel(q_ref, k_ref, v_ref, g_ref, o_ref):
    for h in range(MEM_HEADS):
        sl = slice(h * MEM_HEAD_DIM, (h + 1) * MEM_HEAD_DIM)
        qh = q_ref[:, sl]
        ms = jnp.mean(qh * qh, axis=-1, keepdims=True)
        qn = (qh * lax.rsqrt(ms + EPS) * g_ref[...]).astype(BF)
        s = lax.dot_general(qn, k_ref[:, sl].astype(BF), NT_DIMS,
                            preferred_element_type=F32) * MEM_SCALE
        mx = jnp.max(s, axis=-1, keepdims=True)
        p = jnp.exp(s - mx)
        l = jnp.sum(p, axis=-1, keepdims=True)
        o = jnp.dot(p.astype(BF), v_ref[:, sl].astype(BF), preferred_element_type=F32)
        o_ref[:, sl] = (o / l).astype(o_ref.dtype)


def _memattn_prompt(q, q_colblk, memk, memv, layer, g_mq_l, batch):
    m = q.shape[0]
    nq = SEQ // MEM_TQ
    return pl.pallas_call(
        _memattn_prompt_kernel,
        grid=(batch, nq),
        in_specs=[
            pl.BlockSpec((MEM_TQ, MEM_W), lambda b, i: (b * nq + i, q_colblk)),
            pl.BlockSpec((None, N_MEM, MEM_W), lambda b, i: (layer, b, 0)),
            pl.BlockSpec((None, N_MEM, MEM_W), lambda b, i: (layer, b, 0)),
            pl.BlockSpec((1, MEM_HEAD_DIM), lambda b, i: (0, 0)),
        ],
        out_specs=pl.BlockSpec((MEM_TQ, MEM_W), lambda b, i: (b * nq + i, 0)),
        out_shape=jax.ShapeDtypeStruct((m, MEM_W), BF),
        compiler_params=_params(("arbitrary", "arbitrary")),
        name="memattn_prompt",
    )(q, memk, memv, g_mq_l.reshape(1, MEM_HEAD_DIM))


MEM_SB = 4


def _memattn_sample_kernel(q_ref, k_ref, v_ref, g_ref, o_ref):
    for s in range(MEM_SB):
        q = q_ref[s]
        ms = jnp.mean(q * q, axis=-1, keepdims=True)
        qn = q * lax.rsqrt(ms + EPS) * g_ref[...]
        sc = jnp.sum(k_ref[s] * qn[None], axis=-1, keepdims=True) * MEM_SCALE
        mx = jnp.max(sc, axis=0, keepdims=True)
        p = jnp.exp(sc - mx)
        l = jnp.sum(p, axis=0)
        o = jnp.sum(p * v_ref[s], axis=0) / l
        o_ref[s] = o.astype(o_ref.dtype)


def _memattn_sample(q3, cmk, cmv, layer, g_mq_l):
    nb = q3.shape[0]
    kv_spec = pl.BlockSpec((None, MEM_SB, N_MEM, MEM_HEADS, MEM_HEAD_DIM), lambda i: (layer, i, 0, 0, 0))
    out = pl.pallas_call(
        _memattn_sample_kernel,
        grid=(nb // MEM_SB,),
        in_specs=[
            pl.BlockSpec((MEM_SB, MEM_HEADS, MEM_HEAD_DIM), lambda i: (i, 0, 0)),
            kv_spec, kv_spec,
            pl.BlockSpec((1, MEM_HEAD_DIM), lambda i: (0, 0)),
        ],
        out_specs=pl.BlockSpec((MEM_SB, MEM_HEADS, MEM_HEAD_DIM), lambda i: (i, 0, 0)),
        out_shape=jax.ShapeDtypeStruct((nb, MEM_HEADS, MEM_HEAD_DIM), BF),
        compiler_params=_params(("arbitrary",)),
        name="memattn_sample",
    )(q3, cmk, cmv, g_mq_l.reshape(1, MEM_HEAD_DIM))
    return out.reshape(nb, MEM_W)


FLASH_T = 512
FLASH_TK = 512
FLASH_HP = 2
MLA_SCALE = QK_DIM ** -0.5


def _flash_kernel(q_ref, k_ref, v_ref, o_ref):
    qi = pl.program_id(2)
    t, tk = FLASH_T, FLASH_TK
    ratio = t // tk
    row = lax.broadcasted_iota(jnp.int32, (t, tk), 0)
    col = lax.broadcasted_iota(jnp.int32, (t, tk), 1)

    def tile(ki, carry, hh, diag):
        m, l, acc = carry
        r = pl.multiple_of(ki * tk, tk)
        q = q_ref[:, hh * HEAD_PAD:(hh + 1) * HEAD_PAD]
        k = k_ref[pl.ds(r, tk), hh * HEAD_PAD:(hh + 1) * HEAD_PAD]
        s = lax.dot_general(q, k, NT_DIMS, preferred_element_type=F32) * MLA_SCALE
        if diag is not None:
            s = jnp.where(col + diag * tk <= row, s, NEG_BIG)
        m_new = jnp.maximum(m, jnp.max(s, axis=-1, keepdims=True))
        alpha = jnp.exp(m - m_new)
        p = jnp.exp(s - m_new)
        l = alpha * l + jnp.sum(p, axis=-1, keepdims=True)
        v = v_ref[pl.ds(r, tk), hh * V_DIM:(hh + 1) * V_DIM]
        acc = alpha * acc + jnp.dot(p.astype(BF), v, preferred_element_type=F32)
        return m_new, l, acc

    def tiles(ki, carries, diag):
        return tuple(tile(ki, carries[hh], hh, diag) for hh in range(FLASH_HP))

    init = tuple((jnp.full((t, 1), NEG_BIG, F32), jnp.zeros((t, 1), F32), jnp.zeros((t, V_DIM), F32))
                 for _ in range(FLASH_HP))
    carries = lax.fori_loop(0, qi * ratio, lambda ki, c: tiles(ki, c, None), init)
    for d in range(ratio):
        carries = tiles(qi * ratio + d, carries, d)
    for hh in range(FLASH_HP):
        m, l, acc = carries[hh]
        o_ref[:, hh * V_DIM:(hh + 1) * V_DIM] = (acc / l).astype(o_ref.dtype)


def _mla_prompt_attention(q, kfull, vfull, batch):
    m = q.shape[0]
    nq = SEQ // FLASH_T
    hp = FLASH_HP
    return pl.pallas_call(
        _flash_kernel,
        grid=(batch, MLA_HEADS // hp, nq),
        in_specs=[
            pl.BlockSpec((FLASH_T, hp * HEAD_PAD), lambda b, h, i: (b * nq + i, h)),
            pl.BlockSpec((SEQ, hp * HEAD_PAD), lambda b, h, i: (b, h)),
            pl.BlockSpec((SEQ, hp * V_DIM), lambda b, h, i: (b, h)),
        ],
        out_specs=pl.BlockSpec((FLASH_T, hp * V_DIM), lambda b, h, i: (b * nq + i, h)),
        out_shape=jax.ShapeDtypeStruct((m, MLA_HEADS * V_DIM), BF),
        compiler_params=_params(("arbitrary", "arbitrary", "arbitrary")),
        name="mla_prompt_attention",
    )(q, kfull, vfull)


def _headwise_kernel(x_ref, w_ref, g_ref, o_ref):
    x = (x_ref[...].astype(F32) * g_ref[...]).astype(BF)
    o_ref[...] = jnp.dot(x, w_ref[...].astype(BF), preferred_element_type=F32).astype(o_ref.dtype)


def _headwise_mm(name, x, x_spec, w, w_spec, g, out_shape, out_spec):
    kx = g.shape[-1]
    return pl.pallas_call(
        _headwise_kernel,
        grid=(MLA_HEADS,),
        in_specs=[x_spec, w_spec, pl.BlockSpec((1, kx), lambda h: (0, 0))],
        out_specs=out_spec,
        out_shape=out_shape,
        compiler_params=_params(("arbitrary",)),
        name=name,
    )(x, w, g)


PAGES_PER_STEP = 32


def _paged_kernel(pt_ref, ql_ref, q_ref, cn_ref, krn_ref, ksn_ref, lat_hbm, kr_hbm, ks_hbm, o_ref,
                  m_s, l_s, acc_s, cbuf, krbuf, ksbuf, lat_buf, kr_buf, ks_buf, sem):
    pp = PAGES_PER_STEP
    b = pl.program_id(0)
    c = pl.program_id(1)
    nb = pl.num_programs(0)
    nc = pl.num_programs(1)
    step = b * nc + c
    slot = lax.rem(step, 2)

    def fetch(bb, cc, s):
        for i in range(pp):
            pid = pt_ref[bb, cc * pp + i]
            pltpu.make_async_copy(lat_hbm.at[pid], lat_buf.at[s, i], sem.at[s]).start(priority=i % 2)
            pltpu.make_async_copy(kr_hbm.at[pid], kr_buf.at[s, i], sem.at[s]).start(priority=i % 2)
            pltpu.make_async_copy(ks_hbm.at[pid], ks_buf.at[s, i], sem.at[s]).start(priority=i % 2)

    def wait(s):
        pltpu.make_async_copy(lat_hbm.at[pl.ds(0, pp)], lat_buf.at[s], sem.at[s]).wait()
        pltpu.make_async_copy(kr_hbm.at[pl.ds(0, pp)], kr_buf.at[s], sem.at[s]).wait()
        pltpu.make_async_copy(ks_hbm.at[pl.ds(0, pp)], ks_buf.at[s], sem.at[s]).wait()

    @pl.when(step == 0)
    def _():
        fetch(b, c, 0)

    last = step == nb * nc - 1
    wrap = c == nc - 1
    nb_ = jnp.where(last, b, jnp.where(wrap, b + 1, b))
    nc_ = jnp.where(last, c, jnp.where(wrap, 0, c + 1))
    fetch(nb_, nc_, 1 - slot)
    wait(slot)

    @pl.when(c == 0)
    def _():
        m_s[...] = jnp.full(m_s.shape, NEG_BIG, F32)
        l_s[...] = jnp.zeros(l_s.shape, F32)
        acc_s[...] = jnp.zeros(acc_s.shape, F32)

    ql = ql_ref[0]
    qr32 = q_ref[0][:, LANES:LANES + ROPE_DIM]
    qr = qr32.astype(BF)
    for i in range(pp):
        rows = slice(i * PAGE_SIZE, (i + 1) * PAGE_SIZE)
        cbuf[rows, :] = lat_buf[slot, i].astype(BF)
        krbuf[:, rows] = kr_buf[slot, i].astype(BF)
        ksbuf[:, rows] = ks_buf[slot, i]
    keys = cbuf[...]
    s = (lax.dot_general(ql, keys, NT_DIMS, preferred_element_type=F32)
         + jnp.dot(qr, krbuf[...], preferred_element_type=F32))
    s = s * ksbuf[...] * MLA_SCALE
    m_old = m_s[...]
    m_new = jnp.maximum(m_old, jnp.max(s, axis=-1, keepdims=True))
    alpha = jnp.exp(m_old - m_new)
    p = jnp.exp(s - m_new)
    l_new = alpha * l_s[...] + jnp.sum(p, axis=-1, keepdims=True)
    pv = jnp.dot(p.astype(BF), keys, preferred_element_type=F32)
    acc_new = alpha * acc_s[...] + pv
    m_s[...] = m_new
    l_s[...] = l_new
    acc_s[...] = acc_new

    @pl.when(c == nc - 1)
    def _():
        cn = cn_ref[0]
        krn = krn_ref[0][:, 0:ROPE_DIM]
        s_new = (jnp.sum(ql.astype(F32) * cn, axis=-1, keepdims=True)
                 + jnp.sum(qr32 * krn, axis=-1, keepdims=True)) * ksn_ref[0] * MLA_SCALE
        m2 = jnp.maximum(m_new, s_new)
        a2 = jnp.exp(m_new - m2)
        p2 = jnp.exp(s_new - m2)
        l2 = a2 * l_new + p2
        o_ref[0] = ((a2 * acc_new + p2 * cn) / l2).astype(o_ref.dtype)

    @pl.when(last)
    def _():
        wait(1 - slot)


def _mla_sample_attention(page_table, qlat3, q3, cache_latent, cache_krope_t, cache_kscale_t,
                          cn3, krn3, ksn3):
    nb, n_pages = page_table.shape
    pp = PAGES_PER_STEP
    nc = n_pages // pp
    any_spec = pl.BlockSpec(memory_space=pl.ANY)
    in_specs = [
        pl.BlockSpec((1, MLA_HEADS, KV_LORA), lambda b, c, pt: (b, 0, 0)),
        pl.BlockSpec((1, MLA_HEADS, HEAD_PAD), lambda b, c, pt: (b, 0, 0)),
        pl.BlockSpec((1, 1, KV_LORA), lambda b, c, pt: (b, 0, 0)),
        pl.BlockSpec((1, 1, LANES), lambda b, c, pt: (b, 0, 0)),
        pl.BlockSpec((1, MLA_HEADS, 1), lambda b, c, pt: (b, 0, 0)),
        any_spec, any_spec, any_spec,
    ]
    grid_spec = pltpu.PrefetchScalarGridSpec(
        num_scalar_prefetch=1,
        grid=(nb, nc),
        in_specs=in_specs,
        out_specs=pl.BlockSpec((1, MLA_HEADS, KV_LORA), lambda b, c, pt: (b, 0, 0)),
        scratch_shapes=[pltpu.VMEM((MLA_HEADS, 1), F32), pltpu.VMEM((MLA_HEADS, 1), F32),
                        pltpu.VMEM((MLA_HEADS, KV_LORA), F32),
                        pltpu.VMEM((pp * PAGE_SIZE, KV_LORA), BF), pltpu.VMEM((ROPE_DIM, pp * PAGE_SIZE), BF),
                        pltpu.VMEM((MLA_HEADS, pp * PAGE_SIZE), F32),
                        pltpu.VMEM((2, pp, PAGE_SIZE, KV_LORA), F32),
                        pltpu.VMEM((2, pp, ROPE_DIM, PAGE_SIZE), F32),
                        pltpu.VMEM((2, pp, MLA_HEADS, PAGE_SIZE), F32),
                        pltpu.SemaphoreType.DMA((2,))],
    )
    return pl.pallas_call(
        _paged_kernel,
        grid_spec=grid_spec,
        out_shape=jax.ShapeDtypeStruct((nb, MLA_HEADS, KV_LORA), F32),
        compiler_params=_params(("arbitrary", "arbitrary")),
        name="mla_sample_attention",
    )(page_table, qlat3, q3, cn3, krn3, ksn3, cache_latent, cache_krope_t, cache_kscale_t)


def _router_kernel(xp_ref, xs_ref, g_ref, w_ref, h_ref, gate_ref, info_ref, cnt_ref, cnt_s, *, nbp):
    i = pl.program_id(0)

    @pl.when(i == 0)
    def _():
        cnt_s[...] = jnp.zeros(cnt_s.shape, F32)

    xf = jnp.where(i < nbp, xp_ref[...], xs_ref[...])
    ms = jnp.mean(xf * xf, axis=-1, keepdims=True)
    h = xf * lax.rsqrt(ms + EPS) * g_ref[...]
    hb = lax.bitcast_convert_type(h.astype(BF).astype(F32), jnp.uint32)
    h_ref[...] = (hb[:, :D_MODEL // 2] >> 16) | hb[:, D_MODEL // 2:]
    logits = jnp.dot(h.astype(BF), w_ref[...].astype(BF),
                     preferred_element_type=F32)
    lane = lax.broadcasted_iota(jnp.int32, logits.shape, 1)
    big = jnp.int32(LANES)

    is_g = lane < N_GROUPS
    lg = jnp.where(is_g, logits, NEG_BIG)
    eg = jnp.where(is_g, jnp.exp(lg - jnp.max(lg, axis=-1, keepdims=True)), 0.0)
    pg = eg / jnp.sum(eg, axis=-1, keepdims=True)
    p_grp = jnp.max(pg, axis=-1, keepdims=True)
    grp = jnp.min(jnp.where(is_g & (pg == p_grp), lane, big), axis=-1, keepdims=True)

    e_idx = lane - N_GROUPS
    sel = (e_idx >= 0) & (e_idx < N_EXPERTS) & ((e_idx >> 3) == grp)
    le = jnp.where(sel, logits, NEG_BIG)
    ee = jnp.where(sel, jnp.exp(le - jnp.max(le, axis=-1, keepdims=True)), 0.0)
    pe = ee / jnp.sum(ee, axis=-1, keepdims=True)
    top1 = jnp.max(jnp.where(sel, pe, -1.0), axis=-1, keepdims=True)
    i1 = jnp.min(jnp.where(sel & (pe == top1), lane, big), axis=-1, keepdims=True)
    rest = sel & (lane != i1)
    top2 = jnp.max(jnp.where(rest, pe, -1.0), axis=-1, keepdims=True)
    i2 = jnp.min(jnp.where(rest & (pe == top2), lane, big), axis=-1, keepdims=True)
    denom = top1 + top2
    g1 = p_grp * top1 / denom
    g2 = p_grp * top2 / denom
    gate_ref[...] = jnp.where(lane == 0, g1, jnp.where(lane == 1, g2, 0.0))

    onehot = ((lane == i1) | (lane == i2)).astype(F32)
    r_i = lax.broadcasted_iota(jnp.int32, (MOE_BLOCK, MOE_BLOCK), 0)
    c_i = lax.broadcasted_iota(jnp.int32, (MOE_BLOCK, MOE_BLOCK), 1)
    tri = (c_i < r_i).astype(BF)
    before = jnp.dot(tri, onehot.astype(BF), preferred_element_type=F32) + cnt_s[...]
    rank1 = jnp.sum(jnp.where(lane == i1, before, 0.0), axis=-1, keepdims=True).astype(jnp.int32)
    rank2 = jnp.sum(jnp.where(lane == i2, before, 0.0), axis=-1, keepdims=True).astype(jnp.int32)
    info_ref[...] = jnp.where(lane == 0, i1 - N_GROUPS, jnp.where(lane == 1, i2 - N_GROUPS,
                              jnp.where(lane == 2, rank1, jnp.where(lane == 3, rank2, 0))))
    cnt_new = cnt_s[...] + jnp.sum(onehot, axis=0, keepdims=True)
    cnt_s[...] = cnt_new
    cnt_ref[...] = jnp.broadcast_to(cnt_new, cnt_ref.shape)


def _router(xp, xs, g, w_router):
    nbp = xp.shape[0] // MOE_BLOCK
    assert xs.shape[0] == MOE_BLOCK
    mt = xp.shape[0] + xs.shape[0]
    tok_spec = lambda width: pl.BlockSpec((MOE_BLOCK, width), lambda i: (i, 0))
    return pl.pallas_call(
        functools.partial(_router_kernel, nbp=nbp),
        grid=(nbp + 1,),
        in_specs=[
            pl.BlockSpec((MOE_BLOCK, D_MODEL), lambda i: (jnp.minimum(i, nbp - 1), 0)),
            pl.BlockSpec((MOE_BLOCK, D_MODEL), lambda i: (0, 0)),
            pl.BlockSpec((1, D_MODEL), lambda i: (0, 0)),
            pl.BlockSpec((D_MODEL, LANES), lambda i: (0, 0)),
        ],
        out_specs=[tok_spec(D_MODEL // 2), tok_spec(LANES), tok_spec(LANES),
                   pl.BlockSpec((SUBLANES, LANES), lambda i: (0, 0))],
        out_shape=[jax.ShapeDtypeStruct((mt, D_MODEL // 2), jnp.uint32),
                   jax.ShapeDtypeStruct((mt, LANES), F32),
                   jax.ShapeDtypeStruct((mt, LANES), jnp.int32),
                   jax.ShapeDtypeStruct((SUBLANES, LANES), F32)],
        scratch_shapes=[pltpu.VMEM((1, LANES), F32)],
        compiler_params=_params(("arbitrary",)),
        name="moe_router",
    )(xp, xs, g.reshape(1, D_MODEL), w_router)


TRASH_ROWS = 2 * MOE_BLOCK


def _expert_kernel(be_ref, nu_ref, par_ref, nxt_ref, nch_ref, src_ref, srcn_ref, dst_ref,
                   h_ref, wg_hbm, wu_hbm, wd_hbm, ys_ref,
                   xbuf, ybuf, wg_f, wu_f, wd_f, wg_s, wu_s, wd_s, gsem, ssem, wsem, *, layer, n_slots):
    b = pl.program_id(0)
    nu = nu_ref[0]
    rows = MOE_BLOCK
    nblk = pl.num_programs(0)
    unroll = SUBLANES

    def gather_start(tab_ref, slot, nchunks):
        def body(c, carry):
            for u in range(unroll):
                t = c * unroll + u
                pltpu.make_async_copy(h_ref.at[pl.ds(tab_ref[0, 0, t], 1)], xbuf.at[slot, pl.ds(t, 1)],
                                      gsem.at[slot]).start(priority=u % 2)
            return carry
        lax.fori_loop(0, nchunks, body, 0)

    def gather_wait(slot, nchunks):
        def body(c, carry):
            pltpu.make_async_copy(h_ref.at[pl.ds(0, unroll)], xbuf.at[slot, pl.ds(0, unroll)],
                                  gsem.at[slot]).wait()
            return carry
        lax.fori_loop(0, nchunks, body, 0)

    def weight_copies(e, slot):
        return [pltpu.make_async_copy(w.at[layer, e], f.at[slot], wsem.at[slot])
                for w, f in ((wg_hbm, wg_f), (wu_hbm, wu_f), (wd_hbm, wd_f))]

    def scatter_wait(nchunks):
        def body(c, carry):
            pltpu.make_async_copy(ybuf.at[pl.ds(0, unroll)], ys_ref.at[pl.ds(0, unroll)], ssem).wait()
            return carry
        lax.fori_loop(0, nchunks, body, 0)

    @pl.when(b < nu)
    def _():
        slot = lax.rem(b, 2)

        @pl.when(b == 0)
        def _():
            xbuf[...] = jnp.zeros(xbuf.shape, xbuf.dtype)
            gather_start(src_ref, 0, nch_ref[0])

        @pl.when(b + 1 < nu)
        def _():
            gather_start(srcn_ref, 1 - slot, nch_ref[jnp.minimum(b + 1, nblk - 1)])

        e = be_ref[b]
        p = par_ref[b]
        fresh = (b == 0) | (e != be_ref[jnp.maximum(b - 1, 0)])

        @pl.when(fresh)
        def _():
            @pl.when(b == 0)
            def _():
                for c in weight_copies(e, p):
                    c.start(priority=1)

            for c in weight_copies(e, p):
                c.wait()
            nxt = nxt_ref[b]

            @pl.when(nxt >= 0)
            def _():
                for c in weight_copies(nxt, 1 - p):
                    c.start(priority=1)

            wg_s[...] = wg_f[p].astype(BF)
            wu_s[...] = wu_f[p].astype(BF)
            wd_s[...] = wd_f[p].astype(BF)

        gather_wait(slot, nch_ref[b])
        packed = xbuf[slot]
        x = jnp.concatenate(
            [lax.bitcast_convert_type(packed << 16, F32).astype(BF),
             lax.bitcast_convert_type(packed & jnp.uint32(0xFFFF0000), F32).astype(BF)], axis=1)
        a = jnp.dot(x, wg_s[...], preferred_element_type=F32)
        u = jnp.dot(x, wu_s[...], preferred_element_type=F32)
        hmid = (a * jax.nn.sigmoid(a) * u).astype(BF)

        @pl.when(b >= 1)
        def _():
            scatter_wait(nch_ref[jnp.maximum(b - 1, 0)])

        ybuf[...] = jnp.dot(hmid, wd_s[...], preferred_element_type=F32)

        def scatter_body(c, carry):
            for u in range(unroll):
                t = c * unroll + u
                pltpu.make_async_copy(ybuf.at[pl.ds(t, 1)], ys_ref.at[pl.ds(dst_ref[0, 0, t], 1)],
                                      ssem).start(priority=u % 2)
            return carry
        lax.fori_loop(0, nch_ref[b], scatter_body, 0)

        @pl.when(b == nu - 1)
        def _():
            scatter_wait(nch_ref[b])
            ybuf[...] = jnp.zeros(ybuf.shape, F32)
            for c in range(TRASH_ROWS // rows):
                cp = pltpu.make_async_copy(ybuf, ys_ref.at[pl.ds(n_slots + c * rows, rows)], ssem)
                cp.start()
                cp.wait()


def _experts(blk_e, n_used, par, nxt, nch, row_src3, row_dst3, h, w_gate, w_up, w_down, layer, n_slots):
    nblk = row_src3.shape[0]
    tab_spec = lambda f: pl.BlockSpec((1, 1, MOE_BLOCK), f, memory_space=pltpu.SMEM)
    any_spec = pl.BlockSpec(memory_space=pl.ANY)
    grid_spec = pltpu.PrefetchScalarGridSpec(
        num_scalar_prefetch=5,
        grid=(nblk,),
        in_specs=[
            tab_spec(lambda b, *_: (b, 0, 0)),
            tab_spec(lambda b, *_: (jnp.minimum(b + 1, nblk - 1), 0, 0)),
            tab_spec(lambda b, *_: (b, 0, 0)),
            any_spec, any_spec, any_spec, any_spec,
        ],
        out_specs=any_spec,
        scratch_shapes=[
            pltpu.VMEM((2, MOE_BLOCK, D_MODEL // 2), jnp.uint32), pltpu.VMEM((MOE_BLOCK, D_MODEL), F32),
            pltpu.VMEM((2, D_MODEL, D_EXPERT), F32), pltpu.VMEM((2, D_MODEL, D_EXPERT), F32),
            pltpu.VMEM((2, D_EXPERT, D_MODEL), F32),
            pltpu.VMEM((D_MODEL, D_EXPERT), BF), pltpu.VMEM((D_MODEL, D_EXPERT), BF),
            pltpu.VMEM((D_EXPERT, D_MODEL), BF),
            pltpu.SemaphoreType.DMA((2,)), pltpu.SemaphoreType.DMA(()), pltpu.SemaphoreType.DMA((2,)),
        ],
    )
    return pl.pallas_call(
        functools.partial(_expert_kernel, layer=layer, n_slots=n_slots),
        grid_spec=grid_spec,
        out_shape=jax.ShapeDtypeStruct((n_slots + TRASH_ROWS, D_MODEL), F32),
        compiler_params=_params(("arbitrary",)),
        name="moe_experts",
    )(blk_e, n_used, par, nxt, nch, row_src3, row_src3, row_dst3, h, w_gate, w_up, w_down)


def _combine_kernel(gate_ref, xp_ref, xs_ref, y0_ref, y1_ref, op_ref, os_ref, *, nbp):
    i = pl.program_id(0)
    y = y0_ref[...] * gate_ref[:, 0:1] + y1_ref[...] * gate_ref[:, 1:2]

    @pl.when(i < nbp)
    def _():
        op_ref[...] = xp_ref[...] + y

    @pl.when(i == nbp)
    def _():
        os_ref[...] = xs_ref[...] + y


def _combine(gate, xp, xs, ys):
    nbp = xp.shape[0] // MOE_BLOCK
    nb_tok = nbp + 1
    p_spec = pl.BlockSpec((MOE_BLOCK, D_MODEL), lambda i: (jnp.minimum(i, nbp - 1), 0))
    s_spec = pl.BlockSpec((MOE_BLOCK, D_MODEL), lambda i: (0, 0))
    return pl.pallas_call(
        functools.partial(_combine_kernel, nbp=nbp),
        grid=(nb_tok,),
        in_specs=[
            pl.BlockSpec((MOE_BLOCK, LANES), lambda i: (i, 0)),
            p_spec, s_spec,
            pl.BlockSpec((MOE_BLOCK, D_MODEL), lambda i: (i, 0)),
            pl.BlockSpec((MOE_BLOCK, D_MODEL), lambda i: (nb_tok + i, 0)),
        ],
        out_specs=[p_spec, s_spec],
        out_shape=[jax.ShapeDtypeStruct(xp.shape, F32), jax.ShapeDtypeStruct(xs.shape, F32)],
        compiler_params=_params(("arbitrary",)),
        name="moe_combine",
    )(gate, xp, xs, ys, ys)


def _moe(xp, xs, g_ffn_l, w_router, w_gate, w_up, w_down, layer):
    mt = xp.shape[0] + xs.shape[0]
    n_slots = 2 * mt
    nblk = (n_slots + N_EXPERTS * (MOE_BLOCK - 1) + MOE_BLOCK - 1) // MOE_BLOCK
    h, gate, info, cnt = _router(xp, xs, g_ffn_l, w_router)

    ids = jnp.arange(N_EXPERTS, dtype=jnp.int32)
    counts = cnt[0, N_GROUPS:N_GROUPS + N_EXPERTS].astype(jnp.int32)
    used = counts > 0
    padded = (counts + MOE_BLOCK - 1) // MOE_BLOCK * MOE_BLOCK
    pad_ends = jnp.cumsum(padded)
    pad_starts = pad_ends - padded
    n_used = (pad_ends[-1] // MOE_BLOCK).astype(jnp.int32)
    blk_ids = jnp.arange(nblk, dtype=jnp.int32)
    blk_e = jnp.sum((pad_ends[None, :] <= blk_ids[:, None] * MOE_BLOCK).astype(jnp.int32), axis=1)
    last_e = jnp.max(jnp.where(used, ids, 0))
    blk_e = jnp.where(blk_ids < n_used, jnp.minimum(blk_e, N_EXPERTS - 1), last_e).astype(jnp.int32)
    ordinal = jnp.cumsum(used.astype(jnp.int32)) - 1
    next_e = jnp.min(jnp.where((ids[None, :] > ids[:, None]) & used[None, :], ids[None, :], N_EXPERTS), axis=1)
    next_e = jnp.where(next_e == N_EXPERTS, -1, next_e)
    blk_onehot = blk_e[:, None] == ids[None, :]
    par = jnp.sum(jnp.where(blk_onehot, (ordinal % 2)[None, :], 0), axis=1).astype(jnp.int32)
    nxt = jnp.sum(jnp.where(blk_onehot, next_e[None, :], 0), axis=1).astype(jnp.int32)
    seg_end = jnp.sum(jnp.where(blk_onehot, (pad_starts + counts)[None, :], 0), axis=1)
    rows_valid = jnp.where(blk_ids < n_used, jnp.clip(seg_end - blk_ids * MOE_BLOCK, 0, MOE_BLOCK), 0)
    nch = ((rows_valid + SUBLANES - 1) // SUBLANES).astype(jnp.int32)

    e2, rank2 = info[:, 0:2], info[:, 2:4]
    start2 = jnp.sum(jnp.where(e2[:, :, None] == ids[None, None, :], pad_starts[None, None, :], 0), axis=-1)
    dest = start2 + rank2
    slot_code = jnp.arange(mt, dtype=jnp.int32)[:, None] + jnp.array([[0, mt]], jnp.int32)
    rows = jnp.arange(nblk * MOE_BLOCK, dtype=jnp.int32)
    row_dst = (n_slots + rows % TRASH_ROWS).at[dest.reshape(-1)].set(slot_code.reshape(-1))
    row_src = jnp.where(row_dst < n_slots, row_dst % mt, 0)
    shape3 = (nblk, 1, MOE_BLOCK)

    ys = _experts(blk_e, n_used.reshape(1), par, nxt, nch, row_src.reshape(shape3), row_dst.reshape(shape3),
                  h, w_gate, w_up, w_down, layer, n_slots)
    return _combine(gate, xp, xs, ys)


def _rope_tables(pos):
    half = ROPE_DIM // 2
    inv = ROPE_THETA ** (-jnp.arange(half, dtype=F32) / half)
    ang = pos.astype(F32)[:, None] * inv[None, :]
    cos, sin = jnp.cos(ang), jnp.sin(ang)
    z = jnp.zeros_like(cos)
    a = jnp.concatenate([cos, cos, z, z], axis=1)
    b = jnp.concatenate([-sin, sin, z, z], axis=1)
    return a, b


def _out_spec(tm, tn):
    return pl.BlockSpec((tm, tn), lambda i, j: (i, j))


def _table_specs(tabs, tm, nper):
    return [(t, pl.BlockSpec((tm, LANES), lambda i, j, nper=nper: (i % nper, 0))) for t in tabs]


def _vec_extra(v):
    n = v.shape[-1]
    return (v.reshape(1, n), pl.BlockSpec((1, n), lambda i, j: (0, 0)))


def kernel(x_prompt, x_sample, cache_latent, cache_krope, cache_kscale, cache_mem_k, cache_mem_v, state_conv, page_table, mem_prompt, g_mix, g_ffn, g_mem, w_mem_kv, g_mq, g_mk, w_in_a, conv_w, conv_b, ln_g, ln_b, w_out_a, g_kv_in, w_kv_down, g_kv_lat, w_uk, w_uv, g_k, w_in_b, g_qlat, w_q_up, g_q, w_out_b, w_rg, w_re, w_gate, w_up, w_down):
    batch, seq, _ = x_prompt.shape
    nb = x_sample.shape[0]
    depth = g_mix.shape[0]
    n_a = w_in_a.shape[0]
    mp = batch * seq
    tmp = 1024
    xp = x_prompt.reshape(mp, D_MODEL)
    xs = x_sample.reshape(nb, D_MODEL)

    tabs_p = _rope_tables(jnp.arange(seq))
    tabs_s = _rope_tables(jnp.full((nb,), PAST_LEN))
    w_kvd = jnp.concatenate([w_kv_down, w_kv_down[:, KV_LORA:]], axis=1)
    gk_n = g_k[:NOPE_DIM]
    gk_r = jnp.concatenate([g_k[NOPE_DIM:], g_k[NOPE_DIM:]])
    real_lanes = jnp.concatenate([jnp.ones((ROPE_DIM,), F32), jnp.zeros((LANES - ROPE_DIM,), F32)])
    w_uk2 = w_uk.reshape(KV_LORA, MLA_HEADS * NOPE_DIM)
    w_uk_t = jnp.transpose(w_uk, (1, 2, 0))
    w_uv2 = w_uv.reshape(KV_LORA, MLA_HEADS * V_DIM)
    w_qu = jnp.concatenate([w_q_up, w_q_up[..., NOPE_DIM:]], axis=-1)
    w_qu = w_qu.reshape(w_q_up.shape[0], Q_LORA, MLA_HEADS * HEAD_PAD)
    gq_pad = jnp.concatenate([g_q, g_q[:, NOPE_DIM:]], axis=-1)
    w_router = jnp.pad(jnp.concatenate([w_rg, w_re], axis=-1),
                       ((0, 0), (0, 0), (0, LANES - N_GROUPS - N_EXPERTS)))
    conv_w32 = jnp.pad(conv_w, ((0, 0), (0, 32 - CONV_W), (0, 0)))
    cache_kscale_t = jnp.transpose(cache_kscale, (0, 2, 1))
    cache_krope_t = jnp.transpose(cache_krope, (0, 2, 1))
    state_conv_t = jnp.transpose(state_conv, (0, 2, 1, 3))

    memk, memv = _memory_kv_all(mem_prompt.reshape(batch * N_MEM, D_MODEL), g_mem, w_mem_kv, g_mk)

    conv_p_list, conv_s_list = [], []
    shared = None
    for l in range(depth):
        if l < n_a:
            n_glu = D_MODEL
            outs_p, outs_s = [], []
            for x, m, tm, dst in ((xp, mp, tmp, outs_p), (xs, nb, nb, outs_s)):
                tn = 256
                ng = n_glu // tn
                glu_col = lambda j, ng=ng: jnp.minimum(j, ng - 1)
                a, qm = _fused_mm(
                    "in_a", [(x, D_MODEL, 0)],
                    [(0, w_in_a, l, D_MODEL, tn, 0, glu_col),
                     (0, w_in_a, l, D_MODEL, tn, 0, lambda j, ng=ng: ng + jnp.minimum(j, ng - 1)),
                     (0, w_in_a, l, D_MODEL, tn, 0, lambda j, ng=ng: 2 * ng + jnp.maximum(j - ng, 0))],
                    m=m, tm=tm, nj=ng + MEM_W // tn, gain=g_mix[l],
                    outs=[(jax.ShapeDtypeStruct((m, n_glu), F32),
                           pl.BlockSpec((tm, tn), lambda i, j, ng=ng: (i, jnp.minimum(j, ng - 1)))),
                          (jax.ShapeDtypeStruct((m, MEM_W), F32),
                           pl.BlockSpec((tm, tn), lambda i, j, ng=ng: (i, jnp.maximum(j - ng, 0))))],
                    epi=functools.partial(_epi_glu_memq, ng=ng))
                dst.extend([a, qm])
            a_p, qm_p = outs_p
            a_s, qm_s = outs_s
            yc_p = _conv_prompt(a_p, conv_w32[l], conv_b[l], ln_g[l], ln_b[l], batch)
            conv_p_list.append(a_p.reshape(batch, seq, D_MODEL)[:, seq - (CONV_W - 1):])
            yc_s, ns = _conv_sample(state_conv_t, l, a_s, conv_w32[l], conv_b[l], ln_g[l], ln_b[l])
            conv_s_list.append(ns)
            ym_p = _memattn_prompt(qm_p, 0, memk, memv, l, g_mq[l], batch)
            ym_s = _memattn_sample(qm_s.reshape(nb, MEM_HEADS, MEM_HEAD_DIM), cache_mem_k, cache_mem_v,
                                   l, g_mq[l])
            new = []
            for x, m, tm, y1, y2 in ((xp, mp, tmp, yc_p, ym_p), (xs, nb, nb, yc_s, ym_s)):
                tn = 512
                new.append(_fused_mm(
                    "out_a", [(y1, D_MODEL, 0), (y2, MEM_W, 0)],
                    [(0, w_out_a, l, D_MODEL, tn, 0, 0), (1, w_out_a, l, MEM_W, tn, D_MODEL // MEM_W, 0)],
                    m=m, tm=tm, nj=D_MODEL // tn,
                    extras=[(x, _out_spec(tm, tn))],
                    outs=[(jax.ShapeDtypeStruct((m, D_MODEL), F32), _out_spec(tm, tn))],
                    epi=_epi_residual)[0])
            xp, xs = new
        else:
            jb = l - n_a
            kfull, vfull, c_s, kr_s, ks_s = shared
            att = []
            for x, m, tm, tabs, nper, qdt in ((xp, mp, tmp, tabs_p, seq // tmp, BF),
                                              (xs, nb, nb, tabs_s, 1, F32)):
                tn = Q_LORA
                n_in = Q_LORA + MEM_W
                u, qm = _fused_mm(
                    "in_b", [(x, D_MODEL, 0)], [(0, w_in_b, jb, D_MODEL, tn, 0, 0)],
                    m=m, tm=tm, nj=n_in // tn, gain=g_mix[l],
                    outs=[(jax.ShapeDtypeStruct((m, Q_LORA), F32),
                           pl.BlockSpec((tm, tn), lambda i, j: (i, 0))),
                          (jax.ShapeDtypeStruct((m, MEM_W), F32),
                           pl.BlockSpec((tm, tn), lambda i, j: (i, jnp.maximum(j - 1, 0))))],
                    epi=_epi_split)
                qh = _fused_mm(
                    "q_up", [(u, Q_LORA, 0)], [(0, w_qu, jb, Q_LORA, HEAD_PAD, 0, 0)],
                    m=m, tm=tm, nj=MLA_HEADS, gain=g_qlat[jb],
                    extras=[_vec_extra(gq_pad[jb]), _vec_extra(real_lanes)] + _table_specs(tabs, tm, nper),
                    outs=[(jax.ShapeDtypeStruct((m, MLA_HEADS * HEAD_PAD), qdt), _out_spec(tm, HEAD_PAD))],
                    epi=_epi_qhead)[0]
                att.append((qm, qh))
            (qm_p, qh_p), (qm_s, qh_s) = att
            att_p = _mla_prompt_attention(qh_p, kfull, vfull, batch)
            ym_p = _memattn_prompt(qm_p, 0, memk, memv, l, g_mq[l], batch)
            ym_s = _memattn_sample(qm_s.reshape(nb, MEM_HEADS, MEM_HEAD_DIM), cache_mem_k, cache_mem_v,
                                   l, g_mq[l])

            qlat = _headwise_mm(
                "q_latent", qh_s, pl.BlockSpec((nb, NOPE_DIM), lambda h: (0, 2 * h)),
                w_uk_t, pl.BlockSpec((None, NOPE_DIM, KV_LORA), lambda h: (h, 0, 0)),
                gk_n.reshape(1, NOPE_DIM),
                jax.ShapeDtypeStruct((nb, MLA_HEADS * KV_LORA), BF),
                pl.BlockSpec((nb, KV_LORA), lambda h: (0, h)))
            o_lat = _mla_sample_attention(
                page_table, qlat.reshape(nb, MLA_HEADS, KV_LORA),
                qh_s.reshape(nb, MLA_HEADS, HEAD_PAD), cache_latent, cache_krope_t, cache_kscale_t,
                c_s.reshape(nb, 1, KV_LORA), kr_s.reshape(nb, 1, LANES),
                ks_s[:, :MLA_HEADS].reshape(nb, MLA_HEADS, 1))
            att_s = _headwise_mm(
                "v_expand", o_lat.reshape(nb, MLA_HEADS * KV_LORA),
                pl.BlockSpec((nb, KV_LORA), lambda h: (0, h)),
                w_uv2, pl.BlockSpec((KV_LORA, V_DIM), lambda h: (0, h)),
                jnp.ones((1, KV_LORA), F32),
                jax.ShapeDtypeStruct((nb, MLA_HEADS * V_DIM), BF),
                pl.BlockSpec((nb, V_DIM), lambda h: (0, h)))
            new = []
            n_att = MLA_HEADS * V_DIM
            for x, m, tm, y1, y2 in ((xp, mp, tmp, att_p, ym_p), (xs, nb, nb, att_s, ym_s)):
                tn = 512
                new.append(_fused_mm(
                    "out_b", [(y1, n_att, 0), (y2, MEM_W, 0)],
                    [(0, w_out_b, jb, n_att, tn, 0, 0), (1, w_out_b, jb, MEM_W, tn, n_att // MEM_W, 0)],
                    m=m, tm=tm, nj=D_MODEL // tn,
                    extras=[(x, _out_spec(tm, tn))],
                    outs=[(jax.ShapeDtypeStruct((m, D_MODEL), F32), _out_spec(tm, tn))],
                    epi=_epi_residual)[0])
            xp, xs = new

        xp, xs = _moe(xp, xs, g_ffn[l], w_router[l], w_gate, w_up, w_down, l)

        if l == n_a - 1:
            lat = []
            for x, m, tm, tabs, nper in ((xp, mp, tmp, tabs_p, seq // tmp), (xs, nb, nb, tabs_s, 1)):
                nck = KV_LORA + LANES
                c, kr128, ssq128 = _fused_mm(
                    "kv_latent", [(x, D_MODEL, 0)], [(0, w_kvd, None, D_MODEL, nck, 0, 0)],
                    m=m, tm=tm, nj=1, gain=g_kv_in,
                    extras=[_vec_extra(g_kv_lat), _vec_extra(gk_r), _vec_extra(real_lanes)]
                    + _table_specs(tabs, tm, nper),
                    outs=[(jax.ShapeDtypeStruct((m, KV_LORA), F32), _out_spec(tm, KV_LORA)),
                          (jax.ShapeDtypeStruct((m, LANES), F32), pl.BlockSpec((tm, LANES), lambda i, j: (i, 0))),
                          (jax.ShapeDtypeStruct((m, LANES), F32), pl.BlockSpec((tm, LANES), lambda i, j: (i, 0)))],
                    epi=_epi_latent)
                kf, ks128 = _fused_mm(
                    "k_nope", [(c, KV_LORA, 0)], [(0, w_uk2, None, KV_LORA, 2 * NOPE_DIM, 0, 0)],
                    m=m, tm=tm, nj=MLA_HEADS // 2,
                    extras=[(kr128, pl.BlockSpec((tm, LANES), lambda i, j: (i, 0))),
                            (ssq128, pl.BlockSpec((tm, LANES), lambda i, j: (i, 0))),
                            _vec_extra(gk_n)],
                    outs=[(jax.ShapeDtypeStruct((m, MLA_HEADS * HEAD_PAD), BF), _out_spec(tm, 2 * HEAD_PAD)),
                          (jax.ShapeDtypeStruct((m, LANES), F32), pl.BlockSpec((tm, LANES), lambda i, j: (i, 0)))],
                    epi=_epi_knope)
                lat.append((c, kr128, ks128, kf))
            (c_p, kr_p, ks_p, kfull), (c_s, kr_s, ks_s, _) = lat
            vfull = _fused_mm(
                "v_full", [(c_p, KV_LORA, 0)], [(0, w_uv2, None, KV_LORA, 512, 0, 0)],
                m=mp, tm=tmp, nj=MLA_HEADS * V_DIM // 512,
                outs=[(jax.ShapeDtypeStruct((mp, MLA_HEADS * V_DIM), BF), _out_spec(tmp, 512))],
                epi=_epi_plain)[0]
            shared = (kfull, vfull, c_s, kr_s, ks_s)

    shape5 = (depth, batch, N_MEM, MEM_HEADS, MEM_HEAD_DIM)
    return (xp.reshape(batch, seq, D_MODEL), xs.reshape(nb, 1, D_MODEL),
            c_p.reshape(batch, seq, KV_LORA), kr_p[:, :ROPE_DIM].reshape(batch, seq, ROPE_DIM),
            ks_p[:, :MLA_HEADS].reshape(batch, seq, MLA_HEADS),
            memk.reshape(shape5), memv.reshape(shape5),
            jnp.stack(conv_p_list, axis=0),
            c_s.reshape(nb, 1, KV_LORA), kr_s[:, :ROPE_DIM].reshape(nb, 1, ROPE_DIM),
            ks_s[:, :MLA_HEADS].reshape(nb, 1, MLA_HEADS),
            jnp.transpose(jnp.stack(conv_s_list, axis=0), (0, 2, 1, 3)))
```

```python
import functools
import math

import jax
import jax.numpy as jnp
from jax import lax
from jax.experimental import pallas as pl
from jax.experimental.pallas import tpu as pltpu

F32 = jnp.float32
BF = jnp.bfloat16

D_MODEL = 2048
SEQ = 2048
PAST_LEN = 8192
PAGE_SIZE = 128
CONV_W = 31
N_MEM = 256
MEM_HEADS = 4
MEM_HEAD_DIM = 256
MEM_W = MEM_HEADS * MEM_HEAD_DIM
MLA_HEADS = 16
Q_LORA = 512
KV_LORA = 512
NOPE_DIM = 128
ROPE_DIM = 64
QK_DIM = NOPE_DIM + ROPE_DIM
V_DIM = 128
ROPE_THETA = 10000.0
N_GROUPS = 8
EXPERTS_PER_GROUP = 8
N_EXPERTS = 64
D_EXPERT = 512
MOE_BLOCK = 128
EPS = 1e-6

HEAD_PAD = 256
LANES = 128
VMEM_LIMIT_BYTES = 56 * 1024 * 1024
NEG_BIG = -1e30
NT_DIMS = (((1,), (1,)), ((), ()))


def _params(sem):
    return pltpu.CompilerParams(dimension_semantics=sem, vmem_limit_bytes=VMEM_LIMIT_BYTES)


def _mm_kernel(*refs, nx, w_x, has_gain, n_extra, n_out, epi, tm):
    x_refs = refs[:nx]
    w_refs = refs[nx:nx + len(w_x)]
    pos = nx + len(w_x)
    g_ref = refs[pos] if has_gain else None
    pos += int(has_gain)
    extra_refs = refs[pos:pos + n_extra]
    pos += n_extra
    out_refs = refs[pos:pos + n_out]
    pos += n_out
    h_s = refs[pos] if has_gain else None
    j = pl.program_id(1)

    if has_gain:
        chunk = min(tm, 128)

        @pl.when(j == 0)
        def _():
            def body(c, carry):
                r = pl.multiple_of(c * chunk, chunk)
                xf = x_refs[0][pl.ds(r, chunk), :].astype(F32)
                ms = jnp.mean(xf * xf, axis=-1, keepdims=True)
                h_s[pl.ds(r, chunk), :] = (xf * lax.rsqrt(ms + EPS) * g_ref[...]).astype(BF)
                return carry
            lax.fori_loop(0, tm // chunk, body, 0)

    def product(wi):
        xi = w_x[wi]
        lhs = h_s[...] if has_gain and xi == 0 else x_refs[xi][...].astype(BF)
        return jnp.dot(lhs, w_refs[wi][...].astype(BF), preferred_element_type=F32)

    epi(_Products(product, len(w_x)), extra_refs, out_refs, j)


class _Products:
    def __init__(self, fn, n):
        self._fn, self._n = fn, n

    def __getitem__(self, i):
        if isinstance(i, slice):
            return [self._fn(k) for k in range(*i.indices(self._n))]
        return self._fn(i)


def _fused_mm(name, xs, ws, *, m, tm, nj, gain=None, extras=(), outs, epi):
    in_arrays, in_specs = [], []
    for arr, k, cb in xs:
        in_arrays.append(arr)
        in_specs.append(pl.BlockSpec((tm, k), lambda i, j, cb=cb: (i, cb)))
    for xi, arr, layer, k, tn, rb, cb0 in ws:
        in_arrays.append(arr)
        col = cb0 if callable(cb0) else (lambda j, cb0=cb0: cb0 + j)
        if layer is None:
            in_specs.append(pl.BlockSpec((k, tn), lambda i, j, rb=rb, col=col: (rb, col(j))))
        else:
            in_specs.append(pl.BlockSpec((None, k, tn),
                                         lambda i, j, l=layer, rb=rb, col=col: (l, rb, col(j))))
    scratch = []
    if gain is not None:
        k0 = xs[0][1]
        in_arrays.append(gain.reshape(1, k0).astype(F32))
        in_specs.append(pl.BlockSpec((1, k0), lambda i, j: (0, 0)))
        scratch.append(pltpu.VMEM((tm, k0), BF))
    for arr, spec in extras:
        in_arrays.append(arr)
        in_specs.append(spec)
    kern = functools.partial(_mm_kernel, nx=len(xs), w_x=tuple(w[0] for w in ws),
                             has_gain=gain is not None, n_extra=len(extras),
                             n_out=len(outs), epi=epi, tm=tm)
    res = pl.pallas_call(
        kern,
        grid=(m // tm, nj),
        in_specs=in_specs,
        out_specs=[o[1] for o in outs],
        out_shape=[o[0] for o in outs],
        scratch_shapes=scratch,
        compiler_params=_params(("arbitrary", "arbitrary")),
        name=name,
    )(*in_arrays)
    return res


def _epi_plain(ds, ex, outs, j):
    acc = ds[0]
    for d in ds[1:]:
        acc = acc + d
    outs[0][...] = acc.astype(outs[0].dtype)


def _epi_residual(ds, ex, outs, j):
    acc = ds[0]
    for d in ds[1:]:
        acc = acc + d
    outs[0][...] = (ex[0][...] + acc).astype(outs[0].dtype)


def _epi_split(ds, ex, outs, j):
    @pl.when(j == 0)
    def _():
        outs[0][...] = ds[0]

    @pl.when(j > 0)
    def _():
        outs[1][...] = ds[0]


def _epi_glu_memq(ds, ex, outs, j, *, ng):
    @pl.when(j < ng)
    def _():
        outs[0][...] = (ds[0] * jax.nn.sigmoid(ds[1])).astype(outs[0].dtype)

    @pl.when(j >= ng)
    def _():
        outs[1][...] = ds[2].astype(outs[1].dtype)


def _rope_tile(t, a_ref, b_ref):
    return t * a_ref[...] + pltpu.roll(t, 32, 1) * b_ref[...]


def _epi_qhead(ds, ex, outs, j):
    acc = ds[0]
    gq_ref, real_ref, a_ref, b_ref = ex
    lo, hi = acc[:, :LANES], acc[:, LANES:]
    sq = lo * lo + hi * hi * real_ref[...]
    sq_hi = sq.astype(BF)
    sq_lo = (sq - sq_hi.astype(F32)).astype(BF)
    ones = jnp.ones((LANES, LANES), BF)
    ssq = (jnp.dot(sq_hi, ones, preferred_element_type=F32)
           + jnp.dot(sq_lo, ones, preferred_element_type=F32))
    inv = lax.rsqrt(ssq * (1.0 / QK_DIM) + EPS)
    outs[0][:, :LANES] = (lo * inv * gq_ref[:, :LANES]).astype(outs[0].dtype)
    outs[0][:, LANES:] = _rope_tile(hi * inv * gq_ref[:, LANES:], a_ref, b_ref).astype(outs[0].dtype)


def _epi_latent(ds, ex, outs, j):
    ck = ds[0]
    glat_ref, gkr_ref, real_ref, a_ref, b_ref = ex
    c_raw = ck[:, :KV_LORA]
    ms = jnp.mean(c_raw * c_raw, axis=-1, keepdims=True)
    outs[0][...] = c_raw * lax.rsqrt(ms + EPS) * glat_ref[...]
    krt = ck[:, KV_LORA:]
    ssq = jnp.sum(krt * krt * real_ref[...], axis=-1, keepdims=True)
    outs[1][...] = _rope_tile(krt * gkr_ref[...], a_ref, b_ref)
    outs[2][...] = jnp.broadcast_to(ssq, outs[2].shape)


def _epi_knope(ds, ex, outs, j):
    acc = ds[0]
    kr_ref, ssq_ref, gkn_ref = ex
    ssq_r = ssq_ref[:, 0:1]
    kr = kr_ref[...]
    lane = lax.broadcasted_iota(jnp.int32, outs[1].shape, 1)

    @pl.when(j == 0)
    def _():
        outs[1][...] = jnp.zeros(outs[1].shape, F32)

    ks_all = outs[1][...]
    for t in range(2):
        kn = acc[:, t * NOPE_DIM:(t + 1) * NOPE_DIM]
        ssq = jnp.sum(kn * kn, axis=-1, keepdims=True) + ssq_r
        ks = lax.rsqrt(ssq * (1.0 / QK_DIM) + EPS)
        outs[0][:, t * HEAD_PAD:t * HEAD_PAD + LANES] = (kn * gkn_ref[...] * ks).astype(outs[0].dtype)
        outs[0][:, t * HEAD_PAD + LANES:(t + 1) * HEAD_PAD] = (kr * ks).astype(outs[0].dtype)
        ks_all = jnp.where(lane == 2 * j + t, ks, ks_all)
    outs[1][...] = ks_all


def _memkv_kernel(x_ref, g_ref, wk_ref, wv_ref, gk_ref, k_out, v_out, h_s):
    j = pl.program_id(1)
    rows = x_ref.shape[0]
    chunk = 128

    @pl.when(j == 0)
    def _():
        def body(c, carry):
            r = pl.multiple_of(c * chunk, chunk)
            xf = x_ref[pl.ds(r, chunk), :]
            ms = jnp.mean(xf * xf, axis=-1, keepdims=True)
            h_s[pl.ds(r, chunk), :] = (xf * lax.rsqrt(ms + EPS) * g_ref[...]).astype(BF)
            return carry
        lax.fori_loop(0, rows // chunk, body, 0)

    h = h_s[...]
    k = jnp.dot(h, wk_ref[...].astype(BF), preferred_element_type=F32)
    ms = jnp.mean(k * k, axis=-1, keepdims=True)
    k_out[...] = k * lax.rsqrt(ms + EPS) * gk_ref[...]
    v_out[...] = jnp.dot(h, wv_ref[...].astype(BF), preferred_element_type=F32)


def _memory_kv_all(mem2d, g_mem, w_mem_kv, g_mk):
    depth = w_mem_kv.shape[0]
    rows = mem2d.shape[0]
    hd = MEM_HEAD_DIM
    out = pl.pallas_call(
        _memkv_kernel,
        grid=(depth, MEM_HEADS),
        in_specs=[
            pl.BlockSpec((rows, D_MODEL), lambda l, j: (0, 0)),
            pl.BlockSpec((None, 1, D_MODEL), lambda l, j: (l, 0, 0)),
            pl.BlockSpec((None, D_MODEL, hd), lambda l, j: (l, 0, j)),
            pl.BlockSpec((None, D_MODEL, hd), lambda l, j: (l, 0, MEM_HEADS + j)),
            pl.BlockSpec((None, 1, hd), lambda l, j: (l, 0, 0)),
        ],
        out_specs=[
            pl.BlockSpec((None, rows, hd), lambda l, j: (l, 0, j)),
            pl.BlockSpec((None, rows, hd), lambda l, j: (l, 0, j)),
        ],
        out_shape=[jax.ShapeDtypeStruct((depth, rows, MEM_W), F32)] * 2,
        scratch_shapes=[pltpu.VMEM((rows, D_MODEL), BF)],
        compiler_params=_params(("arbitrary", "arbitrary")),
        name="memory_kv",
    )(mem2d, g_mem.reshape(depth, 1, D_MODEL), w_mem_kv, w_mem_kv, g_mk.reshape(depth, 1, hd))
    return out


CONV_TT = 256
CONV_RC = 32
CONV_LW = 256
CONV_NORM_RC = 64
CONV_HALO = 32
SUBLANES = 8


def _conv_prompt_kernel(a_ref, w_ref, cb_ref, lg_ref, lb_ref, o_ref, sh, ybuf):
    t = pl.program_id(1)
    tt = CONV_TT
    span = tt + CONV_HALO

    @pl.when(t == 0)
    def _():
        sh[0, 0:CONV_HALO, :] = jnp.zeros((CONV_HALO, D_MODEL), F32)
        sh[0, span:span + SUBLANES, :] = jnp.zeros((SUBLANES, D_MODEL), F32)

    sh[0, CONV_HALO:span, :] = a_ref[...]
    for p in range(1, SUBLANES):
        sh[p, 0:span, :] = sh[0, p:p + span, :]

    groups = CONV_RC // SUBLANES

    def conv_body(c, carry):
        r = c * CONV_RC
        for lq in range(D_MODEL // CONV_LW):
            ls = slice(lq * CONV_LW, (lq + 1) * CONV_LW)
            accs = [jnp.zeros((SUBLANES, CONV_LW), F32) for _ in range(groups)]
            for p in range(SUBLANES):
                slabs = {}
                for m in range(CONV_W // SUBLANES + 2):
                    k = p - 2 + SUBLANES * m
                    if not 0 <= k < CONV_W:
                        continue
                    w = w_ref[k * SUBLANES:(k + 1) * SUBLANES, ls]
                    for g in range(groups):
                        j = m + g
                        if j not in slabs:
                            start = pl.multiple_of(r + SUBLANES * j, SUBLANES)
                            slabs[j] = sh[p, pl.ds(start, SUBLANES), ls]
                        accs[g] = accs[g] + slabs[j] * w
            ybuf[pl.ds(pl.multiple_of(r, CONV_RC), CONV_RC), ls] = jnp.concatenate(accs, axis=0)
        return carry

    lax.fori_loop(0, tt // CONV_RC, conv_body, 0)

    def norm_body(c, carry):
        rows = pl.ds(pl.multiple_of(c * CONV_NORM_RC, CONV_NORM_RC), CONV_NORM_RC)
        y = ybuf[rows, :] + cb_ref[...]
        mu = jnp.mean(y, axis=-1, keepdims=True)
        yc = y - mu
        var = jnp.mean(yc * yc, axis=-1, keepdims=True)
        z = yc * lax.rsqrt(var + EPS) * lg_ref[...] + lb_ref[...]
        o_ref[rows, :] = (z * jax.nn.sigmoid(z)).astype(o_ref.dtype)
        return carry

    lax.fori_loop(0, tt // CONV_NORM_RC, norm_body, 0)
    sh[0, 0:CONV_HALO, :] = sh[0, tt:span, :]


def _conv_prompt(a, w32, cb, lg, lb, batch):
    m = a.shape[0]
    nt = SEQ // CONV_TT
    vec = lambda v: v.reshape(1, D_MODEL)
    w_rep = jnp.repeat(w32, SUBLANES, axis=0)
    return pl.pallas_call(
        _conv_prompt_kernel,
        grid=(batch, nt),
        in_specs=[
            pl.BlockSpec((CONV_TT, D_MODEL), lambda b, t: (b * nt + t, 0)),
            pl.BlockSpec((32 * SUBLANES, D_MODEL), lambda b, t: (0, 0)),
            pl.BlockSpec((1, D_MODEL), lambda b, t: (0, 0)),
            pl.BlockSpec((1, D_MODEL), lambda b, t: (0, 0)),
            pl.BlockSpec((1, D_MODEL), lambda b, t: (0, 0)),
        ],
        out_specs=pl.BlockSpec((CONV_TT, D_MODEL), lambda b, t: (b * nt + t, 0)),
        out_shape=jax.ShapeDtypeStruct((m, D_MODEL), BF),
        scratch_shapes=[pltpu.VMEM((SUBLANES, CONV_TT + CONV_HALO + SUBLANES, D_MODEL), F32),
                        pltpu.VMEM((CONV_TT, D_MODEL), F32)],
        compiler_params=_params(("arbitrary", "arbitrary")),
        name="conv_prompt",
    )(a, w_rep, vec(cb), vec(lg), vec(lb))


CONV_SB = 16


def _conv_sample_kernel(st_ref, a_ref, w_ref, cb_ref, lg_ref, lb_ref, y_ref, ns_ref):
    nstate = CONV_W - 1
    a = a_ref[...]
    acc = a * w_ref[nstate:nstate + 1, :]
    for k in range(nstate):
        acc = acc + st_ref[k] * w_ref[k:k + 1, :]
    y = acc + cb_ref[...]
    mu = jnp.mean(y, axis=-1, keepdims=True)
    yc = y - mu
    var = jnp.mean(yc * yc, axis=-1, keepdims=True)
    z = yc * lax.rsqrt(var + EPS) * lg_ref[...] + lb_ref[...]
    y_ref[...] = (z * jax.nn.sigmoid(z)).astype(y_ref.dtype)
    for k in range(nstate - 1):
        ns_ref[k] = st_ref[k + 1]
    ns_ref[nstate - 1] = a


def _conv_sample(state_t, layer, a, w32, cb, lg, lb):
    nstate, nb = state_t.shape[1], state_t.shape[2]
    vec = lambda v: v.reshape(1, D_MODEL)
    return pl.pallas_call(
        _conv_sample_kernel,
        grid=(nb // CONV_SB,),
        in_specs=[
            pl.BlockSpec((None, nstate, CONV_SB, D_MODEL), lambda i: (layer, 0, i, 0)),
            pl.BlockSpec((CONV_SB, D_MODEL), lambda i: (i, 0)),
            pl.BlockSpec((32, D_MODEL), lambda i: (0, 0)),
            pl.BlockSpec((1, D_MODEL), lambda i: (0, 0)),
            pl.BlockSpec((1, D_MODEL), lambda i: (0, 0)),
            pl.BlockSpec((1, D_MODEL), lambda i: (0, 0)),
        ],
        out_specs=[
            pl.BlockSpec((CONV_SB, D_MODEL), lambda i: (i, 0)),
            pl.BlockSpec((nstate, CONV_SB, D_MODEL), lambda i: (0, i, 0)),
        ],
        out_shape=[jax.ShapeDtypeStruct((nb, D_MODEL), BF),
                   jax.ShapeDtypeStruct((nstate, nb, D_MODEL), F32)],
        compiler_params=_params(("arbitrary",)),
        name="conv_sample",
    )(state_t, a, w32, vec(cb), vec(lg), vec(lb))


MEM_TQ = 512
MEM_SCALE = MEM_HEAD_DIM ** -0.5


def _memattn_prompt_kernel(q_ref, k_ref, v_ref, g_ref, o_ref):
    for h in range(MEM_HEADS):
        sl = slice(h * MEM_HEAD_DIM, (h + 1) * MEM_HEAD_DIM)
        qh = q_ref[:, sl]
        ms = jnp.mean(qh * qh, axis=-1, keepdims=True)
        qn = (qh * lax.rsqrt(ms + EPS) * g_ref[...]).astype(BF)
        s = lax.dot_general(qn, k_ref[:, sl].astype(BF), NT_DIMS,
                            preferred_element_type=F32) * MEM_SCALE
        mx = jnp.max(s, axis=-1, keepdims=True)
        p = jnp.exp(s - mx)
        l = jnp.sum(p, axis=-1, keepdims=True)
        o = jnp.dot(p.astype(BF), v_ref[:, sl].astype(BF), preferred_element_type=F32)
        o_ref[:, sl] = (o / l).astype(o_ref.dtype)


def _memattn_prompt(q, q_colblk, memk, memv, layer, g_mq_l, batch):
    m = q.shape[0]
    nq = SEQ // MEM_TQ
    return pl.pallas_call(
        _memattn_prompt_kernel,
        grid=(batch, nq),
        in_specs=[
            pl.BlockSpec((MEM_TQ, MEM_W), lambda b, i: (b * nq + i, q_colblk)),
            pl.BlockSpec((None, N_MEM, MEM_W), lambda b, i: (layer, b, 0)),
            pl.BlockSpec((None, N_MEM, MEM_W), lambda b, i: (layer, b, 0)),
            pl.BlockSpec((1, MEM_HEAD_DIM), lambda b, i: (0, 0)),
        ],
        out_specs=pl.BlockSpec((MEM_TQ, MEM_W), lambda b, i: (b * nq + i, 0)),
        out_shape=jax.ShapeDtypeStruct((m, MEM_W), BF),
        compiler_params=_params(("arbitrary", "arbitrary")),
        name="memattn_prompt",
    )(q, memk, memv, g_mq_l.reshape(1, MEM_HEAD_DIM))


MEM_SB = 4


MEM_NBUF = 3


def _memattn_sample_kernel(q_ref, g_ref, k_hbm, v_hbm, o_ref, kbuf, vbuf, sem, *, layer):
    i = pl.program_id(0)
    n = pl.num_programs(0)

    def copies(step, slot):
        rows = pl.ds(step * MEM_SB, MEM_SB)
        return (pltpu.make_async_copy(k_hbm.at[layer, rows], kbuf.at[slot], sem.at[0, slot]),
                pltpu.make_async_copy(v_hbm.at[layer, rows], vbuf.at[slot], sem.at[1, slot]))

    @pl.when(i == 0)
    def _():
        for s in range(MEM_NBUF - 1):
            for c in copies(s, s):
                c.start()

    ahead = i + MEM_NBUF - 1

    @pl.when(ahead < n)
    def _():
        for c in copies(ahead, lax.rem(ahead, MEM_NBUF)):
            c.start()

    slot = lax.rem(i, MEM_NBUF)
    for c in copies(i, slot):
        c.wait()

    for s in range(MEM_SB):
        q = q_ref[s]
        ms = jnp.mean(q * q, axis=-1, keepdims=True)
        qn = q * lax.rsqrt(ms + EPS) * g_ref[...]
        sc = jnp.sum(kbuf[slot, s] * qn[None], axis=-1, keepdims=True) * MEM_SCALE
        mx = jnp.max(sc, axis=0, keepdims=True)
        p = jnp.exp(sc - mx)
        l = jnp.sum(p, axis=0)
        o = jnp.sum(p * vbuf[slot, s], axis=0) / l
        o_ref[s] = o.astype(o_ref.dtype)


def _memattn_sample(q3, cmk, cmv, layer, g_mq_l):
    nb = q3.shape[0]
    assert nb // MEM_SB >= MEM_NBUF - 1
    any_spec = pl.BlockSpec(memory_space=pl.ANY)
    buf = pltpu.VMEM((MEM_NBUF, MEM_SB, N_MEM, MEM_HEADS, MEM_HEAD_DIM), F32)
    out = pl.pallas_call(
        functools.partial(_memattn_sample_kernel, layer=layer),
        grid=(nb // MEM_SB,),
        in_specs=[
            pl.BlockSpec((MEM_SB, MEM_HEADS, MEM_HEAD_DIM), lambda i: (i, 0, 0)),
            pl.BlockSpec((1, MEM_HEAD_DIM), lambda i: (0, 0)),
            any_spec, any_spec,
        ],
        out_specs=pl.BlockSpec((MEM_SB, MEM_HEADS, MEM_HEAD_DIM), lambda i: (i, 0, 0)),
        out_shape=jax.ShapeDtypeStruct((nb, MEM_HEADS, MEM_HEAD_DIM), BF),
        scratch_shapes=[buf, buf, pltpu.SemaphoreType.DMA((2, MEM_NBUF))],
        compiler_params=_params(("arbitrary",)),
        name="memattn_sample",
    )(q3, g_mq_l.reshape(1, MEM_HEAD_DIM), cmk, cmv)
    return out.reshape(nb, MEM_W)


FLASH_T = 512
FLASH_TK = 512
FLASH_HP = 2
MLA_SCALE = QK_DIM ** -0.5


def _flash_kernel(q_ref, k_ref, v_ref, o_ref):
    qi = pl.program_id(2)
    t, tk = FLASH_T, FLASH_TK
    ratio = t // tk
    row = lax.broadcasted_iota(jnp.int32, (t, tk), 0)
    col = lax.broadcasted_iota(jnp.int32, (t, tk), 1)

    def tile(ki, carry, hh, diag):
        m, l, acc = carry
        r = pl.multiple_of(ki * tk, tk)
        q = q_ref[:, hh * HEAD_PAD:(hh + 1) * HEAD_PAD]
        k = k_ref[pl.ds(r, tk), hh * HEAD_PAD:(hh + 1) * HEAD_PAD]
        s = lax.dot_general(q, k, NT_DIMS, preferred_element_type=F32) * MLA_SCALE
        if diag is not None:
            s = jnp.where(col + diag * tk <= row, s, NEG_BIG)
        m_new = jnp.maximum(m, jnp.max(s, axis=-1, keepdims=True))
        alpha = jnp.exp(m - m_new)
        p = jnp.exp(s - m_new)
        l = alpha * l + jnp.sum(p, axis=-1, keepdims=True)
        v = v_ref[pl.ds(r, tk), hh * V_DIM:(hh + 1) * V_DIM]
        acc = alpha * acc + jnp.dot(p.astype(BF), v, preferred_element_type=F32)
        return m_new, l, acc

    def tiles(ki, carries, diag):
        return tuple(tile(ki, carries[hh], hh, diag) for hh in range(FLASH_HP))

    init = tuple((jnp.full((t, 1), NEG_BIG, F32), jnp.zeros((t, 1), F32), jnp.zeros((t, V_DIM), F32))
                 for _ in range(FLASH_HP))
    carries = lax.fori_loop(0, qi * ratio, lambda ki, c: tiles(ki, c, None), init)
    for d in range(ratio):
        carries = tiles(qi * ratio + d, carries, d)
    for hh in range(FLASH_HP):
        m, l, acc = carries[hh]
        o_ref[:, hh * V_DIM:(hh + 1) * V_DIM] = (acc / l).astype(o_ref.dtype)


def _mla_prompt_attention(q, kfull, vfull, batch):
    m = q.shape[0]
    nq = SEQ // FLASH_T
    hp = FLASH_HP
    return pl.pallas_call(
        _flash_kernel,
        grid=(batch, MLA_HEADS // hp, nq),
        in_specs=[
            pl.BlockSpec((FLASH_T, hp * HEAD_PAD), lambda b, h, i: (b * nq + i, h)),
            pl.BlockSpec((SEQ, hp * HEAD_PAD), lambda b, h, i: (b, h)),
            pl.BlockSpec((SEQ, hp * V_DIM), lambda b, h, i: (b, h)),
        ],
        out_specs=pl.BlockSpec((FLASH_T, hp * V_DIM), lambda b, h, i: (b * nq + i, h)),
        out_shape=jax.ShapeDtypeStruct((m, MLA_HEADS * V_DIM), BF),
        compiler_params=_params(("arbitrary", "arbitrary", "arbitrary")),
        name="mla_prompt_attention",
    )(q, kfull, vfull)


def _headwise_kernel(x_ref, w_ref, g_ref, o_ref):
    x = (x_ref[...].astype(F32) * g_ref[...]).astype(BF)
    o_ref[...] = jnp.dot(x, w_ref[...].astype(BF), preferred_element_type=F32).astype(o_ref.dtype)


def _headwise_mm(name, x, x_spec, w, w_spec, g, out_shape, out_spec):
    kx = g.shape[-1]
    return pl.pallas_call(
        _headwise_kernel,
        grid=(MLA_HEADS,),
        in_specs=[x_spec, w_spec, pl.BlockSpec((1, kx), lambda h: (0, 0))],
        out_specs=out_spec,
        out_shape=out_shape,
        compiler_params=_params(("arbitrary",)),
        name=name,
    )(x, w, g)


PAGES_PER_STEP = 32


def _paged_kernel(pt_ref, ql_ref, q_ref, cn_ref, krn_ref, ksn_ref, lat_hbm, kr_hbm, ks_hbm, o_ref,
                  m_s, l_s, acc_s, cbuf, krbuf, ksbuf, lat_buf, kr_buf, ks_buf, sem):
    pp = PAGES_PER_STEP
    b = pl.program_id(0)
    c = pl.program_id(1)
    nb = pl.num_programs(0)
    nc = pl.num_programs(1)
    step = b * nc + c
    slot = lax.rem(step, 2)

    def fetch(bb, cc, s):
        for i in range(pp):
            pid = pt_ref[bb, cc * pp + i]
            pltpu.make_async_copy(lat_hbm.at[pid], lat_buf.at[s, i], sem.at[s]).start(priority=i % 2)
            pltpu.make_async_copy(kr_hbm.at[pid], kr_buf.at[s, i], sem.at[s]).start(priority=i % 2)
            pltpu.make_async_copy(ks_hbm.at[pid], ks_buf.at[s, i], sem.at[s]).start(priority=i % 2)

    def wait(s):
        pltpu.make_async_copy(lat_hbm.at[pl.ds(0, pp)], lat_buf.at[s], sem.at[s]).wait()
        pltpu.make_async_copy(kr_hbm.at[pl.ds(0, pp)], kr_buf.at[s], sem.at[s]).wait()
        pltpu.make_async_copy(ks_hbm.at[pl.ds(0, pp)], ks_buf.at[s], sem.at[s]).wait()

    @pl.when(step == 0)
    def _():
        fetch(b, c, 0)

    last = step == nb * nc - 1
    wrap = c == nc - 1
    nb_ = jnp.where(last, b, jnp.where(wrap, b + 1, b))
    nc_ = jnp.where(last, c, jnp.where(wrap, 0, c + 1))
    fetch(nb_, nc_, 1 - slot)
    wait(slot)

    @pl.when(c == 0)
    def _():
        m_s[...] = jnp.full(m_s.shape, NEG_BIG, F32)
        l_s[...] = jnp.zeros(l_s.shape, F32)
        acc_s[...] = jnp.zeros(acc_s.shape, F32)

    ql = ql_ref[0]
    qr32 = q_ref[0][:, LANES:LANES + ROPE_DIM]
    qr = qr32.astype(BF)
    for i in range(pp):
        rows = slice(i * PAGE_SIZE, (i + 1) * PAGE_SIZE)
        cbuf[rows, :] = lat_buf[slot, i].astype(BF)
        krbuf[:, rows] = kr_buf[slot, i].astype(BF)
        ksbuf[:, rows] = ks_buf[slot, i]
    keys = cbuf[...]
    s = (lax.dot_general(ql, keys, NT_DIMS, preferred_element_type=F32)
         + jnp.dot(qr, krbuf[...], preferred_element_type=F32))
    s = s * ksbuf[...] * MLA_SCALE
    m_old = m_s[...]
    m_new = jnp.maximum(m_old, jnp.max(s, axis=-1, keepdims=True))
    alpha = jnp.exp(m_old - m_new)
    p = jnp.exp(s - m_new)
    l_new = alpha * l_s[...] + jnp.sum(p, axis=-1, keepdims=True)
    pv = jnp.dot(p.astype(BF), keys, preferred_element_type=F32)
    acc_new = alpha * acc_s[...] + pv
    m_s[...] = m_new
    l_s[...] = l_new
    acc_s[...] = acc_new

    @pl.when(c == nc - 1)
    def _():
        cn = cn_ref[0]
        krn = krn_ref[0][:, 0:ROPE_DIM]
        s_new = (jnp.sum(ql.astype(F32) * cn, axis=-1, keepdims=True)
                 + jnp.sum(qr32 * krn, axis=-1, keepdims=True)) * ksn_ref[0] * MLA_SCALE
        m2 = jnp.maximum(m_new, s_new)
        a2 = jnp.exp(m_new - m2)
        p2 = jnp.exp(s_new - m2)
        l2 = a2 * l_new + p2
        o_ref[0] = ((a2 * acc_new + p2 * cn) / l2).astype(o_ref.dtype)

    @pl.when(last)
    def _():
        wait(1 - slot)


def _mla_sample_attention(page_table, qlat3, q3, cache_latent, cache_krope_t, cache_kscale_t,
                          cn3, krn3, ksn3):
    nb, n_pages = page_table.shape
    pp = PAGES_PER_STEP
    nc = n_pages // pp
    any_spec = pl.BlockSpec(memory_space=pl.ANY)
    in_specs = [
        pl.BlockSpec((1, MLA_HEADS, KV_LORA), lambda b, c, pt: (b, 0, 0)),
        pl.BlockSpec((1, MLA_HEADS, HEAD_PAD), lambda b, c, pt: (b, 0, 0)),
        pl.BlockSpec((1, 1, KV_LORA), lambda b, c, pt: (b, 0, 0)),
        pl.BlockSpec((1, 1, LANES), lambda b, c, pt: (b, 0, 0)),
        pl.BlockSpec((1, MLA_HEADS, 1), lambda b, c, pt: (b, 0, 0)),
        any_spec, any_spec, any_spec,
    ]
    grid_spec = pltpu.PrefetchScalarGridSpec(
        num_scalar_prefetch=1,
        grid=(nb, nc),
        in_specs=in_specs,
        out_specs=pl.BlockSpec((1, MLA_HEADS, KV_LORA), lambda b, c, pt: (b, 0, 0)),
        scratch_shapes=[pltpu.VMEM((MLA_HEADS, 1), F32), pltpu.VMEM((MLA_HEADS, 1), F32),
                        pltpu.VMEM((MLA_HEADS, KV_LORA), F32),
                        pltpu.VMEM((pp * PAGE_SIZE, KV_LORA), BF), pltpu.VMEM((ROPE_DIM, pp * PAGE_SIZE), BF),
                        pltpu.VMEM((MLA_HEADS, pp * PAGE_SIZE), F32),
                        pltpu.VMEM((2, pp, PAGE_SIZE, KV_LORA), F32),
                        pltpu.VMEM((2, pp, ROPE_DIM, PAGE_SIZE), F32),
                        pltpu.VMEM((2, pp, MLA_HEADS, PAGE_SIZE), F32),
                        pltpu.SemaphoreType.DMA((2,))],
    )
    return pl.pallas_call(
        _paged_kernel,
        grid_spec=grid_spec,
        out_shape=jax.ShapeDtypeStruct((nb, MLA_HEADS, KV_LORA), F32),
        compiler_params=_params(("arbitrary", "arbitrary")),
        name="mla_sample_attention",
    )(page_table, qlat3, q3, cn3, krn3, ksn3, cache_latent, cache_krope_t, cache_kscale_t)


def _router_kernel(xp_ref, xs_ref, g_ref, w_ref, h_ref, gate_ref, info_ref, cnt_ref, cnt_s, *, nbp):
    i = pl.program_id(0)

    @pl.when(i == 0)
    def _():
        cnt_s[...] = jnp.zeros(cnt_s.shape, F32)

    xf = jnp.where(i < nbp, xp_ref[...], xs_ref[...])
    ms = jnp.mean(xf * xf, axis=-1, keepdims=True)
    h = xf * lax.rsqrt(ms + EPS) * g_ref[...]
    hb = lax.bitcast_convert_type(h.astype(BF).astype(F32), jnp.uint32)
    h_ref[...] = (hb[:, :D_MODEL // 2] >> 16) | hb[:, D_MODEL // 2:]
    logits = jnp.dot(h.astype(BF), w_ref[...].astype(BF),
                     preferred_element_type=F32)
    lane = lax.broadcasted_iota(jnp.int32, logits.shape, 1)
    big = jnp.int32(LANES)

    is_g = lane < N_GROUPS
    lg = jnp.where(is_g, logits, NEG_BIG)
    eg = jnp.where(is_g, jnp.exp(lg - jnp.max(lg, axis=-1, keepdims=True)), 0.0)
    pg = eg / jnp.sum(eg, axis=-1, keepdims=True)
    p_grp = jnp.max(pg, axis=-1, keepdims=True)
    grp = jnp.min(jnp.where(is_g & (pg == p_grp), lane, big), axis=-1, keepdims=True)

    e_idx = lane - N_GROUPS
    sel = (e_idx >= 0) & (e_idx < N_EXPERTS) & ((e_idx >> 3) == grp)
    le = jnp.where(sel, logits, NEG_BIG)
    ee = jnp.where(sel, jnp.exp(le - jnp.max(le, axis=-1, keepdims=True)), 0.0)
    pe = ee / jnp.sum(ee, axis=-1, keepdims=True)
    top1 = jnp.max(jnp.where(sel, pe, -1.0), axis=-1, keepdims=True)
    i1 = jnp.min(jnp.where(sel & (pe == top1), lane, big), axis=-1, keepdims=True)
    rest = sel & (lane != i1)
    top2 = jnp.max(jnp.where(rest, pe, -1.0), axis=-1, keepdims=True)
    i2 = jnp.min(jnp.where(rest & (pe == top2), lane, big), axis=-1, keepdims=True)
    denom = top1 + top2
    g1 = p_grp * top1 / denom
    g2 = p_grp * top2 / denom
    gate_ref[...] = jnp.where(lane == 0, g1, jnp.where(lane == 1, g2, 0.0))

    onehot = ((lane == i1) | (lane == i2)).astype(F32)
    r_i = lax.broadcasted_iota(jnp.int32, (MOE_BLOCK, MOE_BLOCK), 0)
    c_i = lax.broadcasted_iota(jnp.int32, (MOE_BLOCK, MOE_BLOCK), 1)
    tri = (c_i < r_i).astype(BF)
    before = jnp.dot(tri, onehot.astype(BF), preferred_element_type=F32) + cnt_s[...]
    rank1 = jnp.sum(jnp.where(lane == i1, before, 0.0), axis=-1, keepdims=True).astype(jnp.int32)
    rank2 = jnp.sum(jnp.where(lane == i2, before, 0.0), axis=-1, keepdims=True).astype(jnp.int32)
    info_ref[...] = jnp.where(lane == 0, i1 - N_GROUPS, jnp.where(lane == 1, i2 - N_GROUPS,
                              jnp.where(lane == 2, rank1, jnp.where(lane == 3, rank2, 0))))
    cnt_new = cnt_s[...] + jnp.sum(onehot, axis=0, keepdims=True)
    cnt_s[...] = cnt_new
    cnt_ref[...] = jnp.broadcast_to(cnt_new, cnt_ref.shape)


def _router(xp, xs, g, w_router):
    nbp = xp.shape[0] // MOE_BLOCK
    assert xs.shape[0] == MOE_BLOCK
    mt = xp.shape[0] + xs.shape[0]
    tok_spec = lambda width: pl.BlockSpec((MOE_BLOCK, width), lambda i: (i, 0))
    return pl.pallas_call(
        functools.partial(_router_kernel, nbp=nbp),
        grid=(nbp + 1,),
        in_specs=[
            pl.BlockSpec((MOE_BLOCK, D_MODEL), lambda i: (jnp.minimum(i, nbp - 1), 0)),
            pl.BlockSpec((MOE_BLOCK, D_MODEL), lambda i: (0, 0)),
            pl.BlockSpec((1, D_MODEL), lambda i: (0, 0)),
            pl.BlockSpec((D_MODEL, LANES), lambda i: (0, 0)),
        ],
        out_specs=[tok_spec(D_MODEL // 2), tok_spec(LANES), tok_spec(LANES),
                   pl.BlockSpec((SUBLANES, LANES), lambda i: (0, 0))],
        out_shape=[jax.ShapeDtypeStruct((mt, D_MODEL // 2), jnp.uint32),
                   jax.ShapeDtypeStruct((mt, LANES), F32),
                   jax.ShapeDtypeStruct((mt, LANES), jnp.int32),
                   jax.ShapeDtypeStruct((SUBLANES, LANES), F32)],
        scratch_shapes=[pltpu.VMEM((1, LANES), F32)],
        compiler_params=_params(("arbitrary",)),
        name="moe_router",
    )(xp, xs, g.reshape(1, D_MODEL), w_router)


TRASH_ROWS = 2 * MOE_BLOCK


def _expert_kernel(be_ref, nu_ref, par_ref, nxt_ref, nch_ref, src_ref, srcn_ref, dst_ref,
                   h_ref, wg_hbm, wu_hbm, wd_hbm, ys_ref,
                   xbuf, ybuf, wg_f, wu_f, wd_f, wg_s, wu_s, wd_s, gsem, ssem, wsem, *, layer, n_slots):
    b = pl.program_id(0)
    nu = nu_ref[0]
    rows = MOE_BLOCK
    nblk = pl.num_programs(0)
    unroll = SUBLANES

    def gather_start(tab_ref, slot, nchunks):
        def body(c, carry):
            for u in range(unroll):
                t = c * unroll + u
                pltpu.make_async_copy(h_ref.at[pl.ds(tab_ref[0, 0, t], 1)], xbuf.at[slot, pl.ds(t, 1)],
                                      gsem.at[slot]).start(priority=u % 2)
            return carry
        lax.fori_loop(0, nchunks, body, 0)

    def gather_wait(slot, nchunks):
        def body(c, carry):
            pltpu.make_async_copy(h_ref.at[pl.ds(0, unroll)], xbuf.at[slot, pl.ds(0, unroll)],
                                  gsem.at[slot]).wait()
            return carry
        lax.fori_loop(0, nchunks, body, 0)

    def weight_copies(e, slot):
        return [pltpu.make_async_copy(w.at[layer, e], f.at[slot], wsem.at[slot])
                for w, f in ((wg_hbm, wg_f), (wu_hbm, wu_f), (wd_hbm, wd_f))]

    def scatter_wait(nchunks):
        def body(c, carry):
            pltpu.make_async_copy(ybuf.at[pl.ds(0, unroll)], ys_ref.at[pl.ds(0, unroll)], ssem).wait()
            return carry
        lax.fori_loop(0, nchunks, body, 0)

    @pl.when(b < nu)
    def _():
        slot = lax.rem(b, 2)

        @pl.when(b == 0)
        def _():
            xbuf[...] = jnp.zeros(xbuf.shape, xbuf.dtype)
            gather_start(src_ref, 0, nch_ref[0])

        @pl.when(b + 1 < nu)
        def _():
            gather_start(srcn_ref, 1 - slot, nch_ref[jnp.minimum(b + 1, nblk - 1)])

        e = be_ref[b]
        p = par_ref[b]
        fresh = (b == 0) | (e != be_ref[jnp.maximum(b - 1, 0)])

        @pl.when(fresh)
        def _():
            @pl.when(b == 0)
            def _():
                for c in weight_copies(e, p):
                    c.start(priority=1)

            for c in weight_copies(e, p):
                c.wait()
            nxt = nxt_ref[b]

            @pl.when(nxt >= 0)
            def _():
                for c in weight_copies(nxt, 1 - p):
                    c.start(priority=1)

            wg_s[...] = wg_f[p].astype(BF)
            wu_s[...] = wu_f[p].astype(BF)
            wd_s[...] = wd_f[p].astype(BF)

        gather_wait(slot, nch_ref[b])
        packed = xbuf[slot]
        x = jnp.concatenate(
            [lax.bitcast_convert_type(packed << 16, F32).astype(BF),
             lax.bitcast_convert_type(packed & jnp.uint32(0xFFFF0000), F32).astype(BF)], axis=1)
        a = jnp.dot(x, wg_s[...], preferred_element_type=F32)
        u = jnp.dot(x, wu_s[...], preferred_element_type=F32)
        hmid = (a * jax.nn.sigmoid(a) * u).astype(BF)

        @pl.when(b >= 1)
        def _():
            scatter_wait(nch_ref[jnp.maximum(b - 1, 0)])

        ybuf[...] = jnp.dot(hmid, wd_s[...], preferred_element_type=F32)

        def scatter_body(c, carry):
            for u in range(unroll):
                t = c * unroll + u
                pltpu.make_async_copy(ybuf.at[pl.ds(t, 1)], ys_ref.at[pl.ds(dst_ref[0, 0, t], 1)],
                                      ssem).start(priority=u % 2)
            return carry
        lax.fori_loop(0, nch_ref[b], scatter_body, 0)

        @pl.when(b == nu - 1)
        def _():
            scatter_wait(nch_ref[b])
            ybuf[...] = jnp.zeros(ybuf.shape, F32)
            for c in range(TRASH_ROWS // rows):
                cp = pltpu.make_async_copy(ybuf, ys_ref.at[pl.ds(n_slots + c * rows, rows)], ssem)
                cp.start()
                cp.wait()


def _experts(blk_e, n_used, par, nxt, nch, row_src3, row_dst3, h, w_gate, w_up, w_down, layer, n_slots):
    nblk = row_src3.shape[0]
    tab_spec = lambda f: pl.BlockSpec((1, 1, MOE_BLOCK), f, memory_space=pltpu.SMEM)
    any_spec = pl.BlockSpec(memory_space=pl.ANY)
    grid_spec = pltpu.PrefetchScalarGridSpec(
        num_scalar_prefetch=5,
        grid=(nblk,),
        in_specs=[
            tab_spec(lambda b, *_: (b, 0, 0)),
            tab_spec(lambda b, *_: (jnp.minimum(b + 1, nblk - 1), 0, 0)),
            tab_spec(lambda b, *_: (b, 0, 0)),
            any_spec, any_spec, any_spec, any_spec,
        ],
        out_specs=any_spec,
        scratch_shapes=[
            pltpu.VMEM((2, MOE_BLOCK, D_MODEL // 2), jnp.uint32), pltpu.VMEM((MOE_BLOCK, D_MODEL), F32),
            pltpu.VMEM((2, D_MODEL, D_EXPERT), F32), pltpu.VMEM((2, D_MODEL, D_EXPERT), F32),
            pltpu.VMEM((2, D_EXPERT, D_MODEL), F32),
            pltpu.VMEM((D_MODEL, D_EXPERT), BF), pltpu.VMEM((D_MODEL, D_EXPERT), BF),
            pltpu.VMEM((D_EXPERT, D_MODEL), BF),
            pltpu.SemaphoreType.DMA((2,)), pltpu.SemaphoreType.DMA(()), pltpu.SemaphoreType.DMA((2,)),
        ],
    )
    return pl.pallas_call(
        functools.partial(_expert_kernel, layer=layer, n_slots=n_slots),
        grid_spec=grid_spec,
        out_shape=jax.ShapeDtypeStruct((n_slots + TRASH_ROWS, D_MODEL), F32),
        compiler_params=_params(("arbitrary",)),
        name="moe_experts",
    )(blk_e, n_used, par, nxt, nch, row_src3, row_src3, row_dst3, h, w_gate, w_up, w_down)


def _combine_kernel(gate_ref, xp_ref, xs_ref, y0_ref, y1_ref, op_ref, os_ref, *, nbp):
    i = pl.program_id(0)
    y = y0_ref[...] * gate_ref[:, 0:1] + y1_ref[...] * gate_ref[:, 1:2]

    @pl.when(i < nbp)
    def _():
        op_ref[...] = xp_ref[...] + y

    @pl.when(i == nbp)
    def _():
        os_ref[...] = xs_ref[...] + y


def _combine(gate, xp, xs, ys):
    nbp = xp.shape[0] // MOE_BLOCK
    nb_tok = nbp + 1
    p_spec = pl.BlockSpec((MOE_BLOCK, D_MODEL), lambda i: (jnp.minimum(i, nbp - 1), 0))
    s_spec = pl.BlockSpec((MOE_BLOCK, D_MODEL), lambda i: (0, 0))
    return pl.pallas_call(
        functools.partial(_combine_kernel, nbp=nbp),
        grid=(nb_tok,),
        in_specs=[
            pl.BlockSpec((MOE_BLOCK, LANES), lambda i: (i, 0)),
            p_spec, s_spec,
            pl.BlockSpec((MOE_BLOCK, D_MODEL), lambda i: (i, 0)),
            pl.BlockSpec((MOE_BLOCK, D_MODEL), lambda i: (nb_tok + i, 0)),
        ],
        out_specs=[p_spec, s_spec],
        out_shape=[jax.ShapeDtypeStruct(xp.shape, F32), jax.ShapeDtypeStruct(xs.shape, F32)],
        compiler_params=_params(("arbitrary",)),
        name="moe_combine",
    )(gate, xp, xs, ys, ys)


def _moe(xp, xs, g_ffn_l, w_router, w_gate, w_up, w_down, layer):
    mt = xp.shape[0] + xs.shape[0]
    n_slots = 2 * mt
    nblk = (n_slots + N_EXPERTS * (MOE_BLOCK - 1) + MOE_BLOCK - 1) // MOE_BLOCK
    h, gate, info, cnt = _router(xp, xs, g_ffn_l, w_router)

    ids = jnp.arange(N_EXPERTS, dtype=jnp.int32)
    counts = cnt[0, N_GROUPS:N_GROUPS + N_EXPERTS].astype(jnp.int32)
    used = counts > 0
    padded = (counts + MOE_BLOCK - 1) // MOE_BLOCK * MOE_BLOCK
    pad_ends = jnp.cumsum(padded)
    pad_starts = pad_ends - padded
    n_used = (pad_ends[-1] // MOE_BLOCK).astype(jnp.int32)
    blk_ids = jnp.arange(nblk, dtype=jnp.int32)
    blk_e = jnp.sum((pad_ends[None, :] <= blk_ids[:, None] * MOE_BLOCK).astype(jnp.int32), axis=1)
    last_e = jnp.max(jnp.where(used, ids, 0))
    blk_e = jnp.where(blk_ids < n_used, jnp.minimum(blk_e, N_EXPERTS - 1), last_e).astype(jnp.int32)
    ordinal = jnp.cumsum(used.astype(jnp.int32)) - 1
    next_e = jnp.min(jnp.where((ids[None, :] > ids[:, None]) & used[None, :], ids[None, :], N_EXPERTS), axis=1)
    next_e = jnp.where(next_e == N_EXPERTS, -1, next_e)
    blk_onehot = blk_e[:, None] == ids[None, :]
    par = jnp.sum(jnp.where(blk_onehot, (ordinal % 2)[None, :], 0), axis=1).astype(jnp.int32)
    nxt = jnp.sum(jnp.where(blk_onehot, next_e[None, :], 0), axis=1).astype(jnp.int32)
    seg_end = jnp.sum(jnp.where(blk_onehot, (pad_starts + counts)[None, :], 0), axis=1)
    rows_valid = jnp.where(blk_ids < n_used, jnp.clip(seg_end - blk_ids * MOE_BLOCK, 0, MOE_BLOCK), 0)
    nch = ((rows_valid + SUBLANES - 1) // SUBLANES).astype(jnp.int32)

    e2, rank2 = info[:, 0:2], info[:, 2:4]
    start2 = jnp.sum(jnp.where(e2[:, :, None] == ids[None, None, :], pad_starts[None, None, :], 0), axis=-1)
    dest = start2 + rank2
    slot_code = jnp.arange(mt, dtype=jnp.int32)[:, None] + jnp.array([[0, mt]], jnp.int32)
    rows = jnp.arange(nblk * MOE_BLOCK, dtype=jnp.int32)
    row_dst = (n_slots + rows % TRASH_ROWS).at[dest.reshape(-1)].set(slot_code.reshape(-1))
    row_src = jnp.where(row_dst < n_slots, row_dst % mt, 0)
    shape3 = (nblk, 1, MOE_BLOCK)

    ys = _experts(blk_e, n_used.reshape(1), par, nxt, nch, row_src.reshape(shape3), row_dst.reshape(shape3),
                  h, w_gate, w_up, w_down, layer, n_slots)
    return _combine(gate, xp, xs, ys)


def _rope_tables(pos):
    half = ROPE_DIM // 2
    inv = ROPE_THETA ** (-jnp.arange(half, dtype=F32) / half)
    ang = pos.astype(F32)[:, None] * inv[None, :]
    cos, sin = jnp.cos(ang), jnp.sin(ang)
    z = jnp.zeros_like(cos)
    a = jnp.concatenate([cos, cos, z, z], axis=1)
    b = jnp.concatenate([-sin, sin, z, z], axis=1)
    return a, b


def _out_spec(tm, tn):
    return pl.BlockSpec((tm, tn), lambda i, j: (i, j))


def _table_specs(tabs, tm, nper):
    return [(t, pl.BlockSpec((tm, LANES), lambda i, j, nper=nper: (i % nper, 0))) for t in tabs]


def _vec_extra(v):
    n = v.shape[-1]
    return (v.reshape(1, n), pl.BlockSpec((1, n), lambda i, j: (0, 0)))


def kernel(x_prompt, x_sample, cache_latent, cache_krope, cache_kscale, cache_mem_k, cache_mem_v, state_conv, page_table, mem_prompt, g_mix, g_ffn, g_mem, w_mem_kv, g_mq, g_mk, w_in_a, conv_w, conv_b, ln_g, ln_b, w_out_a, g_kv_in, w_kv_down, g_kv_lat, w_uk, w_uv, g_k, w_in_b, g_qlat, w_q_up, g_q, w_out_b, w_rg, w_re, w_gate, w_up, w_down):
    batch, seq, _ = x_prompt.shape
    nb = x_sample.shape[0]
    depth = g_mix.shape[0]
    n_a = w_in_a.shape[0]
    mp = batch * seq
    tmp = 1024
    xp = x_prompt.reshape(mp, D_MODEL)
    xs = x_sample.reshape(nb, D_MODEL)

    tabs_p = _rope_tables(jnp.arange(seq))
    tabs_s = _rope_tables(jnp.full((nb,), PAST_LEN))
    w_kvd = jnp.concatenate([w_kv_down, w_kv_down[:, KV_LORA:]], axis=1)
    gk_n = g_k[:NOPE_DIM]
    gk_r = jnp.concatenate([g_k[NOPE_DIM:], g_k[NOPE_DIM:]])
    real_lanes = jnp.concatenate([jnp.ones((ROPE_DIM,), F32), jnp.zeros((LANES - ROPE_DIM,), F32)])
    w_uk2 = w_uk.reshape(KV_LORA, MLA_HEADS * NOPE_DIM)
    w_uk_t = jnp.transpose(w_uk, (1, 2, 0))
    w_uv2 = w_uv.reshape(KV_LORA, MLA_HEADS * V_DIM)
    w_qu = jnp.concatenate([w_q_up, w_q_up[..., NOPE_DIM:]], axis=-1)
    w_qu = w_qu.reshape(w_q_up.shape[0], Q_LORA, MLA_HEADS * HEAD_PAD)
    gq_pad = jnp.concatenate([g_q, g_q[:, NOPE_DIM:]], axis=-1)
    w_router = jnp.pad(jnp.concatenate([w_rg, w_re], axis=-1),
                       ((0, 0), (0, 0), (0, LANES - N_GROUPS - N_EXPERTS)))
    conv_w32 = jnp.pad(conv_w, ((0, 0), (0, 32 - CONV_W), (0, 0)))
    cache_kscale_t = jnp.transpose(cache_kscale, (0, 2, 1))
    cache_krope_t = jnp.transpose(cache_krope, (0, 2, 1))
    state_conv_t = jnp.transpose(state_conv, (0, 2, 1, 3))

    memk, memv = _memory_kv_all(mem_prompt.reshape(batch * N_MEM, D_MODEL), g_mem, w_mem_kv, g_mk)

    conv_p_list, conv_s_list = [], []
    shared = None
    for l in range(depth):
        if l < n_a:
            n_glu = D_MODEL
            outs_p, outs_s = [], []
            for x, m, tm, dst in ((xp, mp, tmp, outs_p), (xs, nb, nb, outs_s)):
                tn = 256
                ng = n_glu // tn
                glu_col = lambda j, ng=ng: jnp.minimum(j, ng - 1)
                a, qm = _fused_mm(
                    "in_a", [(x, D_MODEL, 0)],
                    [(0, w_in_a, l, D_MODEL, tn, 0, glu_col),
                     (0, w_in_a, l, D_MODEL, tn, 0, lambda j, ng=ng: ng + jnp.minimum(j, ng - 1)),
                     (0, w_in_a, l, D_MODEL, tn, 0, lambda j, ng=ng: 2 * ng + jnp.maximum(j - ng, 0))],
                    m=m, tm=tm, nj=ng + MEM_W // tn, gain=g_mix[l],
                    outs=[(jax.ShapeDtypeStruct((m, n_glu), F32),
                           pl.BlockSpec((tm, tn), lambda i, j, ng=ng: (i, jnp.minimum(j, ng - 1)))),
                          (jax.ShapeDtypeStruct((m, MEM_W), F32),
                           pl.BlockSpec((tm, tn), lambda i, j, ng=ng: (i, jnp.maximum(j - ng, 0))))],
                    epi=functools.partial(_epi_glu_memq, ng=ng))
                dst.extend([a, qm])
            a_p, qm_p = outs_p
            a_s, qm_s = outs_s
            yc_p = _conv_prompt(a_p, conv_w32[l], conv_b[l], ln_g[l], ln_b[l], batch)
            conv_p_list.append(a_p.reshape(batch, seq, D_MODEL)[:, seq - (CONV_W - 1):])
            yc_s, ns = _conv_sample(state_conv_t, l, a_s, conv_w32[l], conv_b[l], ln_g[l], ln_b[l])
            conv_s_list.append(ns)
            ym_p = _memattn_prompt(qm_p, 0, memk, memv, l, g_mq[l], batch)
            ym_s = _memattn_sample(qm_s.reshape(nb, MEM_HEADS, MEM_HEAD_DIM), cache_mem_k, cache_mem_v,
                                   l, g_mq[l])
            new = []
            for x, m, tm, y1, y2 in ((xp, mp, tmp, yc_p, ym_p), (xs, nb, nb, yc_s, ym_s)):
                tn = 512
                new.append(_fused_mm(
                    "out_a", [(y1, D_MODEL, 0), (y2, MEM_W, 0)],
                    [(0, w_out_a, l, D_MODEL, tn, 0, 0), (1, w_out_a, l, MEM_W, tn, D_MODEL // MEM_W, 0)],
                    m=m, tm=tm, nj=D_MODEL // tn,
                    extras=[(x, _out_spec(tm, tn))],
                    outs=[(jax.ShapeDtypeStruct((m, D_MODEL), F32), _out_spec(tm, tn))],
                    epi=_epi_residual)[0])
            xp, xs = new
        else:
            jb = l - n_a
            kfull, vfull, c_s, kr_s, ks_s = shared
            att = []
            for x, m, tm, tabs, nper, qdt in ((xp, mp, tmp, tabs_p, seq // tmp, BF),
                                              (xs, nb, nb, tabs_s, 1, F32)):
                tn = Q_LORA
                n_in = Q_LORA + MEM_W
                u, qm = _fused_mm(
                    "in_b", [(x, D_MODEL, 0)], [(0, w_in_b, jb, D_MODEL, tn, 0, 0)],
                    m=m, tm=tm, nj=n_in // tn, gain=g_mix[l],
                    outs=[(jax.ShapeDtypeStruct((m, Q_LORA), F32),
                           pl.BlockSpec((tm, tn), lambda i, j: (i, 0))),
                          (jax.ShapeDtypeStruct((m, MEM_W), F32),
                           pl.BlockSpec((tm, tn), lambda i, j: (i, jnp.maximum(j - 1, 0))))],
                    epi=_epi_split)
                qh = _fused_mm(
                    "q_up", [(u, Q_LORA, 0)], [(0, w_qu, jb, Q_LORA, HEAD_PAD, 0, 0)],
                    m=m, tm=tm, nj=MLA_HEADS, gain=g_qlat[jb],
                    extras=[_vec_extra(gq_pad[jb]), _vec_extra(real_lanes)] + _table_specs(tabs, tm, nper),
                    outs=[(jax.ShapeDtypeStruct((m, MLA_HEADS * HEAD_PAD), qdt), _out_spec(tm, HEAD_PAD))],
                    epi=_epi_qhead)[0]
                att.append((qm, qh))
            (qm_p, qh_p), (qm_s, qh_s) = att
            att_p = _mla_prompt_attention(qh_p, kfull, vfull, batch)
            ym_p = _memattn_prompt(qm_p, 0, memk, memv, l, g_mq[l], batch)
            ym_s = _memattn_sample(qm_s.reshape(nb, MEM_HEADS, MEM_HEAD_DIM), cache_mem_k, cache_mem_v,
                                   l, g_mq[l])

            qlat = _headwise_mm(
                "q_latent", qh_s, pl.BlockSpec((nb, NOPE_DIM), lambda h: (0, 2 * h)),
                w_uk_t, pl.BlockSpec((None, NOPE_DIM, KV_LORA), lambda h: (h, 0, 0)),
                gk_n.reshape(1, NOPE_DIM),
                jax.ShapeDtypeStruct((nb, MLA_HEADS * KV_LORA), BF),
                pl.BlockSpec((nb, KV_LORA), lambda h: (0, h)))
            o_lat = _mla_sample_attention(
                page_table, qlat.reshape(nb, MLA_HEADS, KV_LORA),
                qh_s.reshape(nb, MLA_HEADS, HEAD_PAD), cache_latent, cache_krope_t, cache_kscale_t,
                c_s.reshape(nb, 1, KV_LORA), kr_s.reshape(nb, 1, LANES),
                ks_s[:, :MLA_HEADS].reshape(nb, MLA_HEADS, 1))
            att_s = _headwise_mm(
                "v_expand", o_lat.reshape(nb, MLA_HEADS * KV_LORA),
                pl.BlockSpec((nb, KV_LORA), lambda h: (0, h)),
                w_uv2, pl.BlockSpec((KV_LORA, V_DIM), lambda h: (0, h)),
                jnp.ones((1, KV_LORA), F32),
                jax.ShapeDtypeStruct((nb, MLA_HEADS * V_DIM), BF),
                pl.BlockSpec((nb, V_DIM), lambda h: (0, h)))
            new = []
            n_att = MLA_HEADS * V_DIM
            for x, m, tm, y1, y2 in ((xp, mp, tmp, att_p, ym_p), (xs, nb, nb, att_s, ym_s)):
                tn = 512
                new.append(_fused_mm(
                    "out_b", [(y1, n_att, 0), (y2, MEM_W, 0)],
                    [(0, w_out_b, jb, n_att, tn, 0, 0), (1, w_out_b, jb, MEM_W, tn, n_att // MEM_W, 0)],
                    m=m, tm=tm, nj=D_MODEL // tn,
                    extras=[(x, _out_spec(tm, tn))],
                    outs=[(jax.ShapeDtypeStruct((m, D_MODEL), F32), _out_spec(tm, tn))],
                    epi=_epi_residual)[0])
            xp, xs = new

        xp, xs = _moe(xp, xs, g_ffn[l], w_router[l], w_gate, w_up, w_down, l)

        if l == n_a - 1:
            lat = []
            for x, m, tm, tabs, nper in ((xp, mp, tmp, tabs_p, seq // tmp), (xs, nb, nb, tabs_s, 1)):
                nck = KV_LORA + LANES
                c, kr128, ssq128 = _fused_mm(
                    "kv_latent", [(x, D_MODEL, 0)], [(0, w_kvd, None, D_MODEL, nck, 0, 0)],
                    m=m, tm=tm, nj=1, gain=g_kv_in,
                    extras=[_vec_extra(g_kv_lat), _vec_extra(gk_r), _vec_extra(real_lanes)]
                    + _table_specs(tabs, tm, nper),
                    outs=[(jax.ShapeDtypeStruct((m, KV_LORA), F32), _out_spec(tm, KV_LORA)),
                          (jax.ShapeDtypeStruct((m, LANES), F32), pl.BlockSpec((tm, LANES), lambda i, j: (i, 0))),
                          (jax.ShapeDtypeStruct((m, LANES), F32), pl.BlockSpec((tm, LANES), lambda i, j: (i, 0)))],
                    epi=_epi_latent)
                kf, ks128 = _fused_mm(
                    "k_nope", [(c, KV_LORA, 0)], [(0, w_uk2, None, KV_LORA, 2 * NOPE_DIM, 0, 0)],
                    m=m, tm=tm, nj=MLA_HEADS // 2,
                    extras=[(kr128, pl.BlockSpec((tm, LANES), lambda i, j: (i, 0))),
                            (ssq128, pl.BlockSpec((tm, LANES), lambda i, j: (i, 0))),
                            _vec_extra(gk_n)],
                    outs=[(jax.ShapeDtypeStruct((m, MLA_HEADS * HEAD_PAD), BF), _out_spec(tm, 2 * HEAD_PAD)),
                          (jax.ShapeDtypeStruct((m, LANES), F32), pl.BlockSpec((tm, LANES), lambda i, j: (i, 0)))],
                    epi=_epi_knope)
                lat.append((c, kr128, ks128, kf))
            (c_p, kr_p, ks_p, kfull), (c_s, kr_s, ks_s, _) = lat
            vfull = _fused_mm(
                "v_full", [(c_p, KV_LORA, 0)], [(0, w_uv2, None, KV_LORA, 512, 0, 0)],
                m=mp, tm=tmp, nj=MLA_HEADS * V_DIM // 512,
                outs=[(jax.ShapeDtypeStruct((mp, MLA_HEADS * V_DIM), BF), _out_spec(tmp, 512))],
                epi=_epi_plain)[0]
            shared = (kfull, vfull, c_s, kr_s, ks_s)

    shape5 = (depth, batch, N_MEM, MEM_HEADS, MEM_HEAD_DIM)
    return (xp.reshape(batch, seq, D_MODEL), xs.reshape(nb, 1, D_MODEL),
            c_p.reshape(batch, seq, KV_LORA), kr_p[:, :ROPE_DIM].reshape(batch, seq, ROPE_DIM),
            ks_p[:, :MLA_HEADS].reshape(batch, seq, MLA_HEADS),
            memk.reshape(shape5), memv.reshape(shape5),
            jnp.stack(conv_p_list, axis=0),
            c_s.reshape(nb, 1, KV_LORA), kr_s[:, :ROPE_DIM].reshape(nb, 1, ROPE_DIM),
            ks_s[:, :MLA_HEADS].reshape(nb, 1, MLA_HEADS),
            jnp.transpose(jnp.stack(conv_s_list, axis=0), (0, 2, 1, 3)))
```
